```python
import math
import jax, jax.numpy as jnp
from jax import lax
import numpy as np

D_MODEL = 1024
BATCH = 16
SEQ = 256
DEPTH = 4
DEC_BATCH = 8
DEC_SEQ = 1024
PAST_LEN = 256

GRID_W = 64
N_MIXERS = 3
N_POOL = (DEPTH + 2) // 3
N_MLA = (DEPTH + 1) // 3
N_LRU = DEPTH // 3
POOL_WINDOWS = (2, 4, 8, 16)
POOL_GROUPS = 4
POOL_GC = D_MODEL // POOL_GROUPS
MLA_HEADS = 8
Q_LORA = 384
KV_LORA = 256
QK_NOPE = 128
QK_ROPE = 64
V_HEAD = 128
ROPE_NF = QK_ROPE // 4
ROPE_THETA = 10000.0
MLA_SCALE = (QK_NOPE + QK_ROPE) ** -0.5
DENSE_KEY_LIMIT = 2048
Q_BLOCK = 128
D_RNN = D_MODEL
LRU_BLOCKS = 8
LRU_BW = D_RNN // LRU_BLOCKS
CONV_W = 4
CONV_LEFT = 1
LRU_C = 8.0
D_FF = ((8 * D_MODEL // 3 + 255) // 256) * 256
ALPHA = (2.0 * DEPTH) ** 0.25
BETA = (8.0 * DEPTH) ** -0.25
EPS = 1e-6

kernel_name = 'hybrid_pool_mla_rglru_diffusion_step'


def layer_norm(x, g, b):
    x32 = x.astype(jnp.float32)
    mu = jnp.mean(x32, axis=-1, keepdims=True)
    var = jnp.mean(jnp.square(x32 - mu), axis=-1, keepdims=True)
    y = (x32 - mu) * lax.rsqrt(var + EPS)
    return (y * g.astype(jnp.float32) + b.astype(jnp.float32)).astype(x.dtype)


def rms_norm(x, g):
    x32 = x.astype(jnp.float32)
    y = x32 * lax.rsqrt(jnp.mean(jnp.square(x32), axis=-1, keepdims=True) + EPS)
    return (y * g.astype(jnp.float32)).astype(x.dtype)


def modulation(cond, w_ada, b_ada):
    m = (jax.nn.silu(cond) @ w_ada + b_ada)[:, None, :]
    return jnp.split(m, 6, axis=-1)


def swiglu(h, w_in, w_out):
    a, b = jnp.split(h @ w_in, 2, axis=-1)
    return (jax.nn.silu(a) * b) @ w_out


def rope_tables(T):
    t = jnp.arange(T)
    rows = (t // GRID_W).astype(jnp.float32)
    cols = (t % GRID_W).astype(jnp.float32)
    inv = ROPE_THETA ** (-jnp.arange(ROPE_NF, dtype=jnp.float32) / ROPE_NF)
    ang = jnp.stack([rows[:, None] * inv, cols[:, None] * inv], axis=1)
    return jnp.cos(ang), jnp.sin(ang)


def apply_rope(x, cos, sin):
    xr = x.reshape(x.shape[:-1] + (2, 2, ROPE_NF)).astype(jnp.float32)
    x1, x2 = xr[..., 0, :], xr[..., 1, :]
    out = jnp.stack([x1 * cos - x2 * sin, x2 * cos + x1 * sin], axis=-2)
    return out.reshape(x.shape).astype(x.dtype)


def pool_mix(x, w_pool, scale):
    B, T, D = x.shape
    x32 = x.astype(jnp.float32)
    csum = jnp.concatenate([jnp.zeros((B, 1, D), jnp.float32), jnp.cumsum(x32, axis=1)], axis=1)
    t = jnp.arange(T)
    groups = []
    for g, w in enumerate(POOL_WINDOWS):
        lo = jnp.clip(t - w // 2, 0, T)
        hi = jnp.clip(t + w - w // 2, 0, T)
        cg = csum[..., g * POOL_GC:(g + 1) * POOL_GC]
        cnt = (hi - lo).astype(jnp.float32)[:, None]
        groups.append((jnp.take(cg, hi, axis=1) - jnp.take(cg, lo, axis=1)) / cnt)
    pooled = jnp.stack(groups, axis=2) - x32.reshape(B, T, POOL_GROUPS, POOL_GC)
    out = jnp.einsum('btgc,gcd->btgd', pooled.astype(x.dtype), w_pool).reshape(B, T, D)
    return out * scale


def mla_project(x, w_dq, g_q, w_uq, w_dkv, g_kv):
    q = jnp.einsum('btr,rhe->bthe', rms_norm(x @ w_dq, g_q), w_uq)
    kv = x @ w_dkv
    c_kv = rms_norm(kv[..., :KV_LORA], g_kv)
    return q[..., :QK_NOPE], q[..., QK_NOPE:], c_kv, kv[..., KV_LORA:]


def mla_attend(q_nope, q_rope, c_kv, k_rope, w_uk, w_uv, w_o):
    B, T = q_nope.shape[:2]
    k_nope = jnp.einsum('bsr,rhe->bshe', c_kv, w_uk)
    v = jnp.einsum('bsr,rhe->bshe', c_kv, w_uv)

    def attend(qn, qr):
        s = jnp.einsum('bthe,bshe->bhts', qn, k_nope) + jnp.einsum('bthe,bse->bhts', qr, k_rope)
        p = jax.nn.softmax(s.astype(jnp.float32) * MLA_SCALE, axis=-1).astype(v.dtype)
        return jnp.einsum('bhts,bshe->bthe', p, v)

    if k_nope.shape[1] >= DENSE_KEY_LIMIT and T % Q_BLOCK == 0:
        nb = T // Q_BLOCK
        blk = lambda a: jnp.moveaxis(a.reshape((B, nb, Q_BLOCK) + a.shape[2:]), 1, 0)
        o = lax.map(lambda qs: attend(qs[0], qs[1]), (blk(q_nope), blk(q_rope)))
        o = jnp.moveaxis(o, 0, 1).reshape(B, T, MLA_HEADS, V_HEAD)
    else:
        o = attend(q_nope, q_rope)
    return o.reshape(B, T, MLA_HEADS * V_HEAD) @ w_o


def linear_scan(a, b, h0, reverse):
    if reverse:
        b = b.at[:, -1].add(a[:, -1] * h0)
    else:
        b = b.at[:, 0].add(a[:, 0] * h0)
    combine = lambda e1, e2: (e1[0] * e2[0], e2[0] * e1[1] + e2[1])
    _, h = lax.associative_scan(combine, (a, b), axis=1, reverse=reverse)
    return h


def lru_mixer(x, w_in, conv_w, conv_b, w_a, b_a, w_i, b_i, lam, w_out, h0_f, h0_b):
    B, T, _ = x.shape
    xy = x @ w_in
    u, y = xy[..., :D_RNN], jax.nn.gelu(xy[..., D_RNN:])
    up = jnp.pad(u, ((0, 0), (CONV_LEFT, CONV_W - 1 - CONV_LEFT), (0, 0)))
    u = sum(up[:, k:k + T] * conv_w[k] for k in range(CONV_W)) + conv_b
    u32 = u.astype(jnp.float32)
    ub = u32.reshape(B, T, LRU_BLOCKS, LRU_BW)
    r = jax.nn.sigmoid(jnp.einsum('btnc,dncm->dbtnm', ub, w_a.astype(jnp.float32)).reshape(2, B, T, D_RNN)
                       + b_a.astype(jnp.float32)[:, None, None])
    ig = jax.nn.sigmoid(jnp.einsum('btnc,dncm->dbtnm', ub, w_i.astype(jnp.float32)).reshape(2, B, T, D_RNN)
                        + b_i.astype(jnp.float32)[:, None, None])
    log_a = -LRU_C * r * jax.nn.softplus(-lam.astype(jnp.float32))[:, None, None]
    a = jnp.exp(log_a)
    bterm = jnp.sqrt(-jnp.expm1(2.0 * log_a)) * (ig * u32[None])
    h_f = linear_scan(a[0], bterm[0], h0_f.astype(jnp.float32), reverse=False)
    h_b = linear_scan(a[1], bterm[1], h0_b.astype(jnp.float32), reverse=True)
    out = ((h_f + h_b).astype(x.dtype) * y) @ w_out
    return out, h_f[:, -1].astype(x.dtype), h_b[:, 0].astype(x.dtype)


def setup_inputs(seed: int = 0) -> dict:
    key = jax.random.key(seed)
    ks = iter(jax.random.split(key, 40))
    nrm = lambda shape, s=1.0: jax.random.normal(next(ks), shape, jnp.float32) * s
    d = D_MODEL
    u = jax.random.uniform(next(ks), (N_LRU, 2, D_RNN), jnp.float32, 0.9, 0.999)
    a0 = u ** (1.0 / LRU_C)
    return {
        'x_prompt': nrm((BATCH, SEQ, d)),
        'x_sample': nrm((DEC_BATCH, DEC_SEQ, d)),
        'cache_mla_ckv': nrm((DEC_BATCH, N_MLA, PAST_LEN, KV_LORA)),
        'cache_mla_krope': nrm((DEC_BATCH, N_MLA, PAST_LEN, QK_ROPE)),
        'state_lru': nrm((DEC_BATCH, N_LRU, 2, D_RNN), 0.5),
        'c': nrm((DEC_BATCH, d)),
        'c_ctx': nrm((d,)),
        'w_ada': nrm((DEPTH, d, 6 * d), d ** -0.5),
        'b_ada': nrm((DEPTH, 6 * d), 0.02),
        'ln_g': 1.0 + nrm((DEPTH, 2, d), 0.02),
        'ln_b': nrm((DEPTH, 2, d), 0.02),
        'w_ffn_in': nrm((DEPTH, d, 2 * D_FF), d ** -0.5),
        'w_ffn_out': nrm((DEPTH, D_FF, d), BETA * D_FF ** -0.5),
        'w_pool': nrm((N_POOL, POOL_GROUPS, POOL_GC, POOL_GC), BETA * POOL_GC ** -0.5),
        'pool_scale': 1.0 + nrm((N_POOL, d), 0.02),
        'w_dq': nrm((N_MLA, d, Q_LORA), d ** -0.5),
        'g_q': 1.0 + nrm((N_MLA, Q_LORA), 0.02),
        'w_uq': nrm((N_MLA, Q_LORA, MLA_HEADS, QK_NOPE + QK_ROPE), Q_LORA ** -0.5),
        'w_dkv': nrm((N_MLA, d, KV_LORA + QK_ROPE), d ** -0.5),
        'g_kv': 1.0 + nrm((N_MLA, KV_LORA), 0.02),
        'w_uk': nrm((N_MLA, KV_LORA, MLA_HEADS, QK_NOPE), KV_LORA ** -0.5),
        'w_uv': nrm((N_MLA, KV_LORA, MLA_HEADS, V_HEAD), KV_LORA ** -0.5),
        'w_mla_o': nrm((N_MLA, MLA_HEADS * V_HEAD, d), BETA * (MLA_HEADS * V_HEAD) ** -0.5),
        'w_lru_in': nrm((N_LRU, d, 2 * D_RNN), d ** -0.5),
        'lru_conv_w': nrm((N_LRU, CONV_W, D_RNN), CONV_W ** -0.5),
        'lru_conv_b': nrm((N_LRU, D_RNN), 0.02),
        'w_lru_a': nrm((N_LRU, 2, LRU_BLOCKS, LRU_BW, LRU_BW), LRU_BW ** -0.5),
        'b_lru_a': nrm((N_LRU, 2, D_RNN), 0.02),
        'w_lru_i': nrm((N_LRU, 2, LRU_BLOCKS, LRU_BW, LRU_BW), LRU_BW ** -0.5),
        'b_lru_i': nrm((N_LRU, 2, D_RNN), 0.02),
        'lru_lambda': jnp.log(a0) - jnp.log1p(-a0),
        'w_lru_out': nrm((N_LRU, D_RNN, d), BETA * D_RNN ** -0.5),
    }


def reference(x_prompt, x_sample, cache_mla_ckv, cache_mla_krope, state_lru, c, c_ctx,
              w_ada, b_ada, ln_g, ln_b, w_ffn_in, w_ffn_out, w_pool, pool_scale,
              w_dq, g_q, w_uq, w_dkv, g_kv, w_uk, w_uv, w_mla_o,
              w_lru_in, lru_conv_w, lru_conv_b, w_lru_a, b_lru_a, w_lru_i, b_lru_i,
              lru_lambda, w_lru_out):
    x = x_prompt
    B0 = x.shape[0]
    ckv_list, krope_list, lru_list = [], [], []
    for i in range(DEPTH):
        sh1, sc1, g1, sh2, sc2, g2 = modulation(c_ctx[None], w_ada[i], b_ada[i])
        h = x * (1.0 + sc1) + sh1
        kind, j = i % N_MIXERS, i // N_MIXERS
        if kind == 0:
            out = pool_mix(h, w_pool[j], pool_scale[j])
        elif kind == 1:
            qn, qr, ckv, kr = mla_project(h, w_dq[j], g_q[j], w_uq[j], w_dkv[j], g_kv[j])
            out = mla_attend(qn, qr, ckv, kr, w_uk[j], w_uv[j], w_mla_o[j])
            ckv_list.append(ckv)
            krope_list.append(kr)
        else:
            h0 = jnp.zeros((B0, D_RNN), jnp.float32)
            out, hf, hb = lru_mixer(h, w_lru_in[j], lru_conv_w[j], lru_conv_b[j], w_lru_a[j], b_lru_a[j],
                                    w_lru_i[j], b_lru_i[j], lru_lambda[j], w_lru_out[j], h0, h0)
            lru_list.append(jnp.stack([hf, hb], axis=1))
        x = layer_norm(ALPHA * x + g1 * out, ln_g[i, 0], ln_b[i, 0])
        h = x * (1.0 + sc2) + sh2
        x = layer_norm(ALPHA * x + g2 * swiglu(h, w_ffn_in[i], w_ffn_out[i]), ln_g[i, 1], ln_b[i, 1])
    y_prompt = x
    new_mla_ckv = jnp.stack(ckv_list, axis=1)
    new_mla_krope = jnp.stack(krope_list, axis=1)
    new_lru_state = jnp.stack(lru_list, axis=1)

    x = x_sample
    T = x.shape[1]
    cos, sin = rope_tables(T)
    for i in range(DEPTH):
        sh1, sc1, g1, sh2, sc2, g2 = modulation(c, w_ada[i], b_ada[i])
        h = x * (1.0 + sc1) + sh1
        kind, j = i % N_MIXERS, i // N_MIXERS
        if kind == 0:
            out = pool_mix(h, w_pool[j], pool_scale[j])
        elif kind == 1:
            qn, qr, ckv, kr = mla_project(h, w_dq[j], g_q[j], w_uq[j], w_dkv[j], g_kv[j])
            qr = apply_rope(qr, cos[:, None], sin[:, None])
            kr = apply_rope(kr, cos, sin)
            ckv_all = jnp.concatenate([cache_mla_ckv[:, j].astype(ckv.dtype), ckv], axis=1)
            kr_all = jnp.concatenate([cache_mla_krope[:, j].astype(kr.dtype), kr], axis=1)
            out = mla_attend(qn, qr, ckv_all, kr_all, w_uk[j], w_uv[j], w_mla_o[j])
        else:
            out, _, _ = lru_mixer(h, w_lru_in[j], lru_conv_w[j], lru_conv_b[j], w_lru_a[j], b_lru_a[j],
                                  w_lru_i[j], b_lru_i[j], lru_lambda[j], w_lru_out[j],
                                  state_lru[:, j, 0], state_lru[:, j, 1])
        x = layer_norm(ALPHA * x + g1 * out, ln_g[i, 0], ln_b[i, 0])
        h = x * (1.0 + sc2) + sh2
        x = layer_norm(ALPHA * x + g2 * swiglu(h, w_ffn_in[i], w_ffn_out[i]), ln_g[i, 1], ln_b[i, 1])
    y_sample = x
    return (y_prompt, y_sample, new_mla_ckv, new_mla_krope, new_lru_state)
```

```python
import functools

import jax
import jax.numpy as jnp
from jax import lax
from jax.experimental import pallas as pl
from jax.experimental.pallas import tpu as pltpu

F32 = jnp.float32
BF16 = jnp.bfloat16

D_MODEL = 1024
DEPTH = 4
GRID_W = 64
N_MIXERS = 3
POOL_WINDOWS = (2, 4, 8, 16)
POOL_GROUPS = 4
POOL_GC = D_MODEL // POOL_GROUPS
MLA_HEADS = 8
Q_LORA = 384
KV_LORA = 256
QK_NOPE = 128
QK_ROPE = 64
V_HEAD = 128
ROPE_NF = QK_ROPE // 4
ROPE_THETA = 10000.0
MLA_SCALE = (QK_NOPE + QK_ROPE) ** -0.5
D_RNN = D_MODEL
LRU_BLOCKS = 8
LRU_BW = D_RNN // LRU_BLOCKS
CONV_W = 4
CONV_LEFT = 1
LRU_C = 8.0
D_FF = ((8 * D_MODEL // 3 + 255) // 256) * 256
ALPHA = (2.0 * DEPTH) ** 0.25
EPS = 1e-6

LANES = 128
SUBLANES = 8
VMEM_LIMIT_BYTES = 56 * 1024 * 1024
PAD_ROWS = SUBLANES
MOD_ROWS = 16
LRU_COLS = 256
TOKEN_TILE = 256


def _params(*sem):
    return pltpu.CompilerParams(dimension_semantics=sem, vmem_limit_bytes=VMEM_LIMIT_BYTES)


def _dot(a, b):
    return jnp.dot(a, b, preferred_element_type=F32)


def _layer_norm(y, g, b):
    mu = jnp.mean(y, axis=-1, keepdims=True)
    d = y - mu
    var = jnp.mean(d * d, axis=-1, keepdims=True)
    return d * lax.rsqrt(var + EPS) * g + b


def _rms_norm(y, g):
    return y * lax.rsqrt(jnp.mean(y * y, axis=-1, keepdims=True) + EPS) * g


def _modulate(x, mod_ref):
    return x * (1.0 + mod_ref[1, 0]) + mod_ref[0, 0]


def _finish(x, mixed, mod_ref, lng_ref, lnb_ref):
    return _layer_norm(ALPHA * x + mod_ref[2, 0] * mixed, lng_ref[...], lnb_ref[...])


def _mod_spec(mod, sub, nargs):
    per_batch = mod.shape[1] > 1
    if nargs == 1:
        return pl.BlockSpec((3, 1, 1, D_MODEL), lambda b: (sub, b if per_batch else 0, 0, 0))
    return pl.BlockSpec((3, 1, 1, D_MODEL), lambda b, t: (sub, b if per_batch else 0, 0, 0))


def _const_spec(shape, nargs):
    zeros = (0,) * len(shape)
    if nargs == 1:
        return pl.BlockSpec(shape, lambda b: zeros)
    return pl.BlockSpec(shape, lambda b, t: zeros)


def _mod_kernel(cond_ref, w_ref, b_ref, o_ref):
    s = jax.nn.silu(cond_ref[...]).astype(BF16)
    o_ref[0, 0] = _dot(s, w_ref[0].astype(BF16)) + b_ref[0, 0]


def _modulation(cond, w_ada, b_ada):
    d = D_MODEL
    return pl.pallas_call(
        _mod_kernel,
        grid=(DEPTH, 6),
        in_specs=[
            pl.BlockSpec((MOD_ROWS, d), lambda i, j: (0, 0)),
            pl.BlockSpec((1, d, d), lambda i, j: (i, 0, j)),
            pl.BlockSpec((1, 1, 1, d), lambda i, j: (i, j, 0, 0)),
        ],
        out_specs=pl.BlockSpec((1, 1, MOD_ROWS, d), lambda i, j: (i, j, 0, 0)),
        out_shape=jax.ShapeDtypeStruct((DEPTH, 6, MOD_ROWS, d), F32),
        compiler_params=_params("arbitrary", "arbitrary"),
        name="adaln_modulation",
    )(cond, w_ada, b_ada.reshape(DEPTH, 6, 1, d))


def _ffn_kernel(x_ref, mod_ref, lng_ref, lnb_ref, win_ref, wout_ref, o_ref):
    x = x_ref[0]
    h = _modulate(x, mod_ref).astype(BF16)
    ab = _dot(h, win_ref[...])
    u = (jax.nn.silu(ab[:, :D_FF]) * ab[:, D_FF:]).astype(BF16)
    o_ref[0] = _finish(x, _dot(u, wout_ref[...]), mod_ref, lng_ref, lnb_ref)


def _ffn_layer(x, mod, lng, lnb, win, wout):
    b, t, d = x.shape
    tm = TOKEN_TILE
    return pl.pallas_call(
        _ffn_kernel,
        grid=(b, t // tm),
        in_specs=[
            pl.BlockSpec((1, tm, d), lambda i, j: (i, j, 0)),
            _mod_spec(mod, 1, 2),
            _const_spec((1, d), 2),
            _const_spec((1, d), 2),
            _const_spec((d, 2 * D_FF), 2),
            _const_spec((D_FF, d), 2),
        ],
        out_specs=pl.BlockSpec((1, tm, d), lambda i, j: (i, j, 0)),
        out_shape=jax.ShapeDtypeStruct(x.shape, x.dtype),
        compiler_params=_params("arbitrary", "arbitrary"),
        name="ffn_layer",
    )(x, mod, lng, lnb, win, wout)


def _pool_kernel(x_ref, mod_ref, lng_ref, lnb_ref, wp_ref, ps_ref, o_ref, pad_ref):
    t_len = x_ref.shape[1]
    x = x_ref[0]
    h = _modulate(x, mod_ref)
    edge = jnp.zeros((PAD_ROWS, D_MODEL), F32)
    pad_ref[0:PAD_ROWS, :] = edge
    pad_ref[PAD_ROWS + t_len:2 * PAD_ROWS + t_len, :] = edge
    pad_ref[PAD_ROWS:PAD_ROWS + t_len, :] = h
    t = lax.broadcasted_iota(jnp.int32, (t_len, 1), 0)
    outs = []
    for gi, w in enumerate(POOL_WINDOWS):
        lo, hi = -(w // 2), w - w // 2
        c0 = gi * POOL_GC
        s = pad_ref[pl.ds(PAD_ROWS + lo, t_len), c0:c0 + POOL_GC]
        for k in range(lo + 1, hi):
            s = s + pad_ref[pl.ds(PAD_ROWS + k, t_len), c0:c0 + POOL_GC]
        cnt = (jnp.minimum(t + hi, t_len) - jnp.maximum(t + lo, 0)).astype(F32)
        pooled = s / cnt - h[:, c0:c0 + POOL_GC]
        outs.append(_dot(pooled.astype(BF16), wp_ref[gi]))
    mixed = jnp.concatenate(outs, axis=1) * ps_ref[...]
    o_ref[0] = _finish(x, mixed, mod_ref, lng_ref, lnb_ref)


def _pool_layer(x, mod, lng, lnb, wp, ps):
    b, t, d = x.shape
    return pl.pallas_call(
        _pool_kernel,
        grid=(b,),
        in_specs=[
            pl.BlockSpec((1, t, d), lambda i: (i, 0, 0)),
            _mod_spec(mod, 0, 1),
            _const_spec((1, d), 1),
            _const_spec((1, d), 1),
            _const_spec((POOL_GROUPS, POOL_GC, POOL_GC), 1),
            _const_spec((1, d), 1),
        ],
        out_specs=pl.BlockSpec((1, t, d), lambda i: (i, 0, 0)),
        out_shape=jax.ShapeDtypeStruct(x.shape, x.dtype),
        scratch_shapes=[pltpu.VMEM((t + 2 * PAD_ROWS, d), F32)],
        compiler_params=_params("arbitrary"),
        name="pool_layer",
    )(x, mod, lng, lnb, wp, ps)


def _mla_proj_kernel(*refs, rope, emit_cache):
    x_ref, mod_ref, wdq_ref, gq_ref, wq_ref, wdkv_ref, gkv_ref, wuk_ref, wuv_ref = refs[:9]
    refs = refs[9:]
    if rope:
        cq_ref, sq_ref, ck_ref, sk_ref = refs[:4]
        refs = refs[4:]
    qn_ref, qr_ref, kn_ref, v_ref, kr_ref = refs[:5]
    nope_w = MLA_HEADS * QK_NOPE
    rope_w = MLA_HEADS * QK_ROPE
    h = _modulate(x_ref[0], mod_ref).astype(BF16)
    q_lat = _rms_norm(_dot(h, wdq_ref[...]), gq_ref[...]).astype(BF16)
    q = _dot(q_lat, wq_ref[...])
    q_rope = q[:, nope_w:nope_w + rope_w]
    kv = _dot(h, wdkv_ref[...])
    k_rope = kv[:, KV_LORA:KV_LORA + 2 * QK_ROPE]
    if rope:
        q_rope = q_rope * cq_ref[...] + q[:, nope_w + rope_w:] * sq_ref[...]
        k_rope = k_rope * ck_ref[...] + kv[:, KV_LORA + 2 * QK_ROPE:] * sk_ref[...]
    c_kv = _rms_norm(kv[:, :KV_LORA], gkv_ref[...])
    c_kv16 = c_kv.astype(BF16)
    qn_ref[0] = q[:, :nope_w].astype(BF16)
    qr_ref[0] = q_rope.astype(BF16)
    kn_ref[0] = _dot(c_kv16, wuk_ref[...]).astype(BF16)
    v_ref[0] = _dot(c_kv16, wuv_ref[...]).astype(BF16)
    kr_ref[0] = k_rope.astype(BF16)
    if emit_cache:
        ckv_out_ref, kr_out_ref = refs[5:7]
        ckv_out_ref[0] = c_kv
        kr_out_ref[0] = kv[:, KV_LORA:KV_LORA + QK_ROPE]


def _mla_project(x, mod, wdq, gq, wq, wdkv, gkv, wuk, wuv, tables, emit_cache):
    b, t, d = x.shape
    tm = TOKEN_TILE
    rope = tables is not None
    nope_w = MLA_HEADS * QK_NOPE
    rope_w = MLA_HEADS * QK_ROPE
    tok = lambda w: pl.BlockSpec((1, tm, w), lambda i, j: (i, j, 0))
    in_specs = [
        tok(d),
        _mod_spec(mod, 0, 2),
        _const_spec(wdq.shape, 2),
        _const_spec(gq.shape, 2),
        _const_spec(wq.shape, 2),
        _const_spec(wdkv.shape, 2),
        _const_spec(gkv.shape, 2),
        _const_spec(wuk.shape, 2),
        _const_spec(wuv.shape, 2),
    ]
    args = [x, mod, wdq, gq, wq, wdkv, gkv, wuk, wuv]
    if rope:
        for tab in tables:
            in_specs.append(pl.BlockSpec((tm, tab.shape[1]), lambda i, j: (j, 0)))
            args.append(tab)
    out_specs = [tok(nope_w), tok(rope_w), tok(nope_w), tok(nope_w), tok(2 * QK_ROPE)]
    out_shape = [
        jax.ShapeDtypeStruct((b, t, nope_w), BF16),
        jax.ShapeDtypeStruct((b, t, rope_w), BF16),
        jax.ShapeDtypeStruct((b, t, nope_w), BF16),
        jax.ShapeDtypeStruct((b, t, nope_w), BF16),
        jax.ShapeDtypeStruct((b, t, 2 * QK_ROPE), BF16),
    ]
    if emit_cache:
        out_specs += [tok(KV_LORA), tok(QK_ROPE)]
        out_shape += [jax.ShapeDtypeStruct((b, t, KV_LORA), F32),
                      jax.ShapeDtypeStruct((b, t, QK_ROPE), F32)]
    return pl.pallas_call(
        functools.partial(_mla_proj_kernel, rope=rope, emit_cache=emit_cache),
        grid=(b, t // tm),
        in_specs=in_specs,
        out_specs=out_specs,
        out_shape=out_shape,
        compiler_params=_params("arbitrary", "arbitrary"),
        name="mla_project",
    )(*args)


def _kv_up_kernel(ckv_ref, wuk_ref, wuv_ref, kn_ref, v_ref):
    c = ckv_ref[0].astype(BF16)
    kn_ref[0] = _dot(c, wuk_ref[...]).astype(BF16)
    v_ref[0] = _dot(c, wuv_ref[...]).astype(BF16)


def _kv_up(ckv, wuk, wuv):
    b, s, r = ckv.shape
    n = wuk.shape[1]
    return pl.pallas_call(
        _kv_up_kernel,
        grid=(b,),
        in_specs=[
            pl.BlockSpec((1, s, r), lambda i: (i, 0, 0)),
            _const_spec(wuk.shape, 1),
            _const_spec(wuv.shape, 1),
        ],
        out_specs=[pl.BlockSpec((1, s, n), lambda i: (i, 0, 0))] * 2,
        out_shape=[jax.ShapeDtypeStruct((b, s, n), BF16)] * 2,
        compiler_params=_params("arbitrary"),
        name="mla_cache_kv_up",
    )(ckv, wuk, wuv)


def _attn_kernel(*refs, n_seg):
    x_ref, mod_ref, lng_ref, lnb_ref, qn_ref, qr_ref, wo_ref = refs[:7]
    seg_refs = [refs[7 + 3 * i:10 + 3 * i] for i in range(n_seg)]
    o_ref = refs[7 + 3 * n_seg]
    lane = lax.broadcasted_iota(jnp.int32, (1, 2 * QK_ROPE), 1)
    nt = (((1,), (1,)), ((), ()))
    heads = []
    for hd in range(MLA_HEADS):
        c0 = hd * QK_NOPE
        pair = (hd // 2) * 2 * QK_ROPE
        keep = (lane < QK_ROPE) if hd % 2 == 0 else (lane >= QK_ROPE)
        q_rope = jnp.where(keep, qr_ref[0, :, pair:pair + 2 * QK_ROPE], jnp.zeros((), BF16))
        q_cat = jnp.concatenate([qn_ref[0, :, c0:c0 + QK_NOPE], q_rope], axis=1)
        scores = []
        for kn_ref, kr_ref, _ in seg_refs:
            k_cat = jnp.concatenate([kn_ref[0, :, c0:c0 + QK_NOPE], kr_ref[0]], axis=1)
            scores.append(lax.dot_general(q_cat, k_cat, nt, preferred_element_type=F32) * MLA_SCALE)
        m = functools.reduce(jnp.maximum, [jnp.max(s, axis=-1, keepdims=True) for s in scores])
        es = [jnp.exp(s - m) for s in scores]
        inv = 1.0 / functools.reduce(lambda a, b: a + b, [jnp.sum(e, axis=-1, keepdims=True) for e in es])
        o = None
        for e, (_, _, v_ref) in zip(es, seg_refs):
            part = _dot((e * inv).astype(BF16), v_ref[0, :, c0:c0 + V_HEAD])
            o = part if o is None else o + part
        heads.append(o.astype(BF16))
    mixed = _dot(jnp.concatenate(heads, axis=1), wo_ref[...])
    o_ref[0] = _finish(x_ref[0], mixed, mod_ref, lng_ref, lnb_ref)


def _attn_layer(x, mod, lng, lnb, qn, qr, wo, segments):
    b, t, d = x.shape
    tq = TOKEN_TILE
    tok = lambda w: pl.BlockSpec((1, tq, w), lambda i, j: (i, j, 0))
    in_specs = [tok(d), _mod_spec(mod, 0, 2), _const_spec((1, d), 2), _const_spec((1, d), 2),
                tok(qn.shape[2]), tok(qr.shape[2]), _const_spec(wo.shape, 2)]
    args = [x, mod, lng, lnb, qn, qr, wo]
    for seg in segments:
        for a in seg:
            in_specs.append(pl.BlockSpec((1,) + a.shape[1:], lambda i, j: (i, 0, 0)))
            args.append(a)
    return pl.pallas_call(
        functools.partial(_attn_kernel, n_seg=len(segments)),
        grid=(b, t // tq),
        in_specs=in_specs,
        out_specs=tok(d),
        out_shape=jax.ShapeDtypeStruct(x.shape, x.dtype),
        compiler_params=_params("arbitrary", "arbitrary"),
        name="mla_attention",
    )(*args)


def _lru_kernel(x_ref, mod_ref, lng_ref, lnb_ref, wu_ref, wy_ref, cw_ref, cb_ref, wg_ref, bg_ref,
                lam_ref, h0_ref, wout_ref, o_ref, st_ref,
                h_scr, acc_scr, pad_scr, af_scr, bf_scr, ab_scr, bb_scr, hf_scr, hb_scr):
    t_len = x_ref.shape[1]
    j = pl.program_id(1)

    @pl.when(j == 0)
    def _():
        h_scr[...] = _modulate(x_ref[0], mod_ref).astype(BF16)
        acc_scr[...] = jnp.zeros_like(acc_scr)

    h = h_scr[...]
    u = _dot(h, wu_ref[...])
    y = jax.nn.gelu(_dot(h, wy_ref[...]))

    edge = jnp.zeros((PAD_ROWS, LRU_COLS), F32)
    pad_scr[0:PAD_ROWS, :] = edge
    pad_scr[PAD_ROWS + t_len:2 * PAD_ROWS + t_len, :] = edge
    pad_scr[PAD_ROWS:PAD_ROWS + t_len, :] = u
    uc = None
    for k in range(CONV_W):
        term = pad_scr[pl.ds(PAD_ROWS + k - CONV_LEFT, t_len), :] * cw_ref[k:k + 1, :]
        uc = term if uc is None else uc + term
    uc = uc + cb_ref[...]

    a_scr = (af_scr, ab_scr)
    b_scr = (bf_scr, bb_scr)
    for bl in range(LRU_COLS // LRU_BW):
        c0 = bl * LRU_BW
        ub = uc[:, c0:c0 + LRU_BW]
        gates = jax.nn.sigmoid(_dot(ub.astype(BF16), wg_ref[bl]) + bg_ref[bl])
        for dr in range(2):
            r = gates[:, dr * LRU_BW:(dr + 1) * LRU_BW]
            ig = gates[:, (2 + dr) * LRU_BW:(3 + dr) * LRU_BW]
            log_a = -LRU_C * r * jax.nn.softplus(-lam_ref[dr:dr + 1, c0:c0 + LRU_BW])
            a_scr[dr][:, c0:c0 + LRU_BW] = jnp.exp(log_a)
            th = jnp.tanh(log_a)
            b_scr[dr][:, c0:c0 + LRU_BW] = jnp.sqrt(-2.0 * th / (1.0 - th)) * (ig * ub)

    def step(i, carry):
        hf, hb = carry
        tb = t_len - 1 - i
        hf = af_scr[pl.ds(i, 1), :] * hf + bf_scr[pl.ds(i, 1), :]
        hb = ab_scr[pl.ds(tb, 1), :] * hb + bb_scr[pl.ds(tb, 1), :]
        hf_scr[pl.ds(i, 1), :] = hf
        hb_scr[pl.ds(tb, 1), :] = hb
        return hf, hb

    hf, hb = lax.fori_loop(0, t_len, step, (h0_ref[0, 0:1, :], h0_ref[0, 1:2, :]), unroll=8)
    st_ref[0, 0:1, :] = hf
    st_ref[0, 1:2, :] = hb

    mixed = ((hf_scr[...] + hb_scr[...]) * y).astype(BF16)
    acc_scr[...] += _dot(mixed, wout_ref[...])

    @pl.when(j == pl.num_programs(1) - 1)
    def _():
        o_ref[0] = _finish(x_ref[0], acc_scr[...], mod_ref, lng_ref, lnb_ref)


def _lru_layer(x, mod, lng, lnb, w_in, conv_w, conv_b, wg, bg, lam, h0, w_out):
    b, t, d = x.shape
    cw = LRU_COLS
    ncb = D_RNN // cw
    bpc = cw // LRU_BW
    seq = lambda w: pl.BlockSpec((1, t, w), lambda i, j: (i, 0, 0))
    col = lambda rows: pl.BlockSpec((rows, cw), lambda i, j: (0, j))
    out, state = pl.pallas_call(
        _lru_kernel,
        grid=(b, ncb),
        in_specs=[
            seq(d),
            _mod_spec(mod, 0, 2),
            _const_spec((1, d), 2),
            _const_spec((1, d), 2),
            pl.BlockSpec((d, cw), lambda i, j: (0, j)),
            pl.BlockSpec((d, cw), lambda i, j: (0, ncb + j)),
            col(CONV_W),
            col(1),
            pl.BlockSpec((bpc, LRU_BW, 4 * LRU_BW), lambda i, j: (j, 0, 0)),
            pl.BlockSpec((bpc, 1, 4 * LRU_BW), lambda i, j: (j, 0, 0)),
            col(2),
            pl.BlockSpec((1, 2, cw), lambda i, j: (i, 0, j)),
            pl.BlockSpec((cw, d), lambda i, j: (j, 0)),
        ],
        out_specs=[seq(d), pl.BlockSpec((1, 2, cw), lambda i, j: (i, 0, j))],
        out_shape=[jax.ShapeDtypeStruct(x.shape, x.dtype),
                   jax.ShapeDtypeStruct((b, 2, D_RNN), x.dtype)],
        scratch_shapes=[pltpu.VMEM((t, d), BF16), pltpu.VMEM((t, d), F32),
                        pltpu.VMEM((t + 2 * PAD_ROWS, cw), F32)]
                       + [pltpu.VMEM((t, cw), F32)] * 6,
        compiler_params=_params("arbitrary", "arbitrary"),
        name="lru_layer",
    )(x, mod, lng, lnb, w_in, w_in, conv_w, conv_b, wg, bg, lam, h0, w_out)
    return out, state


def _rope_rotate_cols(w):
    nf = ROPE_NF
    return jnp.concatenate([-w[..., nf:2 * nf], w[..., :nf], -w[..., 3 * nf:], w[..., 2 * nf:3 * nf]], axis=-1)


def _rope_tables(t_len):
    t = jnp.arange(t_len)
    rows = (t // GRID_W).astype(F32)
    cols = (t % GRID_W).astype(F32)
    inv = ROPE_THETA ** (-jnp.arange(ROPE_NF, dtype=F32) / ROPE_NF)
    ar, ac = rows[:, None] * inv, cols[:, None] * inv
    cos = jnp.concatenate([jnp.cos(ar), jnp.cos(ar), jnp.cos(ac), jnp.cos(ac)], axis=1)
    sin = jnp.concatenate([jnp.sin(ar), jnp.sin(ar), jnp.sin(ac), jnp.sin(ac)], axis=1)
    return (jnp.tile(cos, (1, MLA_HEADS)), jnp.tile(sin, (1, MLA_HEADS)),
            jnp.tile(cos, (1, 2)), jnp.tile(sin, (1, 2)))


def _mla_weights(w_dq, w_uq, w_dkv, w_uk, w_uv, w_o, rope):
    r = w_uq.shape[0]
    wq = [w_uq[:, :, :QK_NOPE].reshape(r, -1), w_uq[:, :, QK_NOPE:].reshape(r, -1)]
    wk = w_dkv[:, KV_LORA:]
    wdkv = [w_dkv[:, :KV_LORA], wk, wk]
    if rope:
        wq.append(_rope_rotate_cols(w_uq[:, :, QK_NOPE:]).reshape(r, -1))
        wk_rot = _rope_rotate_cols(wk)
        wdkv += [wk_rot, wk_rot]
    cat = lambda parts: jnp.concatenate(parts, axis=1).astype(BF16)
    return (w_dq.astype(BF16), cat(wq), cat(wdkv),
            w_uk.reshape(KV_LORA, -1).astype(BF16), w_uv.reshape(KV_LORA, -1).astype(BF16),
            w_o.astype(BF16))


def _lru_gate_weights(w_a, b_a, w_i, b_i):
    wg = jnp.concatenate([w_a[0], w_a[1], w_i[0], w_i[1]], axis=-1).astype(BF16)
    blk = lambda v: v.reshape(LRU_BLOCKS, 1, LRU_BW)
    bg = jnp.concatenate([blk(b_a[0]), blk(b_a[1]), blk(b_i[0]), blk(b_i[1])], axis=-1)
    return wg, bg


def kernel(x_prompt, x_sample, cache_mla_ckv, cache_mla_krope, state_lru, c, c_ctx, w_ada, b_ada, ln_g, ln_b, w_ffn_in, w_ffn_out, w_pool, pool_scale, w_dq, g_q, w_uq, w_dkv, g_kv, w_uk, w_uv, w_mla_o, w_lru_in, lru_conv_w, lru_conv_b, w_lru_a, b_lru_a, w_lru_i, b_lru_i, lru_lambda, w_lru_out):
    d = D_MODEL
    n_lat = c.shape[0]
    cond = jnp.concatenate([c_ctx[None], c, jnp.zeros((MOD_ROWS - 1 - n_lat, d), F32)], axis=0)
    mod = _modulation(cond, w_ada, b_ada)
    mods = (mod[:, :, 0:1, :][:, :, :, None, :],
            mod[:, :, 1:1 + n_lat, :][:, :, :, None, :])

    xs = [x_prompt, x_sample]
    ckv_out, krope_out, lru_out = [], [], []
    t_lat = x_sample.shape[1]
    tables = _rope_tables(t_lat)
    for i in range(DEPTH):
        kind, j = i % N_MIXERS, i // N_MIXERS
        row = lambda v: v.reshape(1, -1)
        lng0, lnb0, lng1, lnb1 = row(ln_g[i, 0]), row(ln_b[i, 0]), row(ln_g[i, 1]), row(ln_b[i, 1])
        win, wout = w_ffn_in[i].astype(BF16), w_ffn_out[i].astype(BF16)
        if kind == 0:
            wp, ps = w_pool[j].astype(BF16), row(pool_scale[j])
        elif kind == 1:
            mla_w = [_mla_weights(w_dq[j], w_uq[j], w_dkv[j], w_uk[j], w_uv[j], w_mla_o[j], rope)
                     for rope in (False, True)]
            gq, gkv = row(g_q[j]), row(g_kv[j])
        else:
            w_in = w_lru_in[j].astype(BF16)
            wg, bg = _lru_gate_weights(w_lru_a[j], b_lru_a[j], w_lru_i[j], b_lru_i[j])
            w_out = w_lru_out[j].astype(BF16)
        for p in range(2):
            x, m = xs[p], mods[p][i]
            if kind == 0:
                x = _pool_layer(x, m, lng0, lnb0, wp, ps)
            elif kind == 1:
                wdq, wq, wdkv, wuk, wuv, wo = mla_w[p]
                if p == 0:
                    qn, qr, kn, v, kr, ckv, kr64 = _mla_project(
                        x, m, wdq, gq, wq, wdkv, gkv, wuk, wuv, None, True)
                    ckv_out.append(ckv)
                    krope_out.append(kr64)
                    segments = [(kn, kr, v)]
                else:
                    qn, qr, kn, v, kr = _mla_project(
                        x, m, wdq, gq, wq, wdkv, gkv, wuk, wuv, tables, False)
                    kn_c, v_c = _kv_up(cache_mla_ckv[:, j], wuk, wuv)
                    kr_c = cache_mla_krope[:, j]
                    kr_c = jnp.concatenate([kr_c, kr_c], axis=-1).astype(BF16)
                    segments = [(kn_c, kr_c, v_c), (kn, kr, v)]
                x = _attn_layer(x, m, lng0, lnb0, qn, qr, wo, segments)
            else:
                if p == 0:
                    h0 = jnp.zeros((x.shape[0], 2, D_RNN), F32)
                else:
                    h0 = state_lru[:, j]
                x, st = _lru_layer(x, m, lng0, lnb0, w_in, lru_conv_w[j], row(lru_conv_b[j]),
                                   wg, bg, lru_lambda[j], h0, w_out)
                if p == 0:
                    lru_out.append(st)
            xs[p] = _ffn_layer(x, m, lng1, lnb1, win, wout)
    return (xs[0], xs[1], jnp.stack(ckv_out, axis=1), jnp.stack(krope_out, axis=1),
            jnp.stack(lru_out, axis=1))
```

```python
import functools

import jax
import jax.numpy as jnp
from jax import lax
from jax.experimental import pallas as pl
from jax.experimental.pallas import tpu as pltpu

F32 = jnp.float32
BF16 = jnp.bfloat16

D_MODEL = 1024
DEPTH = 4
GRID_W = 64
N_MIXERS = 3
POOL_WINDOWS = (2, 4, 8, 16)
POOL_GROUPS = 4
POOL_GC = D_MODEL // POOL_GROUPS
MLA_HEADS = 8
Q_LORA = 384
KV_LORA = 256
QK_NOPE = 128
QK_ROPE = 64
V_HEAD = 128
ROPE_NF = QK_ROPE // 4
ROPE_THETA = 10000.0
MLA_SCALE = (QK_NOPE + QK_ROPE) ** -0.5
D_RNN = D_MODEL
LRU_BLOCKS = 8
LRU_BW = D_RNN // LRU_BLOCKS
CONV_W = 4
CONV_LEFT = 1
LRU_C = 8.0
D_FF = ((8 * D_MODEL // 3 + 255) // 256) * 256
ALPHA = (2.0 * DEPTH) ** 0.25
EPS = 1e-6

LANES = 128
SUBLANES = 8
VMEM_LIMIT_BYTES = 56 * 1024 * 1024
PAD_ROWS = SUBLANES
MOD_ROWS = 16
LRU_COLS = 256
TOKEN_TILE = 256
FFN_TILE = 512
FFN_CHUNK = D_FF // 2


def _params(*sem):
    return pltpu.CompilerParams(dimension_semantics=sem, vmem_limit_bytes=VMEM_LIMIT_BYTES)


def _dot(a, b):
    return jnp.dot(a, b, preferred_element_type=F32)


def _layer_norm(y, g, b):
    mu = jnp.mean(y, axis=-1, keepdims=True)
    d = y - mu
    var = jnp.mean(d * d, axis=-1, keepdims=True)
    return d * lax.rsqrt(var + EPS) * g + b


def _rms_norm(y, g):
    return y * lax.rsqrt(jnp.mean(y * y, axis=-1, keepdims=True) + EPS) * g


def _modulate(x, mod_ref):
    return x * (1.0 + mod_ref[1, 0]) + mod_ref[0, 0]


def _finish(x, mixed, mod_ref, lng_ref, lnb_ref):
    return _layer_norm(ALPHA * x + mod_ref[2, 0] * mixed, lng_ref[...], lnb_ref[...])


def _mod_spec(mod, sub, nargs):
    per_batch = mod.shape[1] > 1
    if nargs == 1:
        return pl.BlockSpec((3, 1, 1, D_MODEL), lambda b: (sub, b if per_batch else 0, 0, 0))
    return pl.BlockSpec((3, 1, 1, D_MODEL), lambda b, t: (sub, b if per_batch else 0, 0, 0))


def _const_spec(shape, nargs):
    zeros = (0,) * len(shape)
    if nargs == 1:
        return pl.BlockSpec(shape, lambda b: zeros)
    return pl.BlockSpec(shape, lambda b, t: zeros)


def _mod_kernel(cond_ref, w_ref, b_ref, o_ref):
    s = jax.nn.silu(cond_ref[...]).astype(BF16)
    o_ref[0, 0] = _dot(s, w_ref[0].astype(BF16)) + b_ref[0, 0]


def _modulation(cond, w_ada, b_ada):
    d = D_MODEL
    return pl.pallas_call(
        _mod_kernel,
        grid=(DEPTH, 6),
        in_specs=[
            pl.BlockSpec((MOD_ROWS, d), lambda i, j: (0, 0)),
            pl.BlockSpec((1, d, d), lambda i, j: (i, 0, j)),
            pl.BlockSpec((1, 1, 1, d), lambda i, j: (i, j, 0, 0)),
        ],
        out_specs=pl.BlockSpec((1, 1, MOD_ROWS, d), lambda i, j: (i, j, 0, 0)),
        out_shape=jax.ShapeDtypeStruct((DEPTH, 6, MOD_ROWS, d), F32),
        compiler_params=_params("arbitrary", "arbitrary"),
        name="adaln_modulation",
    )(cond, w_ada, b_ada.reshape(DEPTH, 6, 1, d))


def _tile_rows(ref, interleaved):
    d = D_MODEL
    if interleaved:
        parts = [ref[q, :, s * d:(s + 1) * d] for q in range(ref.shape[0]) for s in range(ref.shape[2] // d)]
    else:
        parts = [ref[q] for q in range(ref.shape[0])]
    return parts[0] if len(parts) == 1 else jnp.concatenate(parts, axis=0)


def _store_tile_rows(ref, val, interleaved):
    d = D_MODEL
    rows = ref.shape[1]
    nseg = ref.shape[2] // d if interleaved else 1
    for q in range(ref.shape[0]):
        for s in range(nseg):
            r0 = (q * nseg + s) * rows
            if interleaved:
                ref[q, :, s * d:(s + 1) * d] = val[r0:r0 + rows]
            else:
                ref[q] = val[r0:r0 + rows]


def _tile_spec(t_len, tm, interleaved):
    d = D_MODEL
    seg = t_len // SUBLANES
    if tm >= t_len:
        nq = tm // t_len
        shape = (nq, seg, SUBLANES * d) if interleaved else (nq, t_len, d)
        return pl.BlockSpec(shape, lambda i: (i, 0, 0)), (lambda i: i * nq)
    per_seq = t_len // tm
    if interleaved:
        spec = pl.BlockSpec((1, seg, (tm // seg) * d), lambda i: (i // per_seq, 0, i % per_seq))
    else:
        spec = pl.BlockSpec((1, tm, d), lambda i: (i // per_seq, i % per_seq, 0))
    return spec, (lambda i: i // per_seq)


def _ffn_kernel(x_ref, mod_ref, lng_ref, lnb_ref, win_ref, wout_ref, o_ref, *, in_il, out_il):
    x = _tile_rows(x_ref, in_il)
    h = _modulate(x, mod_ref).astype(BF16)
    acc = None
    for c0 in range(0, D_FF, FFN_CHUNK):
        a = _dot(h, win_ref[:, c0:c0 + FFN_CHUNK])
        b = _dot(h, win_ref[:, D_FF + c0:D_FF + c0 + FFN_CHUNK])
        u = (jax.nn.silu(a) * b).astype(BF16)
        part = _dot(u, wout_ref[c0:c0 + FFN_CHUNK, :])
        acc = part if acc is None else acc + part
    _store_tile_rows(o_ref, _finish(x, acc, mod_ref, lng_ref, lnb_ref), out_il)


def _ffn_layer(x, mod, lng, lnb, win, wout, t_len, in_il, out_il):
    d = D_MODEL
    nseq = x.shape[0]
    tm = FFN_TILE
    per_batch = mod.shape[1] > 1
    assert not per_batch or tm <= t_len
    x_spec, seq_of = _tile_spec(t_len, tm, in_il)
    o_spec, _ = _tile_spec(t_len, tm, out_il)
    out_dims = (nseq, t_len // SUBLANES, SUBLANES * d) if out_il else (nseq, t_len, d)
    resident = lambda shape: pl.BlockSpec(shape, lambda i: (0, 0), pipeline_mode=pl.Buffered(1))
    return pl.pallas_call(
        functools.partial(_ffn_kernel, in_il=in_il, out_il=out_il),
        grid=(nseq * t_len // tm,),
        in_specs=[
            x_spec,
            pl.BlockSpec((3, 1, 1, d), lambda i: (1, seq_of(i) if per_batch else 0, 0, 0)),
            _const_spec((1, d), 1),
            _const_spec((1, d), 1),
            resident((d, 2 * D_FF)),
            resident((D_FF, d)),
        ],
        out_specs=o_spec,
        out_shape=jax.ShapeDtypeStruct(out_dims, x.dtype),
        compiler_params=_params("arbitrary"),
        name="ffn_layer",
    )(x, mod, lng, lnb, win, wout)


def _pool_kernel(x_ref, mod_ref, lng_ref, lnb_ref, wp_ref, ps_ref, o_ref, pad_ref):
    t_len = x_ref.shape[1]
    x = x_ref[0]
    h = _modulate(x, mod_ref)
    edge = jnp.zeros((PAD_ROWS, D_MODEL), F32)
    pad_ref[0:PAD_ROWS, :] = edge
    pad_ref[PAD_ROWS + t_len:2 * PAD_ROWS + t_len, :] = edge
    pad_ref[PAD_ROWS:PAD_ROWS + t_len, :] = h
    t = lax.broadcasted_iota(jnp.int32, (t_len, 1), 0)
    outs = []
    for gi, w in enumerate(POOL_WINDOWS):
        lo, hi = -(w // 2), w - w // 2
        c0 = gi * POOL_GC
        s = pad_ref[pl.ds(PAD_ROWS + lo, t_len), c0:c0 + POOL_GC]
        for k in range(lo + 1, hi):
            s = s + pad_ref[pl.ds(PAD_ROWS + k, t_len), c0:c0 + POOL_GC]
        cnt = (jnp.minimum(t + hi, t_len) - jnp.maximum(t + lo, 0)).astype(F32)
        pooled = s / cnt - h[:, c0:c0 + POOL_GC]
        outs.append(_dot(pooled.astype(BF16), wp_ref[gi]))
    mixed = jnp.concatenate(outs, axis=1) * ps_ref[...]
    o_ref[0] = _finish(x, mixed, mod_ref, lng_ref, lnb_ref)


def _pool_layer(x, mod, lng, lnb, wp, ps):
    b, t, d = x.shape
    return pl.pallas_call(
        _pool_kernel,
        grid=(b,),
        in_specs=[
            pl.BlockSpec((1, t, d), lambda i: (i, 0, 0)),
            _mod_spec(mod, 0, 1),
            _const_spec((1, d), 1),
            _const_spec((1, d), 1),
            _const_spec((POOL_GROUPS, POOL_GC, POOL_GC), 1),
            _const_spec((1, d), 1),
        ],
        out_specs=pl.BlockSpec((1, t, d), lambda i: (i, 0, 0)),
        out_shape=jax.ShapeDtypeStruct(x.shape, x.dtype),
        scratch_shapes=[pltpu.VMEM((t + 2 * PAD_ROWS, d), F32)],
        compiler_params=_params("arbitrary"),
        name="pool_layer",
    )(x, mod, lng, lnb, wp, ps)


def _mla_proj_kernel(*refs, rope, emit_cache):
    x_ref, mod_ref, wdq_ref, gq_ref, wq_ref, wdkv_ref, gkv_ref, wuk_ref, wuv_ref = refs[:9]
    refs = refs[9:]
    if rope:
        cq_ref, sq_ref, ck_ref, sk_ref = refs[:4]
        refs = refs[4:]
    qn_ref, qr_ref, kn_ref, v_ref, kr_ref = refs[:5]
    nope_w = MLA_HEADS * QK_NOPE
    rope_w = MLA_HEADS * QK_ROPE
    h = _modulate(x_ref[0], mod_ref).astype(BF16)
    q_lat = _rms_norm(_dot(h, wdq_ref[...]), gq_ref[...]).astype(BF16)
    q = _dot(q_lat, wq_ref[...])
    q_rope = q[:, nope_w:nope_w + rope_w]
    kv = _dot(h, wdkv_ref[...])
    k_rope = kv[:, KV_LORA:KV_LORA + 2 * QK_ROPE]
    if rope:
        q_rope = q_rope * cq_ref[...] + q[:, nope_w + rope_w:] * sq_ref[...]
        k_rope = k_rope * ck_ref[...] + kv[:, KV_LORA + 2 * QK_ROPE:] * sk_ref[...]
    c_kv = _rms_norm(kv[:, :KV_LORA], gkv_ref[...])
    c_kv16 = c_kv.astype(BF16)
    qn_ref[0] = q[:, :nope_w].astype(BF16)
    qr_ref[0] = q_rope.astype(BF16)
    kn_ref[0] = _dot(c_kv16, wuk_ref[...]).astype(BF16)
    v_ref[0] = _dot(c_kv16, wuv_ref[...]).astype(BF16)
    kr_ref[0] = k_rope.astype(BF16)
    if emit_cache:
        ckv_out_ref, kr_out_ref = refs[5:7]
        ckv_out_ref[0] = c_kv
        kr_out_ref[0] = kv[:, KV_LORA:KV_LORA + QK_ROPE]


def _mla_project(x, mod, wdq, gq, wq, wdkv, gkv, wuk, wuv, tables, emit_cache):
    b, t, d = x.shape
    tm = TOKEN_TILE
    rope = tables is not None
    nope_w = MLA_HEADS * QK_NOPE
    rope_w = MLA_HEADS * QK_ROPE
    tok = lambda w: pl.BlockSpec((1, tm, w), lambda i, j: (i, j, 0))
    in_specs = [
        tok(d),
        _mod_spec(mod, 0, 2),
        _const_spec(wdq.shape, 2),
        _const_spec(gq.shape, 2),
        _const_spec(wq.shape, 2),
        _const_spec(wdkv.shape, 2),
        _const_spec(gkv.shape, 2),
        _const_spec(wuk.shape, 2),
        _const_spec(wuv.shape, 2),
    ]
    args = [x, mod, wdq, gq, wq, wdkv, gkv, wuk, wuv]
    if rope:
        for tab in tables:
            in_specs.append(pl.BlockSpec((tm, tab.shape[1]), lambda i, j: (j, 0)))
            args.append(tab)
    out_specs = [tok(nope_w), tok(rope_w), tok(nope_w), tok(nope_w), tok(2 * QK_ROPE)]
    out_shape = [
        jax.ShapeDtypeStruct((b, t, nope_w), BF16),
        jax.ShapeDtypeStruct((b, t, rope_w), BF16),
        jax.ShapeDtypeStruct((b, t, nope_w), BF16),
        jax.ShapeDtypeStruct((b, t, nope_w), BF16),
        jax.ShapeDtypeStruct((b, t, 2 * QK_ROPE), BF16),
    ]
    if emit_cache:
        out_specs += [tok(KV_LORA), tok(QK_ROPE)]
        out_shape += [jax.ShapeDtypeStruct((b, t, KV_LORA), F32),
                      jax.ShapeDtypeStruct((b, t, QK_ROPE), F32)]
    return pl.pallas_call(
        functools.partial(_mla_proj_kernel, rope=rope, emit_cache=emit_cache),
        grid=(b, t // tm),
        in_specs=in_specs,
        out_specs=out_specs,
        out_shape=out_shape,
        compiler_params=_params("arbitrary", "arbitrary"),
        name="mla_project",
    )(*args)


def _kv_up_kernel(ckv_ref, wuk_ref, wuv_ref, kn_ref, v_ref):
    c = ckv_ref[0].astype(BF16)
    kn_ref[0] = _dot(c, wuk_ref[...]).astype(BF16)
    v_ref[0] = _dot(c, wuv_ref[...]).astype(BF16)


def _kv_up(ckv, wuk, wuv):
    b, s, r = ckv.shape
    n = wuk.shape[1]
    return pl.pallas_call(
        _kv_up_kernel,
        grid=(b,),
        in_specs=[
            pl.BlockSpec((1, s, r), lambda i: (i, 0, 0)),
            _const_spec(wuk.shape, 1),
            _const_spec(wuv.shape, 1),
        ],
        out_specs=[pl.BlockSpec((1, s, n), lambda i: (i, 0, 0))] * 2,
        out_shape=[jax.ShapeDtypeStruct((b, s, n), BF16)] * 2,
        compiler_params=_params("arbitrary"),
        name="mla_cache_kv_up",
    )(ckv, wuk, wuv)


def _attn_kernel(*refs, n_seg):
    x_ref, mod_ref, lng_ref, lnb_ref, qn_ref, qr_ref, wo_ref = refs[:7]
    seg_refs = [refs[7 + 3 * i:10 + 3 * i] for i in range(n_seg)]
    o_ref = refs[7 + 3 * n_seg]
    lane = lax.broadcasted_iota(jnp.int32, (1, 2 * QK_ROPE), 1)
    nt = (((1,), (1,)), ((), ()))
    heads = []
    for hd in range(MLA_HEADS):
        c0 = hd * QK_NOPE
        pair = (hd // 2) * 2 * QK_ROPE
        keep = (lane < QK_ROPE) if hd % 2 == 0 else (lane >= QK_ROPE)
        q_rope = jnp.where(keep, qr_ref[0, :, pair:pair + 2 * QK_ROPE], jnp.zeros((), BF16))
        q_cat = jnp.concatenate([qn_ref[0, :, c0:c0 + QK_NOPE], q_rope], axis=1)
        scores = []
        for kn_ref, kr_ref, _ in seg_refs:
            k_cat = jnp.concatenate([kn_ref[0, :, c0:c0 + QK_NOPE], kr_ref[0]], axis=1)
            scores.append(lax.dot_general(q_cat, k_cat, nt, preferred_element_type=F32) * MLA_SCALE)
        m = functools.reduce(jnp.maximum, [jnp.max(s, axis=-1, keepdims=True) for s in scores])
        es = [jnp.exp(s - m) for s in scores]
        inv = 1.0 / functools.reduce(lambda a, b: a + b, [jnp.sum(e, axis=-1, keepdims=True) for e in es])
        o = None
        for e, (_, _, v_ref) in zip(es, seg_refs):
            part = _dot((e * inv).astype(BF16), v_ref[0, :, c0:c0 + V_HEAD])
            o = part if o is None else o + part
        heads.append(o.astype(BF16))
    mixed = _dot(jnp.concatenate(heads, axis=1), wo_ref[...])
    o_ref[0] = _finish(x_ref[0], mixed, mod_ref, lng_ref, lnb_ref)


def _attn_layer(x, mod, lng, lnb, qn, qr, wo, segments):
    b, t, d = x.shape
    tq = TOKEN_TILE
    tok = lambda w: pl.BlockSpec((1, tq, w), lambda i, j: (i, j, 0))
    in_specs = [tok(d), _mod_spec(mod, 0, 2), _const_spec((1, d), 2), _const_spec((1, d), 2),
                tok(qn.shape[2]), tok(qr.shape[2]), _const_spec(wo.shape, 2)]
    args = [x, mod, lng, lnb, qn, qr, wo]
    for seg in segments:
        for a in seg:
            in_specs.append(pl.BlockSpec((1,) + a.shape[1:], lambda i, j: (i, 0, 0)))
            args.append(a)
    return pl.pallas_call(
        functools.partial(_attn_kernel, n_seg=len(segments)),
        grid=(b, t // tq),
        in_specs=in_specs,
        out_specs=tok(d),
        out_shape=jax.ShapeDtypeStruct(x.shape, x.dtype),
        compiler_params=_params("arbitrary", "arbitrary"),
        name="mla_attention",
    )(*args)


def _lru_kernel(x_ref, mod_ref, lng_ref, lnb_ref, wu_ref, wy_ref, cw_ref, cb_ref, wg_ref, bg_ref,
                lam_ref, h0_ref, wout_ref, o_ref, st_ref,
                h_scr, acc_scr, pad_scr, af_scr, bf_scr, ab_scr, bb_scr):
    t_len = x_ref.shape[1]
    n_blk = t_len // SUBLANES
    j = pl.program_id(1)

    @pl.when(j == 0)
    def _():
        h_scr[...] = _modulate(x_ref[0], mod_ref).astype(BF16)
        acc_scr[...] = jnp.zeros_like(acc_scr)

    h = h_scr[...]
    u = _dot(h, wu_ref[...])
    y = jax.nn.gelu(_dot(h, wy_ref[...]))

    sub = lax.broadcasted_iota(jnp.int32, (SUBLANES, LRU_COLS), 0)
    lead = CONV_LEFT * SUBLANES
    for i in range(1, CONV_LEFT + 1):
        blk = u[(n_blk - i) * SUBLANES:(n_blk - i + 1) * SUBLANES]
        pad_scr[lead - i * SUBLANES:lead - (i - 1) * SUBLANES, :] = jnp.where(
            sub >= 1, pltpu.roll(blk, 1, axis=0), 0.0)
    pad_scr[lead:lead + t_len, :] = u
    for i in range(CONV_W - 1 - CONV_LEFT):
        blk = u[i * SUBLANES:(i + 1) * SUBLANES]
        pad_scr[lead + t_len + i * SUBLANES:lead + t_len + (i + 1) * SUBLANES, :] = jnp.where(
            sub < SUBLANES - 1, pltpu.roll(blk, SUBLANES - 1, axis=0), 0.0)
    uc = None
    for k in range(CONV_W):
        term = pad_scr[k * SUBLANES:k * SUBLANES + t_len, :] * cw_ref[k:k + 1, :]
        uc = term if uc is None else uc + term
    uc = uc + cb_ref[...]

    a_scr = (af_scr, ab_scr)
    b_scr = (bf_scr, bb_scr)
    for bl in range(LRU_COLS // LRU_BW):
        c0 = bl * LRU_BW
        ub = uc[:, c0:c0 + LRU_BW]
        gates = jax.nn.sigmoid(_dot(ub.astype(BF16), wg_ref[bl]) + bg_ref[bl])
        for dr in range(2):
            r = gates[:, dr * LRU_BW:(dr + 1) * LRU_BW]
            ig = gates[:, (2 + dr) * LRU_BW:(3 + dr) * LRU_BW]
            log_a = -LRU_C * r * jax.nn.softplus(-lam_ref[dr:dr + 1, c0:c0 + LRU_BW])
            a_scr[dr][:, c0:c0 + LRU_BW] = jnp.exp(log_a)
            th = jnp.tanh(log_a)
            b_scr[dr][:, c0:c0 + LRU_BW] = jnp.sqrt(-2.0 * th / (1.0 - th)) * (ig * ub)

    def step(k, carry):
        a_f, b_f, a_b, b_b = carry
        rf = pl.multiple_of(k * SUBLANES, SUBLANES)
        rb = pl.multiple_of((n_blk - 1 - k) * SUBLANES, SUBLANES)
        a = af_scr[pl.ds(rf, SUBLANES), :]
        b_f = a * b_f + bf_scr[pl.ds(rf, SUBLANES), :]
        a_f = a * a_f
        af_scr[pl.ds(rf, SUBLANES), :] = a_f
        bf_scr[pl.ds(rf, SUBLANES), :] = b_f
        a = ab_scr[pl.ds(rb, SUBLANES), :]
        b_b = a * b_b + bb_scr[pl.ds(rb, SUBLANES), :]
        a_b = a * a_b
        ab_scr[pl.ds(rb, SUBLANES), :] = a_b
        bb_scr[pl.ds(rb, SUBLANES), :] = b_b
        return a_f, b_f, a_b, b_b

    one = jnp.ones((SUBLANES, LRU_COLS), F32)
    zero = jnp.zeros((SUBLANES, LRU_COLS), F32)
    a_f, b_f, a_b, b_b = lax.fori_loop(0, n_blk, step, (one, zero, one, zero), unroll=4)

    hf = h0_ref[0, 0:1, :]
    hf_in = zero
    for s in range(SUBLANES):
        hf_in = jnp.where(sub == s, hf, hf_in)
        hf = a_f[s:s + 1] * hf + b_f[s:s + 1]
    hb = h0_ref[0, 1:2, :]
    hb_in = zero
    for s in reversed(range(SUBLANES)):
        hb_in = jnp.where(sub == s, hb, hb_in)
        hb = a_b[s:s + 1] * hb + b_b[s:s + 1]
    st_ref[0, 0:1, :] = hf
    st_ref[0, 1:2, :] = hb

    blocks = lambda ref: ref[...].reshape(n_blk, SUBLANES, LRU_COLS)
    h_sum = (blocks(af_scr) * hf_in + blocks(bf_scr)) + (blocks(ab_scr) * hb_in + blocks(bb_scr))
    mixed = (h_sum.reshape(t_len, LRU_COLS) * y).astype(BF16)
    acc_scr[...] += _dot(mixed, wout_ref[...])

    @pl.when(j == pl.num_programs(1) - 1)
    def _():
        o_ref[0] = _finish(x_ref[0], acc_scr[...], mod_ref, lng_ref, lnb_ref)


def _lru_layer(x, mod, lng, lnb, w_in, conv_w, conv_b, wg, bg, lam, h0, w_out):
    b, t, d = x.shape
    cw = LRU_COLS
    ncb = D_RNN // cw
    bpc = cw // LRU_BW
    seq = lambda w: pl.BlockSpec((1, t, w), lambda i, j: (i, 0, 0))
    col = lambda rows: pl.BlockSpec((rows, cw), lambda i, j: (0, j))
    out, state = pl.pallas_call(
        _lru_kernel,
        grid=(b, ncb),
        in_specs=[
            seq(d),
            _mod_spec(mod, 0, 2),
            _const_spec((1, d), 2),
            _const_spec((1, d), 2),
            pl.BlockSpec((d, cw), lambda i, j: (0, j)),
            pl.BlockSpec((d, cw), lambda i, j: (0, ncb + j)),
            col(CONV_W),
            col(1),
            pl.BlockSpec((bpc, LRU_BW, 4 * LRU_BW), lambda i, j: (j, 0, 0)),
            pl.BlockSpec((bpc, 1, 4 * LRU_BW), lambda i, j: (j, 0, 0)),
            col(2),
            pl.BlockSpec((1, 2, cw), lambda i, j: (i, 0, j)),
            pl.BlockSpec((cw, d), lambda i, j: (j, 0)),
        ],
        out_specs=[seq(d), pl.BlockSpec((1, 2, cw), lambda i, j: (i, 0, j))],
        out_shape=[jax.ShapeDtypeStruct(x.shape, x.dtype),
                   jax.ShapeDtypeStruct((b, 2, D_RNN), x.dtype)],
        scratch_shapes=[pltpu.VMEM((t, d), BF16), pltpu.VMEM((t, d), F32),
                        pltpu.VMEM((t + (CONV_W - 1) * SUBLANES, cw), F32)]
                       + [pltpu.VMEM((t, cw), F32)] * 4,
        compiler_params=_params("arbitrary", "arbitrary"),
        name="lru_layer",
    )(x, mod, lng, lnb, w_in, w_in, conv_w, conv_b, wg, bg, lam, h0, w_out)
    return out, state


def _rope_rotate_cols(w):
    nf = ROPE_NF
    return jnp.concatenate([-w[..., nf:2 * nf], w[..., :nf], -w[..., 3 * nf:], w[..., 2 * nf:3 * nf]], axis=-1)


def _rope_tables(t_len):
    t = jnp.arange(t_len)
    rows = (t // GRID_W).astype(F32)
    cols = (t % GRID_W).astype(F32)
    inv = ROPE_THETA ** (-jnp.arange(ROPE_NF, dtype=F32) / ROPE_NF)
    ar, ac = rows[:, None] * inv, cols[:, None] * inv
    cos = jnp.concatenate([jnp.cos(ar), jnp.cos(ar), jnp.cos(ac), jnp.cos(ac)], axis=1)
    sin = jnp.concatenate([jnp.sin(ar), jnp.sin(ar), jnp.sin(ac), jnp.sin(ac)], axis=1)
    return (jnp.tile(cos, (1, MLA_HEADS)), jnp.tile(sin, (1, MLA_HEADS)),
            jnp.tile(cos, (1, 2)), jnp.tile(sin, (1, 2)))


def _mla_weights(w_dq, w_uq, w_dkv, w_uk, w_uv, w_o, rope):
    r = w_uq.shape[0]
    wq = [w_uq[:, :, :QK_NOPE].reshape(r, -1), w_uq[:, :, QK_NOPE:].reshape(r, -1)]
    wk = w_dkv[:, KV_LORA:]
    wdkv = [w_dkv[:, :KV_LORA], wk, wk]
    if rope:
        wq.append(_rope_rotate_cols(w_uq[:, :, QK_NOPE:]).reshape(r, -1))
        wk_rot = _rope_rotate_cols(wk)
        wdkv += [wk_rot, wk_rot]
    cat = lambda parts: jnp.concatenate(parts, axis=1).astype(BF16)
    return (w_dq.astype(BF16), cat(wq), cat(wdkv),
            w_uk.reshape(KV_LORA, -1).astype(BF16), w_uv.reshape(KV_LORA, -1).astype(BF16),
            w_o.astype(BF16))


def _lru_gate_weights(w_a, b_a, w_i, b_i):
    wg = jnp.concatenate([w_a[0], w_a[1], w_i[0], w_i[1]], axis=-1).astype(BF16)
    blk = lambda v: v.reshape(LRU_BLOCKS, 1, LRU_BW)
    bg = jnp.concatenate([blk(b_a[0]), blk(b_a[1]), blk(b_i[0]), blk(b_i[1])], axis=-1)
    return wg, bg


def kernel(x_prompt, x_sample, cache_mla_ckv, cache_mla_krope, state_lru, c, c_ctx, w_ada, b_ada, ln_g, ln_b, w_ffn_in, w_ffn_out, w_pool, pool_scale, w_dq, g_q, w_uq, w_dkv, g_kv, w_uk, w_uv, w_mla_o, w_lru_in, lru_conv_w, lru_conv_b, w_lru_a, b_lru_a, w_lru_i, b_lru_i, lru_lambda, w_lru_out):
    d = D_MODEL
    n_lat = c.shape[0]
    cond = jnp.concatenate([c_ctx[None], c, jnp.zeros((MOD_ROWS - 1 - n_lat, d), F32)], axis=0)
    mod = _modulation(cond, w_ada, b_ada)
    mods = (mod[:, :, 0:1, :][:, :, :, None, :],
            mod[:, :, 1:1 + n_lat, :][:, :, :, None, :])

    xs = [x_prompt, x_sample]
    ckv_out, krope_out, lru_out = [], [], []
    t_lat = x_sample.shape[1]
    tables = _rope_tables(t_lat)
    for i in range(DEPTH):
        kind, j = i % N_MIXERS, i // N_MIXERS
        row = lambda v: v.reshape(1, -1)
        lng0, lnb0, lng1, lnb1 = row(ln_g[i, 0]), row(ln_b[i, 0]), row(ln_g[i, 1]), row(ln_b[i, 1])
        win, wout = w_ffn_in[i].astype(BF16), w_ffn_out[i].astype(BF16)
        if kind == 0:
            wp, ps = w_pool[j].astype(BF16), row(pool_scale[j])
        elif kind == 1:
            mla_w = [_mla_weights(w_dq[j], w_uq[j], w_dkv[j], w_uk[j], w_uv[j], w_mla_o[j], rope)
                     for rope in (False, True)]
            gq, gkv = row(g_q[j]), row(g_kv[j])
        else:
            w_in = w_lru_in[j].astype(BF16)
            wg, bg = _lru_gate_weights(w_lru_a[j], b_lru_a[j], w_lru_i[j], b_lru_i[j])
            w_out = w_lru_out[j].astype(BF16)
        for p in range(2):
            x, m = xs[p], mods[p][i]
            nseq, t_len = (x_prompt, x_sample)[p].shape[:2]
            if kind == 0:
                x = _pool_layer(x, m, lng0, lnb0, wp, ps)
            elif kind == 1:
                wdq, wq, wdkv, wuk, wuv, wo = mla_w[p]
                if p == 0:
                    qn, qr, kn, v, kr, ckv, kr64 = _mla_project(
                        x, m, wdq, gq, wq, wdkv, gkv, wuk, wuv, None, True)
                    ckv_out.append(ckv)
                    krope_out.append(kr64)
                    segments = [(kn, kr, v)]
                else:
                    qn, qr, kn, v, kr = _mla_project(
                        x, m, wdq, gq, wq, wdkv, gkv, wuk, wuv, tables, False)
                    kn_c, v_c = _kv_up(cache_mla_ckv[:, j], wuk, wuv)
                    kr_c = cache_mla_krope[:, j]
                    kr_c = jnp.concatenate([kr_c, kr_c], axis=-1).astype(BF16)
                    segments = [(kn_c, kr_c, v_c), (kn, kr, v)]
                x = _attn_layer(x, m, lng0, lnb0, qn, qr, wo, segments)
            else:
                if p == 0:
                    h0 = jnp.zeros((nseq, 2, D_RNN), F32)
                else:
                    h0 = state_lru[:, j]
                x, st = _lru_layer(x.reshape(nseq, t_len, d), m, lng0, lnb0, w_in, lru_conv_w[j],
                                   row(lru_conv_b[j]), wg, bg, lru_lambda[j], h0, w_out)
                x = x.reshape(nseq, t_len // SUBLANES, SUBLANES * d)
                if p == 0:
                    lru_out.append(st)
            next_is_lru = i + 1 < DEPTH and (i + 1) % N_MIXERS == 2
            xs[p] = _ffn_layer(x, m, lng1, lnb1, win, wout, t_len, kind == 2, next_is_lru)
    return (xs[0], xs[1], jnp.stack(ckv_out, axis=1), jnp.stack(krope_out, axis=1),
            jnp.stack(lru_out, axis=1))
```

```python
import functools
import math

import jax
import jax.numpy as jnp
from jax import lax
from jax.experimental import pallas as pl
from jax.experimental.pallas import tpu as pltpu

F32 = jnp.float32
BF16 = jnp.bfloat16

D_MODEL = 1024
DEPTH = 4
GRID_W = 64
N_MIXERS = 3
POOL_WINDOWS = (2, 4, 8, 16)
POOL_GROUPS = 4
POOL_GC = D_MODEL // POOL_GROUPS
MLA_HEADS = 8
Q_LORA = 384
KV_LORA = 256
QK_NOPE = 128
QK_ROPE = 64
V_HEAD = 128
ROPE_NF = QK_ROPE // 4
ROPE_THETA = 10000.0
MLA_SCALE = (QK_NOPE + QK_ROPE) ** -0.5
D_RNN = D_MODEL
LRU_BLOCKS = 8
LRU_BW = D_RNN // LRU_BLOCKS
CONV_W = 4
CONV_LEFT = 1
LRU_C = 8.0
D_FF = ((8 * D_MODEL // 3 + 255) // 256) * 256
ALPHA = (2.0 * DEPTH) ** 0.25
EPS = 1e-6

LANES = 128
SUBLANES = 8
VMEM_LIMIT_BYTES = 56 * 1024 * 1024
PAD_ROWS = SUBLANES
MOD_ROWS = 16
LRU_COLS = 256
SEG_PITCH_PAD = 4
TOKEN_TILE = 256
QK_CAT = 2 * LANES


def _params(*sem):
    return pltpu.CompilerParams(dimension_semantics=sem, vmem_limit_bytes=VMEM_LIMIT_BYTES)


def _dot(a, b):
    return jnp.dot(a, b, preferred_element_type=F32)


def _layer_norm(y, g, b):
    mu = jnp.mean(y, axis=-1, keepdims=True)
    d = y - mu
    var = jnp.mean(d * d, axis=-1, keepdims=True)
    return d * lax.rsqrt(var + EPS) * g + b


def _rms_norm(y, g):
    return y * lax.rsqrt(jnp.mean(y * y, axis=-1, keepdims=True) + EPS) * g


def _modulate(x, mod_ref):
    return x * (1.0 + mod_ref[1, 0]) + mod_ref[0, 0]


def _finish(x, mixed, mod_ref, lng_ref, lnb_ref):
    return _layer_norm(ALPHA * x + mod_ref[2, 0] * mixed, lng_ref[...], lnb_ref[...])


def _mod_spec(mod, sub, nargs):
    per_batch = mod.shape[1] > 1
    if nargs == 1:
        return pl.BlockSpec((3, 1, 1, D_MODEL), lambda b: (sub, b if per_batch else 0, 0, 0))
    return pl.BlockSpec((3, 1, 1, D_MODEL), lambda b, t: (sub, b if per_batch else 0, 0, 0))


def _const_spec(shape, nargs):
    zeros = (0,) * len(shape)
    if nargs == 1:
        return pl.BlockSpec(shape, lambda b: zeros)
    return pl.BlockSpec(shape, lambda b, t: zeros)


def _mod_kernel(cond_ref, w_ref, b_ref, o_ref):
    s = jax.nn.silu(cond_ref[...]).astype(BF16)
    o_ref[0, 0] = _dot(s, w_ref[0].astype(BF16)) + b_ref[0, 0]


def _modulation(cond, w_ada, b_ada):
    d = D_MODEL
    return pl.pallas_call(
        _mod_kernel,
        grid=(DEPTH, 6),
        in_specs=[
            pl.BlockSpec((MOD_ROWS, d), lambda i, j: (0, 0)),
            pl.BlockSpec((1, d, d), lambda i, j: (i, 0, j)),
            pl.BlockSpec((1, 1, 1, d), lambda i, j: (i, j, 0, 0)),
        ],
        out_specs=pl.BlockSpec((1, 1, MOD_ROWS, d), lambda i, j: (i, j, 0, 0)),
        out_shape=jax.ShapeDtypeStruct((DEPTH, 6, MOD_ROWS, d), F32),
        compiler_params=_params("arbitrary", "arbitrary"),
        name="adaln_modulation",
    )(cond, w_ada, b_ada.reshape(DEPTH, 6, 1, d))


def _ffn_kernel(x_ref, mod_ref, lng_ref, lnb_ref, win_ref, wout_ref, o_ref):
    x = x_ref[0]
    h = _modulate(x, mod_ref).astype(BF16)
    ab = _dot(h, win_ref[...])
    u = (jax.nn.silu(ab[:, :D_FF]) * ab[:, D_FF:]).astype(BF16)
    o_ref[0] = _finish(x, _dot(u, wout_ref[...]), mod_ref, lng_ref, lnb_ref)


def _ffn_layer(x, mod, lng, lnb, win, wout):
    b, t, d = x.shape
    tm = TOKEN_TILE
    return pl.pallas_call(
        _ffn_kernel,
        grid=(b, t // tm),
        in_specs=[
            pl.BlockSpec((1, tm, d), lambda i, j: (i, j, 0)),
            _mod_spec(mod, 1, 2),
            _const_spec((1, d), 2),
            _const_spec((1, d), 2),
            _const_spec((d, 2 * D_FF), 2),
            _const_spec((D_FF, d), 2),
        ],
        out_specs=pl.BlockSpec((1, tm, d), lambda i, j: (i, j, 0)),
        out_shape=jax.ShapeDtypeStruct(x.shape, x.dtype),
        compiler_params=_params("arbitrary", "arbitrary"),
        name="ffn_layer",
    )(x, mod, lng, lnb, win, wout)


def _pool_kernel(x_ref, mod_ref, lng_ref, lnb_ref, wp_ref, ps_ref, o_ref, pad_ref):
    t_len = x_ref.shape[1]
    x = x_ref[0]
    h = _modulate(x, mod_ref)
    edge = jnp.zeros((PAD_ROWS, D_MODEL), F32)
    pad_ref[0:PAD_ROWS, :] = edge
    pad_ref[PAD_ROWS + t_len:2 * PAD_ROWS + t_len, :] = edge
    pad_ref[PAD_ROWS:PAD_ROWS + t_len, :] = h
    t = lax.broadcasted_iota(jnp.int32, (t_len, 1), 0)
    outs = []
    for gi, w in enumerate(POOL_WINDOWS):
        lo, hi = -(w // 2), w - w // 2
        c0 = gi * POOL_GC
        s = pad_ref[pl.ds(PAD_ROWS + lo, t_len), c0:c0 + POOL_GC]
        for k in range(lo + 1, hi):
            s = s + pad_ref[pl.ds(PAD_ROWS + k, t_len), c0:c0 + POOL_GC]
        cnt = (jnp.minimum(t + hi, t_len) - jnp.maximum(t + lo, 0)).astype(F32)
        pooled = s / cnt - h[:, c0:c0 + POOL_GC]
        outs.append(_dot(pooled.astype(BF16), wp_ref[gi]))
    mixed = jnp.concatenate(outs, axis=1) * ps_ref[...]
    o_ref[0] = _finish(x, mixed, mod_ref, lng_ref, lnb_ref)


def _pool_layer(x, mod, lng, lnb, wp, ps):
    b, t, d = x.shape
    return pl.pallas_call(
        _pool_kernel,
        grid=(b,),
        in_specs=[
            pl.BlockSpec((1, t, d), lambda i: (i, 0, 0)),
            _mod_spec(mod, 0, 1),
            _const_spec((1, d), 1),
            _const_spec((1, d), 1),
            _const_spec((POOL_GROUPS, POOL_GC, POOL_GC), 1),
            _const_spec((1, d), 1),
        ],
        out_specs=pl.BlockSpec((1, t, d), lambda i: (i, 0, 0)),
        out_shape=jax.ShapeDtypeStruct(x.shape, x.dtype),
        scratch_shapes=[pltpu.VMEM((t + 2 * PAD_ROWS, d), F32)],
        compiler_params=_params("arbitrary"),
        name="pool_layer",
    )(x, mod, lng, lnb, wp, ps)


def _store_k_cat(kcat_ref, k_nope, k_rope2):
    for hd in range(MLA_HEADS):
        kcat_ref[0, :, hd * QK_CAT:hd * QK_CAT + QK_NOPE] = k_nope[:, hd * QK_NOPE:(hd + 1) * QK_NOPE]
        kcat_ref[0, :, hd * QK_CAT + QK_NOPE:(hd + 1) * QK_CAT] = k_rope2


def _mla_proj_kernel(*refs, rope, emit_cache):
    x_ref, mod_ref, wdq_ref, gq_ref, wq_ref, wdkv_ref, gkv_ref, wuk_ref, wuv_ref = refs[:9]
    refs = refs[9:]
    if rope:
        wqrot_ref, cos_ref, sin_ref = refs[:3]
        refs = refs[3:]
    qcat_ref, kcat_ref, v_ref = refs[:3]
    h = _modulate(x_ref[0], mod_ref).astype(BF16)
    q_lat = _rms_norm(_dot(h, wdq_ref[...]), gq_ref[...]).astype(BF16)
    q = _dot(q_lat, wq_ref[...])
    kv = _dot(h, wdkv_ref[...])
    k_rope2 = kv[:, KV_LORA:KV_LORA + 2 * QK_ROPE]
    if rope:
        cos, sin = cos_ref[...], sin_ref[...]
        q_rot = _dot(q_lat, wqrot_ref[...])
        k_rope2 = k_rope2 * cos + kv[:, KV_LORA + 2 * QK_ROPE:] * sin
        for hd in range(MLA_HEADS):
            c0 = hd * QK_CAT
            qcat_ref[0, :, c0:c0 + QK_NOPE] = q[:, c0:c0 + QK_NOPE].astype(BF16)
            slot = q[:, c0 + QK_NOPE:c0 + QK_CAT] * cos + q_rot[:, hd * LANES:(hd + 1) * LANES] * sin
            qcat_ref[0, :, c0 + QK_NOPE:c0 + QK_CAT] = slot.astype(BF16)
    else:
        qcat_ref[0] = q.astype(BF16)
    c_kv = _rms_norm(kv[:, :KV_LORA], gkv_ref[...])
    c_kv16 = c_kv.astype(BF16)
    _store_k_cat(kcat_ref, _dot(c_kv16, wuk_ref[...]).astype(BF16), k_rope2.astype(BF16))
    v_ref[0] = _dot(c_kv16, wuv_ref[...]).astype(BF16)
    if emit_cache:
        ckv_out_ref, kr_out_ref = refs[3:5]
        ckv_out_ref[0] = c_kv
        kr_out_ref[0] = kv[:, KV_LORA:KV_LORA + QK_ROPE]


def _mla_project(x, mod, wdq, gq, wq, wdkv, gkv, wuk, wuv, rope_args, emit_cache):
    b, t, d = x.shape
    tm = TOKEN_TILE
    rope = rope_args is not None
    tok = lambda w: pl.BlockSpec((1, tm, w), lambda i, j: (i, j, 0))
    args = [x, mod, wdq, gq, wq, wdkv, gkv, wuk, wuv]
    in_specs = [tok(d), _mod_spec(mod, 0, 2)] + [_const_spec(a.shape, 2) for a in args[2:]]
    if rope:
        wqrot, cos, sin = rope_args
        args += [wqrot, cos, sin]
        in_specs += [_const_spec(wqrot.shape, 2)] + [pl.BlockSpec((tm, LANES), lambda i, j: (j, 0))] * 2
    widths = [MLA_HEADS * QK_CAT, MLA_HEADS * QK_CAT, MLA_HEADS * V_HEAD]
    out_specs = [tok(w) for w in widths]
    out_shape = [jax.ShapeDtypeStruct((b, t, w), BF16) for w in widths]
    if emit_cache:
        out_specs += [tok(KV_LORA), tok(QK_ROPE)]
        out_shape += [jax.ShapeDtypeStruct((b, t, KV_LORA), F32),
                      jax.ShapeDtypeStruct((b, t, QK_ROPE), F32)]
    return pl.pallas_call(
        functools.partial(_mla_proj_kernel, rope=rope, emit_cache=emit_cache),
        grid=(b, t // tm),
        in_specs=in_specs,
        out_specs=out_specs,
        out_shape=out_shape,
        compiler_params=_params("arbitrary", "arbitrary"),
        name="mla_project",
    )(*args)


def _kv_up_kernel(ckv_ref, kr2_ref, wuk_ref, wuv_ref, kcat_ref, v_ref):
    c = ckv_ref[0].astype(BF16)
    _store_k_cat(kcat_ref, _dot(c, wuk_ref[...]).astype(BF16), kr2_ref[0])
    v_ref[0] = _dot(c, wuv_ref[...]).astype(BF16)


def _kv_up(ckv, kr2, wuk, wuv):
    b, s, r = ckv.shape
    seq = lambda w: pl.BlockSpec((1, s, w), lambda i: (i, 0, 0))
    widths = [MLA_HEADS * QK_CAT, MLA_HEADS * V_HEAD]
    return pl.pallas_call(
        _kv_up_kernel,
        grid=(b,),
        in_specs=[seq(r), seq(kr2.shape[2]), _const_spec(wuk.shape, 1), _const_spec(wuv.shape, 1)],
        out_specs=[seq(w) for w in widths],
        out_shape=[jax.ShapeDtypeStruct((b, s, w), BF16) for w in widths],
        compiler_params=_params("arbitrary"),
        name="mla_cache_kv_up",
    )(ckv, kr2, wuk, wuv)


def _attn_kernel(*refs, n_seg):
    x_ref, mod_ref, lng_ref, lnb_ref, q_ref, wo_ref = refs[:6]
    seg_refs = [refs[6 + 2 * i:8 + 2 * i] for i in range(n_seg)]
    o_ref = refs[6 + 2 * n_seg]
    nt = (((1,), (1,)), ((), ()))
    exp2_scale = MLA_SCALE * math.log2(math.e)
    heads = []
    for hd in range(MLA_HEADS):
        q = q_ref[0, :, hd * QK_CAT:(hd + 1) * QK_CAT]
        scores = [lax.dot_general(q, k_ref[0, :, hd * QK_CAT:(hd + 1) * QK_CAT], nt,
                                  preferred_element_type=F32) for k_ref, _ in seg_refs]
        m = functools.reduce(jnp.maximum, [jnp.max(s, axis=-1, keepdims=True) for s in scores])
        es = [jnp.exp2((s - m) * exp2_scale) for s in scores]
        total = functools.reduce(lambda a, b: a + b, [jnp.sum(e, axis=-1, keepdims=True) for e in es])
        o = None
        for e, (_, v_ref) in zip(es, seg_refs):
            part = _dot(e.astype(BF16), v_ref[0, :, hd * V_HEAD:(hd + 1) * V_HEAD])
            o = part if o is None else o + part
        heads.append((o * (1.0 / total)).astype(BF16))
    mixed = _dot(jnp.concatenate(heads, axis=1), wo_ref[...])
    o_ref[0] = _finish(x_ref[0], mixed, mod_ref, lng_ref, lnb_ref)


def _attn_layer(x, mod, lng, lnb, qcat, wo, segments):
    b, t, d = x.shape
    tq = TOKEN_TILE
    tok = lambda w: pl.BlockSpec((1, tq, w), lambda i, j: (i, j, 0))
    in_specs = [tok(d), _mod_spec(mod, 0, 2), _const_spec((1, d), 2), _const_spec((1, d), 2),
                tok(qcat.shape[2]), _const_spec(wo.shape, 2)]
    args = [x, mod, lng, lnb, qcat, wo]
    for seg in segments:
        for a in seg:
            in_specs.append(pl.BlockSpec((1,) + a.shape[1:], lambda i, j: (i, 0, 0)))
            args.append(a)
    return pl.pallas_call(
        functools.partial(_attn_kernel, n_seg=len(segments)),
        grid=(b, t // tq),
        in_specs=in_specs,
        out_specs=tok(d),
        out_shape=jax.ShapeDtypeStruct(x.shape, x.dtype),
        compiler_params=_params("arbitrary", "arbitrary"),
        name="mla_attention",
    )(*args)


def _lru_kernel(x_ref, mod_ref, lng_ref, lnb_ref, wu_ref, wy_ref, cw_ref, cb_ref, wg_ref, bg_ref,
                lam_ref, h0_ref, wout_ref, o_ref, st_ref,
                perm_scr, h_scr, acc_scr, pad_scr, af_scr, bf_scr, ab_scr, bb_scr):
    t_len = x_ref.shape[1]
    n_blk = t_len // SUBLANES
    pitch = n_blk + SEG_PITCH_PAD
    n_slab = D_MODEL // LANES
    j = pl.program_id(1)
    slab = lambda c: slice(c * LANES, (c + 1) * LANES)

    @pl.when(j == 0)
    def _():
        for c in range(n_slab):
            for s in range(SUBLANES):
                perm_scr[c, s * pitch:s * pitch + n_blk, :] = x_ref[0, s * n_blk:(s + 1) * n_blk, slab(c)]

        def gather(k, carry):
            r = pl.multiple_of(k * SUBLANES, SUBLANES)
            for c in range(n_slab):
                acc_scr[pl.ds(r, SUBLANES), slab(c)] = perm_scr[c, pl.ds(k, SUBLANES, stride=pitch), :]
            return carry

        lax.fori_loop(0, n_blk, gather, 0, unroll=2)
        h_scr[...] = _modulate(acc_scr[...], mod_ref).astype(BF16)
        acc_scr[...] = jnp.zeros_like(acc_scr)

    h = h_scr[...]
    u = _dot(h, wu_ref[...])
    y = jax.nn.gelu(_dot(h, wy_ref[...]))

    sub = lax.broadcasted_iota(jnp.int32, (SUBLANES, LRU_COLS), 0)
    lead = CONV_LEFT * SUBLANES
    for i in range(1, CONV_LEFT + 1):
        blk = u[(n_blk - i) * SUBLANES:(n_blk - i + 1) * SUBLANES]
        pad_scr[lead - i * SUBLANES:lead - (i - 1) * SUBLANES, :] = jnp.where(
            sub >= 1, pltpu.roll(blk, 1, axis=0), 0.0)
    pad_scr[lead:lead + t_len, :] = u
    for i in range(CONV_W - 1 - CONV_LEFT):
        blk = u[i * SUBLANES:(i + 1) * SUBLANES]
        pad_scr[lead + t_len + i * SUBLANES:lead + t_len + (i + 1) * SUBLANES, :] = jnp.where(
            sub < SUBLANES - 1, pltpu.roll(blk, SUBLANES - 1, axis=0), 0.0)
    uc = None
    for k in range(CONV_W):
        term = pad_scr[k * SUBLANES:k * SUBLANES + t_len, :] * cw_ref[k:k + 1, :]
        uc = term if uc is None else uc + term
    uc = uc + cb_ref[...]

    a_scr = (af_scr, ab_scr)
    b_scr = (bf_scr, bb_scr)
    for bl in range(LRU_COLS // LRU_BW):
        c0 = bl * LRU_BW
        ub = uc[:, c0:c0 + LRU_BW]
        gates = jax.nn.sigmoid(_dot(ub.astype(BF16), wg_ref[bl]) + bg_ref[bl])
        for dr in range(2):
            r = gates[:, dr * LRU_BW:(dr + 1) * LRU_BW]
            ig = gates[:, (2 + dr) * LRU_BW:(3 + dr) * LRU_BW]
            log_a = -LRU_C * r * jax.nn.softplus(-lam_ref[dr:dr + 1, c0:c0 + LRU_BW])
            a_scr[dr][:, c0:c0 + LRU_BW] = jnp.exp(log_a)
            th = jnp.tanh(log_a)
            b_scr[dr][:, c0:c0 + LRU_BW] = jnp.sqrt(-2.0 * th / (1.0 - th)) * (ig * ub)

    def step(k, carry):
        a_f, b_f, a_b, b_b = carry
        rf = pl.multiple_of(k * SUBLANES, SUBLANES)
        rb = pl.multiple_of((n_blk - 1 - k) * SUBLANES, SUBLANES)
        a = af_scr[pl.ds(rf, SUBLANES), :]
        b_f = a * b_f + bf_scr[pl.ds(rf, SUBLANES), :]
        a_f = a * a_f
        af_scr[pl.ds(rf, SUBLANES), :] = a_f
        bf_scr[pl.ds(rf, SUBLANES), :] = b_f
        a = ab_scr[pl.ds(rb, SUBLANES), :]
        b_b = a * b_b + bb_scr[pl.ds(rb, SUBLANES), :]
        a_b = a * a_b
        ab_scr[pl.ds(rb, SUBLANES), :] = a_b
        bb_scr[pl.ds(rb, SUBLANES), :] = b_b
        return a_f, b_f, a_b, b_b

    one = jnp.ones((SUBLANES, LRU_COLS), F32)
    zero = jnp.zeros((SUBLANES, LRU_COLS), F32)
    a_f, b_f, a_b, b_b = lax.fori_loop(0, n_blk, step, (one, zero, one, zero), unroll=4)

    hf = h0_ref[0, 0:1, :]
    hf_in = zero
    for s in range(SUBLANES):
        hf_in = jnp.where(sub == s, hf, hf_in)
        hf = a_f[s:s + 1] * hf + b_f[s:s + 1]
    hb = h0_ref[0, 1:2, :]
    hb_in = zero
    for s in reversed(range(SUBLANES)):
        hb_in = jnp.where(sub == s, hb, hb_in)
        hb = a_b[s:s + 1] * hb + b_b[s:s + 1]
    st_ref[0, 0:1, :] = hf
    st_ref[0, 1:2, :] = hb

    blocks = lambda ref: ref[...].reshape(n_blk, SUBLANES, LRU_COLS)
    h_sum = (blocks(af_scr) * hf_in + blocks(bf_scr)) + (blocks(ab_scr) * hb_in + blocks(bb_scr))
    mixed = (h_sum.reshape(t_len, LRU_COLS) * y).astype(BF16)
    acc_scr[...] += _dot(mixed, wout_ref[...])

    @pl.when(j == pl.num_programs(1) - 1)
    def _():
        def scatter(k, carry):
            r = pl.multiple_of(k * SUBLANES, SUBLANES)
            for c in range(n_slab):
                perm_scr[c, pl.ds(k, SUBLANES, stride=pitch), :] = acc_scr[pl.ds(r, SUBLANES), slab(c)]
            return carry

        lax.fori_loop(0, n_blk, scatter, 0, unroll=2)
        for s in range(SUBLANES):
            rows = slice(s * n_blk, (s + 1) * n_blk)
            mixed_s = jnp.concatenate(
                [perm_scr[c, s * pitch:s * pitch + n_blk, :] for c in range(n_slab)], axis=1)
            o_ref[0, rows, :] = _finish(x_ref[0, rows, :], mixed_s, mod_ref, lng_ref, lnb_ref)


def _lru_layer(x, mod, lng, lnb, w_in, conv_w, conv_b, wg, bg, lam, h0, w_out):
    b, t, d = x.shape
    cw = LRU_COLS
    ncb = D_RNN // cw
    bpc = cw // LRU_BW
    pitch = t // SUBLANES + SEG_PITCH_PAD
    seq = lambda w: pl.BlockSpec((1, t, w), lambda i, j: (i, 0, 0))
    col = lambda rows: pl.BlockSpec((rows, cw), lambda i, j: (0, j))
    out, state = pl.pallas_call(
        _lru_kernel,
        grid=(b, ncb),
        in_specs=[
            seq(d),
            _mod_spec(mod, 0, 2),
            _const_spec((1, d), 2),
            _const_spec((1, d), 2),
            pl.BlockSpec((d, cw), lambda i, j: (0, j)),
            pl.BlockSpec((d, cw), lambda i, j: (0, ncb + j)),
            col(CONV_W),
            col(1),
            pl.BlockSpec((bpc, LRU_BW, 4 * LRU_BW), lambda i, j: (j, 0, 0)),
            pl.BlockSpec((bpc, 1, 4 * LRU_BW), lambda i, j: (j, 0, 0)),
            col(2),
            pl.BlockSpec((1, 2, cw), lambda i, j: (i, 0, j)),
            pl.BlockSpec((cw, d), lambda i, j: (j, 0)),
        ],
        out_specs=[seq(d), pl.BlockSpec((1, 2, cw), lambda i, j: (i, 0, j))],
        out_shape=[jax.ShapeDtypeStruct(x.shape, x.dtype),
                   jax.ShapeDtypeStruct((b, 2, D_RNN), x.dtype)],
        scratch_shapes=[pltpu.VMEM((d // LANES, SUBLANES * pitch, LANES), F32),
                        pltpu.VMEM((t, d), BF16), pltpu.VMEM((t, d), F32),
                        pltpu.VMEM((t + (CONV_W - 1) * SUBLANES, cw), F32)]
                       + [pltpu.VMEM((t, cw), F32)] * 4,
        compiler_params=_params("arbitrary", "arbitrary"),
        name="lru_layer",
    )(x, mod, lng, lnb, w_in, w_in, conv_w, conv_b, wg, bg, lam, h0, w_out)
    return out, state


def _rope_rotate_cols(w):
    nf = ROPE_NF
    return jnp.concatenate([-w[..., nf:2 * nf], w[..., :nf], -w[..., 3 * nf:], w[..., 2 * nf:3 * nf]], axis=-1)


def _rope_tables(t_len):
    t = jnp.arange(t_len)
    rows = (t // GRID_W).astype(F32)
    cols = (t % GRID_W).astype(F32)
    inv = ROPE_THETA ** (-jnp.arange(ROPE_NF, dtype=F32) / ROPE_NF)
    ar, ac = rows[:, None] * inv, cols[:, None] * inv
    cos = jnp.concatenate([jnp.cos(ar), jnp.cos(ar), jnp.cos(ac), jnp.cos(ac)], axis=1)
    sin = jnp.concatenate([jnp.sin(ar), jnp.sin(ar), jnp.sin(ac), jnp.sin(ac)], axis=1)
    return jnp.tile(cos, (1, 2)), jnp.tile(sin, (1, 2))


def _parity_slots(w_rope):
    z = jnp.zeros_like(w_rope)
    even = (jnp.arange(MLA_HEADS) % 2 == 0)[None, :, None]
    return jnp.concatenate([jnp.where(even, w_rope, z), jnp.where(even, z, w_rope)], axis=-1)


def _mla_weights(w_dq, w_uq, w_dkv, w_uk, w_uv, w_o, rope):
    r = w_uq.shape[0]
    w_nope, w_rope = w_uq[:, :, :QK_NOPE], w_uq[:, :, QK_NOPE:]
    wq = jnp.concatenate([w_nope, _parity_slots(w_rope)], axis=-1).reshape(r, -1).astype(BF16)
    wk = w_dkv[:, KV_LORA:]
    wdkv = [w_dkv[:, :KV_LORA], wk, wk]
    wq_rot = None
    if rope:
        wq_rot = _parity_slots(_rope_rotate_cols(w_rope)).reshape(r, -1).astype(BF16)
        wk_rot = _rope_rotate_cols(wk)
        wdkv += [wk_rot, wk_rot]
    return (w_dq.astype(BF16), wq, jnp.concatenate(wdkv, axis=1).astype(BF16),
            w_uk.reshape(KV_LORA, -1).astype(BF16), w_uv.reshape(KV_LORA, -1).astype(BF16),
            w_o.astype(BF16), wq_rot)


def _lru_gate_weights(w_a, b_a, w_i, b_i):
    wg = jnp.concatenate([w_a[0], w_a[1], w_i[0], w_i[1]], axis=-1).astype(BF16)
    blk = lambda v: v.reshape(LRU_BLOCKS, 1, LRU_BW)
    bg = jnp.concatenate([blk(b_a[0]), blk(b_a[1]), blk(b_i[0]), blk(b_i[1])], axis=-1)
    return wg, bg


def kernel(x_prompt, x_sample, cache_mla_ckv, cache_mla_krope, state_lru, c, c_ctx, w_ada, b_ada, ln_g, ln_b, w_ffn_in, w_ffn_out, w_pool, pool_scale, w_dq, g_q, w_uq, w_dkv, g_kv, w_uk, w_uv, w_mla_o, w_lru_in, lru_conv_w, lru_conv_b, w_lru_a, b_lru_a, w_lru_i, b_lru_i, lru_lambda, w_lru_out):
    d = D_MODEL
    n_lat = c.shape[0]
    cond = jnp.concatenate([c_ctx[None], c, jnp.zeros((MOD_ROWS - 1 - n_lat, d), F32)], axis=0)
    mod = _modulation(cond, w_ada, b_ada)
    mods = (mod[:, :, 0:1, :][:, :, :, None, :],
            mod[:, :, 1:1 + n_lat, :][:, :, :, None, :])

    xs = [x_prompt, x_sample]
    ckv_out, krope_out, lru_out = [], [], []
    cos, sin = _rope_tables(x_sample.shape[1])
    for i in range(DEPTH):
        kind, j = i % N_MIXERS, i // N_MIXERS
        row = lambda v: v.reshape(1, -1)
        lng0, lnb0, lng1, lnb1 = row(ln_g[i, 0]), row(ln_b[i, 0]), row(ln_g[i, 1]), row(ln_b[i, 1])
        win, wout = w_ffn_in[i].astype(BF16), w_ffn_out[i].astype(BF16)
        if kind == 0:
            wp, ps = w_pool[j].astype(BF16), row(pool_scale[j])
        elif kind == 1:
            mla_w = [_mla_weights(w_dq[j], w_uq[j], w_dkv[j], w_uk[j], w_uv[j], w_mla_o[j], rope)
                     for rope in (False, True)]
            gq, gkv = row(g_q[j]), row(g_kv[j])
        else:
            w_in = w_lru_in[j].astype(BF16)
            wg, bg = _lru_gate_weights(w_lru_a[j], b_lru_a[j], w_lru_i[j], b_lru_i[j])
            w_out = w_lru_out[j].astype(BF16)
        for p in range(2):
            x, m = xs[p], mods[p][i]
            if kind == 0:
                x = _pool_layer(x, m, lng0, lnb0, wp, ps)
            elif kind == 1:
                wdq, wq, wdkv, wuk, wuv, wo, wq_rot = mla_w[p]
                if p == 0:
                    qcat, kcat, v, ckv, kr64 = _mla_project(
                        x, m, wdq, gq, wq, wdkv, gkv, wuk, wuv, None, True)
                    ckv_out.append(ckv)
                    krope_out.append(kr64)
                    segments = [(kcat, v)]
                else:
                    qcat, kcat, v = _mla_project(
                        x, m, wdq, gq, wq, wdkv, gkv, wuk, wuv, (wq_rot, cos, sin), False)
                    kr_c = cache_mla_krope[:, j]
                    kr_c = jnp.concatenate([kr_c, kr_c], axis=-1).astype(BF16)
                    kcat_c, v_c = _kv_up(cache_mla_ckv[:, j], kr_c, wuk, wuv)
                    segments = [(kcat_c, v_c), (kcat, v)]
                x = _attn_layer(x, m, lng0, lnb0, qcat, wo, segments)
            else:
                if p == 0:
                    h0 = jnp.zeros((x.shape[0], 2, D_RNN), F32)
                else:
                    h0 = state_lru[:, j]
                x, st = _lru_layer(x, m, lng0, lnb0, w_in, lru_conv_w[j], row(lru_conv_b[j]),
                                   wg, bg, lru_lambda[j], h0, w_out)
                if p == 0:
                    lru_out.append(st)
            xs[p] = _ffn_layer(x, m, lng1, lnb1, win, wout)
    return (xs[0], xs[1], jnp.stack(ckv_out, axis=1), jnp.stack(krope_out, axis=1),
            jnp.stack(lru_out, axis=1))
```

```python
import functools
import math

import jax
import jax.numpy as jnp
from jax import lax
from jax.experimental import pallas as pl
from jax.experimental.pallas import tpu as pltpu

F32 = jnp.float32
BF16 = jnp.bfloat16

D_MODEL = 1024
DEPTH = 4
GRID_W = 64
N_MIXERS = 3
POOL_WINDOWS = (2, 4, 8, 16)
POOL_GROUPS = 4
POOL_GC = D_MODEL // POOL_GROUPS
MLA_HEADS = 8
Q_LORA = 384
KV_LORA = 256
QK_NOPE = 128
QK_ROPE = 64
V_HEAD = 128
ROPE_NF = QK_ROPE // 4
ROPE_THETA = 10000.0
MLA_SCALE = (QK_NOPE + QK_ROPE) ** -0.5
D_RNN = D_MODEL
LRU_BLOCKS = 8
LRU_BW = D_RNN // LRU_BLOCKS
CONV_W = 4
CONV_LEFT = 1
LRU_C = 8.0
D_FF = ((8 * D_MODEL // 3 + 255) // 256) * 256
ALPHA = (2.0 * DEPTH) ** 0.25
EPS = 1e-6

LANES = 128
SUBLANES = 8
VMEM_LIMIT_BYTES = 56 * 1024 * 1024
PAD_ROWS = SUBLANES
MOD_ROWS = 16
LRU_COLS = 512
SEG_PITCH_PAD = 4
TOKEN_TILE = 256
ATTN_TILE = 512
QK_CAT = 2 * LANES


def _params(*sem):
    return pltpu.CompilerParams(dimension_semantics=sem, vmem_limit_bytes=VMEM_LIMIT_BYTES)


def _dot(a, b):
    return jnp.dot(a, b, preferred_element_type=F32)


def _layer_norm(y, g, b):
    mu = jnp.mean(y, axis=-1, keepdims=True)
    d = y - mu
    var = jnp.mean(d * d, axis=-1, keepdims=True)
    return d * lax.rsqrt(var + EPS) * g + b


def _gelu_tanh(x):
    c = math.sqrt(2.0 / math.pi)
    return (0.5 * x) * (1.0 + jnp.tanh(x * (c + (c * 0.044715) * (x * x))))


def _rms_norm(y, g):
    return y * lax.rsqrt(jnp.mean(y * y, axis=-1, keepdims=True) + EPS) * g


def _modulate(x, mod_ref):
    return x * (1.0 + mod_ref[1, 0]) + mod_ref[0, 0]


def _finish(x, mixed, mod_ref, lng_ref, lnb_ref):
    return _layer_norm(ALPHA * x + mod_ref[2, 0] * mixed, lng_ref[...], lnb_ref[...])


def _mod_spec(mod, sub, nargs):
    per_batch = mod.shape[1] > 1
    if nargs == 1:
        return pl.BlockSpec((3, 1, 1, D_MODEL), lambda b: (sub, b if per_batch else 0, 0, 0))
    return pl.BlockSpec((3, 1, 1, D_MODEL), lambda b, t: (sub, b if per_batch else 0, 0, 0))


def _const_spec(shape, nargs):
    zeros = (0,) * len(shape)
    if nargs == 1:
        return pl.BlockSpec(shape, lambda b: zeros)
    return pl.BlockSpec(shape, lambda b, t: zeros)


def _mod_kernel(cond_ref, w_ref, b_ref, o_ref):
    s = jax.nn.silu(cond_ref[...]).astype(BF16)
    o_ref[0, 0] = _dot(s, w_ref[0].astype(BF16)) + b_ref[0, 0]


def _modulation(cond, w_ada, b_ada):
    d = D_MODEL
    return pl.pallas_call(
        _mod_kernel,
        grid=(DEPTH, 6),
        in_specs=[
            pl.BlockSpec((MOD_ROWS, d), lambda i, j: (0, 0)),
            pl.BlockSpec((1, d, d), lambda i, j: (i, 0, j)),
            pl.BlockSpec((1, 1, 1, d), lambda i, j: (i, j, 0, 0)),
        ],
        out_specs=pl.BlockSpec((1, 1, MOD_ROWS, d), lambda i, j: (i, j, 0, 0)),
        out_shape=jax.ShapeDtypeStruct((DEPTH, 6, MOD_ROWS, d), F32),
        compiler_params=_params("arbitrary", "arbitrary"),
        name="adaln_modulation",
    )(cond, w_ada, b_ada.reshape(DEPTH, 6, 1, d))


def _ffn_kernel(x_ref, mod_ref, lng_ref, lnb_ref, win_ref, wout_ref, o_ref):
    x = x_ref[0]
    h = _modulate(x, mod_ref).astype(BF16)
    ab = _dot(h, win_ref[0])
    u = (jax.nn.silu(ab[:, :D_FF]) * ab[:, D_FF:]).astype(BF16)
    o_ref[0] = _finish(x, _dot(u, wout_ref[0]), mod_ref, lng_ref, lnb_ref)


def _ffn_layer(x, mod, lng, lnb, win, wout, layer):
    b, t, d = x.shape
    tm = TOKEN_TILE
    return pl.pallas_call(
        _ffn_kernel,
        grid=(b, t // tm),
        in_specs=[
            pl.BlockSpec((1, tm, d), lambda i, j: (i, j, 0)),
            _mod_spec(mod, 1, 2),
            _const_spec((1, d), 2),
            _const_spec((1, d), 2),
            pl.BlockSpec((1, d, 2 * D_FF), lambda i, j: (layer, 0, 0)),
            pl.BlockSpec((1, D_FF, d), lambda i, j: (layer, 0, 0)),
        ],
        out_specs=pl.BlockSpec((1, tm, d), lambda i, j: (i, j, 0)),
        out_shape=jax.ShapeDtypeStruct(x.shape, x.dtype),
        compiler_params=_params("arbitrary", "arbitrary"),
        name="ffn_layer",
    )(x, mod, lng, lnb, win, wout)


def _pool_kernel(x_ref, mod_ref, lng_ref, lnb_ref, wp_ref, ps_ref, o_ref):
    t_len = x_ref.shape[1]
    n_pad = t_len + 2 * PAD_ROWS
    x = x_ref[0]
    h = _modulate(x, mod_ref)
    edge = jnp.zeros((PAD_ROWS, POOL_GC), F32)
    t = lax.broadcasted_iota(jnp.int32, (t_len, 1), 0)
    outs = []
    for gi, w in enumerate(POOL_WINDOWS):
        assert w & (w - 1) == 0 and w // 2 <= PAD_ROWS
        hg = h[:, gi * POOL_GC:(gi + 1) * POOL_GC]
        run = jnp.concatenate([edge, hg, edge], axis=0)
        span = 1
        while 2 * span < w:
            run = run + pltpu.roll(run, n_pad - span, axis=0)
            span *= 2
        s = (run + pltpu.roll(run, span, axis=0))[PAD_ROWS:PAD_ROWS + t_len]
        lo, hi = -(w // 2), w - w // 2
        cnt = (jnp.minimum(t + hi, t_len) - jnp.maximum(t + lo, 0)).astype(F32)
        pooled = s / cnt - hg
        outs.append(_dot(pooled.astype(BF16), wp_ref[gi]))
    mixed = jnp.concatenate(outs, axis=1) * ps_ref[...]
    o_ref[0] = _finish(x, mixed, mod_ref, lng_ref, lnb_ref)


def _pool_layer(x, mod, lng, lnb, wp, ps):
    b, t, d = x.shape
    return pl.pallas_call(
        _pool_kernel,
        grid=(b,),
        in_specs=[
            pl.BlockSpec((1, t, d), lambda i: (i, 0, 0)),
            _mod_spec(mod, 0, 1),
            _const_spec((1, d), 1),
            _const_spec((1, d), 1),
            _const_spec((POOL_GROUPS, POOL_GC, POOL_GC), 1),
            _const_spec((1, d), 1),
        ],
        out_specs=pl.BlockSpec((1, t, d), lambda i: (i, 0, 0)),
        out_shape=jax.ShapeDtypeStruct(x.shape, x.dtype),
        compiler_params=_params("arbitrary"),
        name="pool_layer",
    )(x, mod, lng, lnb, wp, ps)


def _store_k_cat(kcat_ref, k_nope, k_rope2):
    for hd in range(MLA_HEADS):
        kcat_ref[0, :, hd * QK_CAT:hd * QK_CAT + QK_NOPE] = k_nope[:, hd * QK_NOPE:(hd + 1) * QK_NOPE]
        kcat_ref[0, :, hd * QK_CAT + QK_NOPE:(hd + 1) * QK_CAT] = k_rope2


def _mla_proj_kernel(*refs, rope, emit_cache):
    x_ref, mod_ref, wdq_ref, gq_ref, wq_ref, wdkv_ref, gkv_ref, wuk_ref, wuv_ref = refs[:9]
    refs = refs[9:]
    if rope:
        wqrot_ref, cos_ref, sin_ref = refs[:3]
        refs = refs[3:]
    qcat_ref, kcat_ref, v_ref = refs[:3]
    h = _modulate(x_ref[0], mod_ref).astype(BF16)
    q_lat = _rms_norm(_dot(h, wdq_ref[...]), gq_ref[...]).astype(BF16)
    q = _dot(q_lat, wq_ref[...])
    kv = _dot(h, wdkv_ref[...])
    k_rope2 = kv[:, KV_LORA:KV_LORA + 2 * QK_ROPE]
    if rope:
        cos, sin = cos_ref[...], sin_ref[...]
        q_rot = _dot(q_lat, wqrot_ref[...])
        k_rope2 = k_rope2 * cos + kv[:, KV_LORA + 2 * QK_ROPE:] * sin
        for hd in range(MLA_HEADS):
            c0 = hd * QK_CAT
            qcat_ref[0, :, c0:c0 + QK_NOPE] = q[:, c0:c0 + QK_NOPE].astype(BF16)
            slot = q[:, c0 + QK_NOPE:c0 + QK_CAT] * cos + q_rot[:, hd * LANES:(hd + 1) * LANES] * sin
            qcat_ref[0, :, c0 + QK_NOPE:c0 + QK_CAT] = slot.astype(BF16)
    else:
        qcat_ref[0] = q.astype(BF16)
    c_kv = _rms_norm(kv[:, :KV_LORA], gkv_ref[...])
    c_kv16 = c_kv.astype(BF16)
    _store_k_cat(kcat_ref, _dot(c_kv16, wuk_ref[...]).astype(BF16), k_rope2.astype(BF16))
    v_ref[0] = _dot(c_kv16, wuv_ref[...]).astype(BF16)
    if emit_cache:
        ckv_out_ref, kr_out_ref = refs[3:5]
        ckv_out_ref[0] = c_kv
        kr_out_ref[0] = kv[:, KV_LORA:KV_LORA + QK_ROPE]


def _mla_project(x, mod, wdq, gq, wq, wdkv, gkv, wuk, wuv, rope_args, emit_cache):
    b, t, d = x.shape
    tm = TOKEN_TILE
    rope = rope_args is not None
    tok = lambda w: pl.BlockSpec((1, tm, w), lambda i, j: (i, j, 0))
    args = [x, mod, wdq, gq, wq, wdkv, gkv, wuk, wuv]
    in_specs = [tok(d), _mod_spec(mod, 0, 2)] + [_const_spec(a.shape, 2) for a in args[2:]]
    if rope:
        wqrot, cos, sin = rope_args
        args += [wqrot, cos, sin]
        in_specs += [_const_spec(wqrot.shape, 2)] + [pl.BlockSpec((tm, LANES), lambda i, j: (j, 0))] * 2
    widths = [MLA_HEADS * QK_CAT, MLA_HEADS * QK_CAT, MLA_HEADS * V_HEAD]
    out_specs = [tok(w) for w in widths]
    out_shape = [jax.ShapeDtypeStruct((b, t, w), BF16) for w in widths]
    if emit_cache:
        out_specs += [tok(KV_LORA), tok(QK_ROPE)]
        out_shape += [jax.ShapeDtypeStruct((b, t, KV_LORA), F32),
                      jax.ShapeDtypeStruct((b, t, QK_ROPE), F32)]
    return pl.pallas_call(
        functools.partial(_mla_proj_kernel, rope=rope, emit_cache=emit_cache),
        grid=(b, t // tm),
        in_specs=in_specs,
        out_specs=out_specs,
        out_shape=out_shape,
        compiler_params=_params("arbitrary", "arbitrary"),
        name="mla_project",
    )(*args)


def _kv_up_kernel(ckv_ref, kr2_ref, wuk_ref, wuv_ref, kcat_ref, v_ref):
    c = ckv_ref[0].astype(BF16)
    _store_k_cat(kcat_ref, _dot(c, wuk_ref[...]).astype(BF16), kr2_ref[0])
    v_ref[0] = _dot(c, wuv_ref[...]).astype(BF16)


def _kv_up(ckv, kr2, wuk, wuv):
    b, s, r = ckv.shape
    seq = lambda w: pl.BlockSpec((1, s, w), lambda i: (i, 0, 0))
    widths = [MLA_HEADS * QK_CAT, MLA_HEADS * V_HEAD]
    return pl.pallas_call(
        _kv_up_kernel,
        grid=(b,),
        in_specs=[seq(r), seq(kr2.shape[2]), _const_spec(wuk.shape, 1), _const_spec(wuv.shape, 1)],
        out_specs=[seq(w) for w in widths],
        out_shape=[jax.ShapeDtypeStruct((b, s, w), BF16) for w in widths],
        compiler_params=_params("arbitrary"),
        name="mla_cache_kv_up",
    )(ckv, kr2, wuk, wuv)


def _attn_kernel(*refs, n_seg):
    x_ref, mod_ref, lng_ref, lnb_ref, q_ref, wo_ref = refs[:6]
    seg_refs = [refs[6 + 2 * i:8 + 2 * i] for i in range(n_seg)]
    o_ref = refs[6 + 2 * n_seg]
    nt = (((1,), (1,)), ((), ()))
    exp2_scale = MLA_SCALE * math.log2(math.e)
    heads = []
    for hd in range(MLA_HEADS):
        q = q_ref[0, :, hd * QK_CAT:(hd + 1) * QK_CAT]
        scores = [lax.dot_general(q, k_ref[0, :, hd * QK_CAT:(hd + 1) * QK_CAT], nt,
                                  preferred_element_type=F32) for k_ref, _ in seg_refs]
        m = functools.reduce(jnp.maximum, [jnp.max(s, axis=-1, keepdims=True) for s in scores])
        es = [jnp.exp2((s - m) * exp2_scale) for s in scores]
        total = functools.reduce(lambda a, b: a + b, [jnp.sum(e, axis=-1, keepdims=True) for e in es])
        o = None
        for e, (_, v_ref) in zip(es, seg_refs):
            part = _dot(e.astype(BF16), v_ref[0, :, hd * V_HEAD:(hd + 1) * V_HEAD])
            o = part if o is None else o + part
        heads.append((o * (1.0 / total)).astype(BF16))
    mixed = _dot(jnp.concatenate(heads, axis=1), wo_ref[...])
    o_ref[0] = _finish(x_ref[0], mixed, mod_ref, lng_ref, lnb_ref)


def _attn_layer(x, mod, lng, lnb, qcat, wo, segments):
    b, t, d = x.shape
    tq = min(ATTN_TILE, t)
    tok = lambda w: pl.BlockSpec((1, tq, w), lambda i, j: (i, j, 0))
    in_specs = [tok(d), _mod_spec(mod, 0, 2), _const_spec((1, d), 2), _const_spec((1, d), 2),
                tok(qcat.shape[2]), _const_spec(wo.shape, 2)]
    args = [x, mod, lng, lnb, qcat, wo]
    for seg in segments:
        for a in seg:
            in_specs.append(pl.BlockSpec((1,) + a.shape[1:], lambda i, j: (i, 0, 0)))
            args.append(a)
    return pl.pallas_call(
        functools.partial(_attn_kernel, n_seg=len(segments)),
        grid=(b, t // tq),
        in_specs=in_specs,
        out_specs=tok(d),
        out_shape=jax.ShapeDtypeStruct(x.shape, x.dtype),
        compiler_params=_params("arbitrary", "arbitrary"),
        name="mla_attention",
    )(*args)


def _lru_kernel(x_ref, mod_ref, lng_ref, lnb_ref, wu_ref, wy_ref, cw_ref, cb_ref, wg_ref, bg_ref,
                lam_ref, h0_ref, wout_ref, o_ref, st_ref,
                perm_scr, h_scr, acc_scr, pad_scr, af_scr, bf_scr, ab_scr, bb_scr):
    t_len = x_ref.shape[1]
    n_blk = t_len // SUBLANES
    pitch = n_blk + SEG_PITCH_PAD
    n_slab = D_MODEL // LANES
    j = pl.program_id(1)
    slab = lambda c: slice(c * LANES, (c + 1) * LANES)

    @pl.when(j == 0)
    def _():
        for c in range(n_slab):
            for s in range(SUBLANES):
                perm_scr[c, s * pitch:s * pitch + n_blk, :] = x_ref[0, s * n_blk:(s + 1) * n_blk, slab(c)]

        def gather(k, carry):
            r = pl.multiple_of(k * SUBLANES, SUBLANES)
            for c in range(n_slab):
                acc_scr[pl.ds(r, SUBLANES), slab(c)] = perm_scr[c, pl.ds(k, SUBLANES, stride=pitch), :]
            return carry

        lax.fori_loop(0, n_blk, gather, 0, unroll=2)
        h_scr[...] = _modulate(acc_scr[...], mod_ref).astype(BF16)
        acc_scr[...] = jnp.zeros_like(acc_scr)

    h = h_scr[...]
    u = _dot(h, wu_ref[...])
    y = _gelu_tanh(_dot(h, wy_ref[...]))

    sub = lax.broadcasted_iota(jnp.int32, (SUBLANES, LRU_COLS), 0)
    lead = CONV_LEFT * SUBLANES
    for i in range(1, CONV_LEFT + 1):
        blk = u[(n_blk - i) * SUBLANES:(n_blk - i + 1) * SUBLANES]
        pad_scr[lead - i * SUBLANES:lead - (i - 1) * SUBLANES, :] = jnp.where(
            sub >= 1, pltpu.roll(blk, 1, axis=0), 0.0)
    pad_scr[lead:lead + t_len, :] = u
    for i in range(CONV_W - 1 - CONV_LEFT):
        blk = u[i * SUBLANES:(i + 1) * SUBLANES]
        pad_scr[lead + t_len + i * SUBLANES:lead + t_len + (i + 1) * SUBLANES, :] = jnp.where(
            sub < SUBLANES - 1, pltpu.roll(blk, SUBLANES - 1, axis=0), 0.0)
    uc = None
    for k in range(CONV_W):
        term = pad_scr[k * SUBLANES:k * SUBLANES + t_len, :] * cw_ref[k:k + 1, :]
        uc = term if uc is None else uc + term
    uc = uc + cb_ref[...]

    a_scr = (af_scr, ab_scr)
    b_scr = (bf_scr, bb_scr)
    for bl in range(LRU_COLS // LRU_BW):
        c0 = bl * LRU_BW
        ub = uc[:, c0:c0 + LRU_BW]
        tg = jnp.tanh(0.5 * (_dot(ub.astype(BF16), wg_ref[bl]) + bg_ref[bl]))
        for dr in range(2):
            tr = tg[:, dr * LRU_BW:(dr + 1) * LRU_BW]
            ti = tg[:, (2 + dr) * LRU_BW:(3 + dr) * LRU_BW]
            half_sp = (-0.25 * LRU_C) * jax.nn.softplus(-lam_ref[dr:dr + 1, c0:c0 + LRU_BW])
            t = jnp.tanh(half_sp * (1.0 + tr))
            inv = 1.0 / (1.0 - t)
            nt = -t
            root = jnp.where(nt > 0.0, nt * lax.rsqrt(nt), 0.0)
            a_scr[dr][:, c0:c0 + LRU_BW] = (1.0 + t) * inv
            b_scr[dr][:, c0:c0 + LRU_BW] = (root * inv) * ((1.0 + ti) * ub)

    def step(k, carry):
        a_f, b_f, a_b, b_b = carry
        rf = pl.multiple_of(k * SUBLANES, SUBLANES)
        rb = pl.multiple_of((n_blk - 1 - k) * SUBLANES, SUBLANES)
        a = af_scr[pl.ds(rf, SUBLANES), :]
        b_f = a * b_f + bf_scr[pl.ds(rf, SUBLANES), :]
        a_f = a * a_f
        af_scr[pl.ds(rf, SUBLANES), :] = a_f
        bf_scr[pl.ds(rf, SUBLANES), :] = b_f
        a = ab_scr[pl.ds(rb, SUBLANES), :]
        b_b = a * b_b + bb_scr[pl.ds(rb, SUBLANES), :]
        a_b = a * a_b
        ab_scr[pl.ds(rb, SUBLANES), :] = a_b
        bb_scr[pl.ds(rb, SUBLANES), :] = b_b
        return a_f, b_f, a_b, b_b

    one = jnp.ones((SUBLANES, LRU_COLS), F32)
    zero = jnp.zeros((SUBLANES, LRU_COLS), F32)
    a_f, b_f, a_b, b_b = lax.fori_loop(0, n_blk, step, (one, zero, one, zero), unroll=4)

    hf = h0_ref[0, 0:1, :]
    hf_in = zero
    for s in range(SUBLANES):
        hf_in = jnp.where(sub == s, hf, hf_in)
        hf = a_f[s:s + 1] * hf + b_f[s:s + 1]
    hb = h0_ref[0, 1:2, :]
    hb_in = zero
    for s in reversed(range(SUBLANES)):
        hb_in = jnp.where(sub == s, hb, hb_in)
        hb = a_b[s:s + 1] * hb + b_b[s:s + 1]
    st_ref[0, 0:1, :] = hf
    st_ref[0, 1:2, :] = hb

    blocks = lambda ref: ref[...].reshape(n_blk, SUBLANES, LRU_COLS)
    h_sum = (blocks(af_scr) * hf_in + blocks(bf_scr)) + (blocks(ab_scr) * hb_in + blocks(bb_scr))
    mixed = (h_sum.reshape(t_len, LRU_COLS) * y).astype(BF16)
    acc_scr[...] += _dot(mixed, wout_ref[...])

    @pl.when(j == pl.num_programs(1) - 1)
    def _():
        def scatter(k, carry):
            r = pl.multiple_of(k * SUBLANES, SUBLANES)
            for c in range(n_slab):
                perm_scr[c, pl.ds(k, SUBLANES, stride=pitch), :] = acc_scr[pl.ds(r, SUBLANES), slab(c)]
            return carry

        lax.fori_loop(0, n_blk, scatter, 0, unroll=2)
        for s in range(SUBLANES):
            rows = slice(s * n_blk, (s + 1) * n_blk)
            mixed_s = jnp.concatenate(
                [perm_scr[c, s * pitch:s * pitch + n_blk, :] for c in range(n_slab)], axis=1)
            o_ref[0, rows, :] = _finish(x_ref[0, rows, :], mixed_s, mod_ref, lng_ref, lnb_ref)


def _lru_layer(x, mod, lng, lnb, w_in, conv_w, conv_b, wg, bg, lam, h0, w_out):
    b, t, d = x.shape
    cw = LRU_COLS
    ncb = D_RNN // cw
    bpc = cw // LRU_BW
    pitch = t // SUBLANES + SEG_PITCH_PAD
    seq = lambda w: pl.BlockSpec((1, t, w), lambda i, j: (i, 0, 0))
    col = lambda rows: pl.BlockSpec((rows, cw), lambda i, j: (0, j))
    out, state = pl.pallas_call(
        _lru_kernel,
        grid=(b, ncb),
        in_specs=[
            seq(d),
            _mod_spec(mod, 0, 2),
            _const_spec((1, d), 2),
            _const_spec((1, d), 2),
            pl.BlockSpec((d, cw), lambda i, j: (0, j)),
            pl.BlockSpec((d, cw), lambda i, j: (0, ncb + j)),
            col(CONV_W),
            col(1),
            pl.BlockSpec((bpc, LRU_BW, 4 * LRU_BW), lambda i, j: (j, 0, 0)),
            pl.BlockSpec((bpc, 1, 4 * LRU_BW), lambda i, j: (j, 0, 0)),
            col(2),
            pl.BlockSpec((1, 2, cw), lambda i, j: (i, 0, j)),
            pl.BlockSpec((cw, d), lambda i, j: (j, 0)),
        ],
        out_specs=[seq(d), pl.BlockSpec((1, 2, cw), lambda i, j: (i, 0, j))],
        out_shape=[jax.ShapeDtypeStruct(x.shape, x.dtype),
                   jax.ShapeDtypeStruct((b, 2, D_RNN), x.dtype)],
        scratch_shapes=[pltpu.VMEM((d // LANES, SUBLANES * pitch, LANES), F32),
                        pltpu.VMEM((t, d), BF16), pltpu.VMEM((t, d), F32),
                        pltpu.VMEM((t + (CONV_W - 1) * SUBLANES, cw), F32)]
                       + [pltpu.VMEM((t, cw), F32)] * 4,
        compiler_params=_params("arbitrary", "arbitrary"),
        name="lru_layer",
    )(x, mod, lng, lnb, w_in, w_in, conv_w, conv_b, wg, bg, lam, h0, w_out)
    return out, state


def _rope_rotate_cols(w):
    nf = ROPE_NF
    return jnp.concatenate([-w[..., nf:2 * nf], w[..., :nf], -w[..., 3 * nf:], w[..., 2 * nf:3 * nf]], axis=-1)


def _rope_tables(t_len):
    t = jnp.arange(t_len)
    rows = (t // GRID_W).astype(F32)
    cols = (t % GRID_W).astype(F32)
    inv = ROPE_THETA ** (-jnp.arange(ROPE_NF, dtype=F32) / ROPE_NF)
    ar, ac = rows[:, None] * inv, cols[:, None] * inv
    cos = jnp.concatenate([jnp.cos(ar), jnp.cos(ar), jnp.cos(ac), jnp.cos(ac)], axis=1)
    sin = jnp.concatenate([jnp.sin(ar), jnp.sin(ar), jnp.sin(ac), jnp.sin(ac)], axis=1)
    return jnp.tile(cos, (1, 2)), jnp.tile(sin, (1, 2))


def _parity_slots(w_rope):
    z = jnp.zeros_like(w_rope)
    even = (jnp.arange(MLA_HEADS) % 2 == 0)[None, :, None]
    return jnp.concatenate([jnp.where(even, w_rope, z), jnp.where(even, z, w_rope)], axis=-1)


def _mla_weights(w_dq, w_uq, w_dkv, w_uk, w_uv, w_o, rope):
    r = w_uq.shape[0]
    w_nope, w_rope = w_uq[:, :, :QK_NOPE], w_uq[:, :, QK_NOPE:]
    wq = jnp.concatenate([w_nope, _parity_slots(w_rope)], axis=-1).reshape(r, -1).astype(BF16)
    wk = w_dkv[:, KV_LORA:]
    wdkv = [w_dkv[:, :KV_LORA], wk, wk]
    wq_rot = None
    if rope:
        wq_rot = _parity_slots(_rope_rotate_cols(w_rope)).reshape(r, -1).astype(BF16)
        wk_rot = _rope_rotate_cols(wk)
        wdkv += [wk_rot, wk_rot]
    return (w_dq.astype(BF16), wq, jnp.concatenate(wdkv, axis=1).astype(BF16),
            w_uk.reshape(KV_LORA, -1).astype(BF16), w_uv.reshape(KV_LORA, -1).astype(BF16),
            w_o.astype(BF16), wq_rot)


def _lru_gate_weights(w_a, b_a, w_i, b_i):
    wg = jnp.concatenate([w_a[0], w_a[1], w_i[0], w_i[1]], axis=-1).astype(BF16)
    blk = lambda v: v.reshape(LRU_BLOCKS, 1, LRU_BW)
    bg = jnp.concatenate([blk(b_a[0]), blk(b_a[1]), blk(b_i[0]), blk(b_i[1])], axis=-1)
    return wg, bg


def kernel(x_prompt, x_sample, cache_mla_ckv, cache_mla_krope, state_lru, c, c_ctx, w_ada, b_ada, ln_g, ln_b, w_ffn_in, w_ffn_out, w_pool, pool_scale, w_dq, g_q, w_uq, w_dkv, g_kv, w_uk, w_uv, w_mla_o, w_lru_in, lru_conv_w, lru_conv_b, w_lru_a, b_lru_a, w_lru_i, b_lru_i, lru_lambda, w_lru_out):
    d = D_MODEL
    n_lat = c.shape[0]
    cond = jnp.concatenate([c_ctx[None], c, jnp.zeros((MOD_ROWS - 1 - n_lat, d), F32)], axis=0)
    mod = _modulation(cond, w_ada, b_ada)
    mods = (mod[:, :, 0:1, :][:, :, :, None, :],
            mod[:, :, 1:1 + n_lat, :][:, :, :, None, :])

    xs = [x_prompt, x_sample]
    ckv_out, krope_out, lru_out = [], [], []
    cos, sin = _rope_tables(x_sample.shape[1])
    win, wout = w_ffn_in.astype(BF16), w_ffn_out.astype(BF16)
    for i in range(DEPTH):
        kind, j = i % N_MIXERS, i // N_MIXERS
        row = lambda v: v.reshape(1, -1)
        lng0, lnb0, lng1, lnb1 = row(ln_g[i, 0]), row(ln_b[i, 0]), row(ln_g[i, 1]), row(ln_b[i, 1])
        if kind == 0:
            wp, ps = w_pool[j].astype(BF16), row(pool_scale[j])
        elif kind == 1:
            mla_w = [_mla_weights(w_dq[j], w_uq[j], w_dkv[j], w_uk[j], w_uv[j], w_mla_o[j], rope)
                     for rope in (False, True)]
            gq, gkv = row(g_q[j]), row(g_kv[j])
        else:
            w_in = w_lru_in[j].astype(BF16)
            wg, bg = _lru_gate_weights(w_lru_a[j], b_lru_a[j], w_lru_i[j], b_lru_i[j])
            w_out = w_lru_out[j].astype(BF16)
        for p in range(2):
            x, m = xs[p], mods[p][i]
            if kind == 0:
                x = _pool_layer(x, m, lng0, lnb0, wp, ps)
            elif kind == 1:
                wdq, wq, wdkv, wuk, wuv, wo, wq_rot = mla_w[p]
                if p == 0:
                    qcat, kcat, v, ckv, kr64 = _mla_project(
                        x, m, wdq, gq, wq, wdkv, gkv, wuk, wuv, None, True)
                    ckv_out.append(ckv)
                    krope_out.append(kr64)
                    segments = [(kcat, v)]
                else:
                    qcat, kcat, v = _mla_project(
                        x, m, wdq, gq, wq, wdkv, gkv, wuk, wuv, (wq_rot, cos, sin), False)
                    kr_c = cache_mla_krope[:, j]
                    kr_c = jnp.concatenate([kr_c, kr_c], axis=-1).astype(BF16)
                    kcat_c, v_c = _kv_up(cache_mla_ckv[:, j], kr_c, wuk, wuv)
                    segments = [(kcat_c, v_c), (kcat, v)]
                x = _attn_layer(x, m, lng0, lnb0, qcat, wo, segments)
            else:
                if p == 0:
                    h0 = jnp.zeros((x.shape[0], 2, D_RNN), F32)
                else:
                    h0 = state_lru[:, j]
                x, st = _lru_layer(x, m, lng0, lnb0, w_in, lru_conv_w[j], row(lru_conv_b[j]),
                                   wg, bg, lru_lambda[j], h0, w_out)
                if p == 0:
                    lru_out.append(st)
            xs[p] = _ffn_layer(x, m, lng1, lnb1, win, wout, i)
    return (xs[0], xs[1], jnp.stack(ckv_out, axis=1), jnp.stack(krope_out, axis=1),
            jnp.stack(lru_out, axis=1))
```

```python
import functools
import math
from typing import NamedTuple

import jax
import jax.numpy as jnp
from jax import lax
from jax.experimental import pallas as pl
from jax.experimental.pallas import tpu as pltpu

F32 = jnp.float32
BF16 = jnp.bfloat16

D_MODEL = 1024
DEPTH = 4
GRID_W = 64
N_MIXERS = 3
POOL_WINDOWS = (2, 4, 8, 16)
POOL_GROUPS = 4
POOL_GC = D_MODEL // POOL_GROUPS
MLA_HEADS = 8
Q_LORA = 384
KV_LORA = 256
QK_NOPE = 128
QK_ROPE = 64
V_HEAD = 128
ROPE_NF = QK_ROPE // 4
ROPE_THETA = 10000.0
MLA_SCALE = (QK_NOPE + QK_ROPE) ** -0.5
D_RNN = D_MODEL
LRU_BLOCKS = 8
LRU_BW = D_RNN // LRU_BLOCKS
CONV_W = 4
CONV_LEFT = 1
LRU_C = 8.0
D_FF = ((8 * D_MODEL // 3 + 255) // 256) * 256
ALPHA = (2.0 * DEPTH) ** 0.25
EPS = 1e-6

LANES = 128
SUBLANES = 8
VMEM_LIMIT_BYTES = 56 * 1024 * 1024
PAD_ROWS = SUBLANES
MOD_ROWS = 16
LRU_COLS = 512
SEG_PITCH_PAD = 4
TOKEN_TILE = 256
FFN_TILE = 512
ATTN_TILE = 512
QK_CAT = 2 * LANES


def _params(*sem):
    return pltpu.CompilerParams(dimension_semantics=sem, vmem_limit_bytes=VMEM_LIMIT_BYTES)


def _dot(a, b):
    return jnp.dot(a, b, preferred_element_type=F32)


def _layer_norm(y, g, b):
    mu = jnp.mean(y, axis=-1, keepdims=True)
    d = y - mu
    var = jnp.mean(d * d, axis=-1, keepdims=True)
    return d * lax.rsqrt(var + EPS) * g + b


def _gelu_tanh(x):
    c = math.sqrt(2.0 / math.pi)
    return (0.5 * x) * (1.0 + jnp.tanh(x * (c + (c * 0.044715) * (x * x))))


def _rms_norm(y, g):
    return y * lax.rsqrt(jnp.mean(y * y, axis=-1, keepdims=True) + EPS) * g


def _modulate(x, mod_ref):
    return x * (1.0 + mod_ref[1, 0]) + mod_ref[0, 0]


def _finish(x, mixed, mod_ref, lng_ref, lnb_ref):
    return _layer_norm(ALPHA * x + mod_ref[2, 0] * mixed, lng_ref[0], lnb_ref[0])


class _Cond(NamedTuple):
    mod: jax.Array
    lng: jax.Array
    lnb: jax.Array
    layer: int
    latent: bool

    @property
    def arrays(self):
        return self.mod, self.lng, self.lnb


def _cond_specs(cond, sub, nargs):
    k = 2 * cond.layer + sub
    row = (lambda b: 1 + b) if cond.latent else (lambda b: 0)
    if nargs == 1:
        mod = pl.BlockSpec((3, 1, 1, D_MODEL), lambda b: (k, row(b), 0, 0))
        ln = pl.BlockSpec((1, 1, D_MODEL), lambda b: (k, 0, 0))
    else:
        mod = pl.BlockSpec((3, 1, 1, D_MODEL), lambda b, t: (k, row(b), 0, 0))
        ln = pl.BlockSpec((1, 1, D_MODEL), lambda b, t: (k, 0, 0))
    return [mod, ln, ln]


def _const_spec(shape, nargs):
    zeros = (0,) * len(shape)
    if nargs == 1:
        return pl.BlockSpec(shape, lambda b: zeros)
    return pl.BlockSpec(shape, lambda b, t: zeros)


def _mod_kernel(cond_ref, w_ref, b_ref, o_ref):
    s = jax.nn.silu(cond_ref[...]).astype(BF16)
    o_ref[0, 0] = _dot(s, w_ref[0].astype(BF16)) + b_ref[0, 0]


def _modulation(cond, w_ada, b_ada):
    d = D_MODEL
    return pl.pallas_call(
        _mod_kernel,
        grid=(DEPTH, 6),
        in_specs=[
            pl.BlockSpec((MOD_ROWS, d), lambda i, j: (0, 0)),
            pl.BlockSpec((1, d, d), lambda i, j: (i, 0, j)),
            pl.BlockSpec((1, 1, 1, d), lambda i, j: (i, j, 0, 0)),
        ],
        out_specs=pl.BlockSpec((1, 1, MOD_ROWS, d), lambda i, j: (i, j, 0, 0)),
        out_shape=jax.ShapeDtypeStruct((DEPTH, 6, MOD_ROWS, d), F32),
        compiler_params=_params("arbitrary", "arbitrary"),
        name="adaln_modulation",
    )(cond, w_ada, b_ada.reshape(DEPTH, 6, 1, d))


def _ffn_kernel(x_ref, mod_ref, lng_ref, lnb_ref, win_ref, wout_ref, o_ref):
    x = x_ref[0]
    h = _modulate(x, mod_ref).astype(BF16)
    ab = _dot(h, win_ref[0])
    u = (jax.nn.silu(ab[:, :D_FF]) * ab[:, D_FF:]).astype(BF16)
    o_ref[0] = _finish(x, _dot(u, wout_ref[0]), mod_ref, lng_ref, lnb_ref)


def _ffn_layer(x, cond, win, wout):
    layer = cond.layer
    b, t, d = x.shape
    tm = min(FFN_TILE, t)
    slab = lambda r, c: pl.BlockSpec((1, r, c), lambda i, j: (layer, 0, 0), pipeline_mode=pl.Buffered(1))
    return pl.pallas_call(
        _ffn_kernel,
        grid=(b, t // tm),
        in_specs=[
            pl.BlockSpec((1, tm, d), lambda i, j: (i, j, 0)),
            *_cond_specs(cond, 1, 2),
            slab(d, 2 * D_FF),
            slab(D_FF, d),
        ],
        out_specs=pl.BlockSpec((1, tm, d), lambda i, j: (i, j, 0)),
        out_shape=jax.ShapeDtypeStruct(x.shape, x.dtype),
        compiler_params=_params("arbitrary", "arbitrary"),
        name="ffn_layer",
    )(x, *cond.arrays, win, wout)


def _pool_kernel(x_ref, mod_ref, lng_ref, lnb_ref, wp_ref, ps_ref, o_ref):
    t_len = x_ref.shape[1]
    n_pad = t_len + 2 * PAD_ROWS
    x = x_ref[0]
    h = _modulate(x, mod_ref)
    edge = jnp.zeros((PAD_ROWS, POOL_GC), F32)
    t = lax.broadcasted_iota(jnp.int32, (t_len, 1), 0)
    outs = []
    for gi, w in enumerate(POOL_WINDOWS):
        assert w & (w - 1) == 0 and w // 2 <= PAD_ROWS
        hg = h[:, gi * POOL_GC:(gi + 1) * POOL_GC]
        run = jnp.concatenate([edge, hg, edge], axis=0)
        span = 1
        while 2 * span < w:
            run = run + pltpu.roll(run, n_pad - span, axis=0)
            span *= 2
        s = (run + pltpu.roll(run, span, axis=0))[PAD_ROWS:PAD_ROWS + t_len]
        lo, hi = -(w // 2), w - w // 2
        cnt = (jnp.minimum(t + hi, t_len) - jnp.maximum(t + lo, 0)).astype(F32)
        pooled = s / cnt - hg
        outs.append(_dot(pooled.astype(BF16), wp_ref[gi]))
    mixed = jnp.concatenate(outs, axis=1) * ps_ref[...]
    o_ref[0] = _finish(x, mixed, mod_ref, lng_ref, lnb_ref)


def _pool_layer(x, cond, wp, ps):
    b, t, d = x.shape
    return pl.pallas_call(
        _pool_kernel,
        grid=(b,),
        in_specs=[
            pl.BlockSpec((1, t, d), lambda i: (i, 0, 0)),
            *_cond_specs(cond, 0, 1),
            _const_spec((POOL_GROUPS, POOL_GC, POOL_GC), 1),
            _const_spec((1, d), 1),
        ],
        out_specs=pl.BlockSpec((1, t, d), lambda i: (i, 0, 0)),
        out_shape=jax.ShapeDtypeStruct(x.shape, x.dtype),
        compiler_params=_params("arbitrary"),
        name="pool_layer",
    )(x, *cond.arrays, wp, ps)


def _store_k_cat(kcat_ref, k_nope, k_rope2):
    for hd in range(MLA_HEADS):
        kcat_ref[0, :, hd * QK_CAT:hd * QK_CAT + QK_NOPE] = k_nope[:, hd * QK_NOPE:(hd + 1) * QK_NOPE]
        kcat_ref[0, :, hd * QK_CAT + QK_NOPE:(hd + 1) * QK_CAT] = k_rope2


def _mla_proj_kernel(*refs, rope, emit_cache):
    x_ref, mod_ref, wdq_ref, gq_ref, wq_ref, wdkv_ref, gkv_ref, wuk_ref, wuv_ref = refs[:9]
    refs = refs[9:]
    if rope:
        wqrot_ref, cos_ref, sin_ref = refs[:3]
        refs = refs[3:]
    qcat_ref, kcat_ref, v_ref = refs[:3]
    h = _modulate(x_ref[0], mod_ref).astype(BF16)
    q_lat = _rms_norm(_dot(h, wdq_ref[...]), gq_ref[...]).astype(BF16)
    nope_w = MLA_HEADS * QK_NOPE
    q = _dot(q_lat, wq_ref[...])
    kv = _dot(h, wdkv_ref[...])
    k_rope2 = kv[:, KV_LORA:KV_LORA + 2 * QK_ROPE]
    if rope:
        cos, sin = cos_ref[...], sin_ref[...]
        q_rot = _dot(q_lat, wqrot_ref[...])
        k_rope2 = k_rope2 * cos + kv[:, KV_LORA + 2 * QK_ROPE:] * sin
    lane = lax.broadcasted_iota(jnp.int32, (1, LANES), 1)
    for pair in range(MLA_HEADS // 2):
        both = q[:, nope_w + pair * LANES:nope_w + (pair + 1) * LANES]
        if rope:
            both = both * cos + q_rot[:, pair * LANES:(pair + 1) * LANES] * sin
        for hd, keep in ((2 * pair, lane < QK_ROPE), (2 * pair + 1, lane >= QK_ROPE)):
            c0 = hd * QK_CAT
            qcat_ref[0, :, c0:c0 + QK_NOPE] = q[:, hd * QK_NOPE:(hd + 1) * QK_NOPE].astype(BF16)
            qcat_ref[0, :, c0 + QK_NOPE:c0 + QK_CAT] = jnp.where(keep, both, 0.0).astype(BF16)
    c_kv = _rms_norm(kv[:, :KV_LORA], gkv_ref[...])
    c_kv16 = c_kv.astype(BF16)
    _store_k_cat(kcat_ref, _dot(c_kv16, wuk_ref[...]).astype(BF16), k_rope2.astype(BF16))
    v_ref[0] = _dot(c_kv16, wuv_ref[...]).astype(BF16)
    if emit_cache:
        ckv_out_ref, kr_out_ref = refs[3:5]
        ckv_out_ref[0] = c_kv
        kr_out_ref[0] = kv[:, KV_LORA:KV_LORA + QK_ROPE]


def _mla_project(x, cond, wdq, gq, wq, wdkv, gkv, wuk, wuv, rope_args, emit_cache):
    b, t, d = x.shape
    tm = TOKEN_TILE
    rope = rope_args is not None
    tok = lambda w: pl.BlockSpec((1, tm, w), lambda i, j: (i, j, 0))
    args = [x, cond.mod, wdq, gq, wq, wdkv, gkv, wuk, wuv]
    in_specs = [tok(d), _cond_specs(cond, 0, 2)[0]] + [_const_spec(a.shape, 2) for a in args[2:]]
    if rope:
        wqrot, cos, sin = rope_args
        args += [wqrot, cos, sin]
        in_specs += [_const_spec(wqrot.shape, 2)] + [pl.BlockSpec((tm, LANES), lambda i, j: (j, 0))] * 2
    widths = [MLA_HEADS * QK_CAT, MLA_HEADS * QK_CAT, MLA_HEADS * V_HEAD]
    out_specs = [tok(w) for w in widths]
    out_shape = [jax.ShapeDtypeStruct((b, t, w), BF16) for w in widths]
    if emit_cache:
        out_specs += [tok(KV_LORA), tok(QK_ROPE)]
        out_shape += [jax.ShapeDtypeStruct((b, t, KV_LORA), F32),
                      jax.ShapeDtypeStruct((b, t, QK_ROPE), F32)]
    return pl.pallas_call(
        functools.partial(_mla_proj_kernel, rope=rope, emit_cache=emit_cache),
        grid=(b, t // tm),
        in_specs=in_specs,
        out_specs=out_specs,
        out_shape=out_shape,
        compiler_params=_params("arbitrary", "arbitrary"),
        name="mla_project",
    )(*args)


def _kv_up_kernel(ckv_ref, kr2_ref, wuk_ref, wuv_ref, kcat_ref, v_ref):
    c = ckv_ref[0].astype(BF16)
    _store_k_cat(kcat_ref, _dot(c, wuk_ref[...]).astype(BF16), kr2_ref[0])
    v_ref[0] = _dot(c, wuv_ref[...]).astype(BF16)


def _kv_up(ckv, kr2, wuk, wuv):
    b, s, r = ckv.shape
    seq = lambda w: pl.BlockSpec((1, s, w), lambda i: (i, 0, 0))
    widths = [MLA_HEADS * QK_CAT, MLA_HEADS * V_HEAD]
    return pl.pallas_call(
        _kv_up_kernel,
        grid=(b,),
        in_specs=[seq(r), seq(kr2.shape[2]), _const_spec(wuk.shape, 1), _const_spec(wuv.shape, 1)],
        out_specs=[seq(w) for w in widths],
        out_shape=[jax.ShapeDtypeStruct((b, s, w), BF16) for w in widths],
        compiler_params=_params("arbitrary"),
        name="mla_cache_kv_up",
    )(ckv, kr2, wuk, wuv)


def _attn_kernel(*refs, n_seg):
    x_ref, mod_ref, lng_ref, lnb_ref, q_ref, wo_ref = refs[:6]
    seg_refs = [refs[6 + 2 * i:8 + 2 * i] for i in range(n_seg)]
    o_ref = refs[6 + 2 * n_seg]
    nt = (((1,), (1,)), ((), ()))
    exp2_scale = MLA_SCALE * math.log2(math.e)
    heads = []
    for hd in range(MLA_HEADS):
        q = q_ref[0, :, hd * QK_CAT:(hd + 1) * QK_CAT]
        scores = [lax.dot_general(q, k_ref[0, :, hd * QK_CAT:(hd + 1) * QK_CAT], nt,
                                  preferred_element_type=F32) for k_ref, _ in seg_refs]
        m = functools.reduce(jnp.maximum, [jnp.max(s, axis=-1, keepdims=True) for s in scores])
        es = [jnp.exp2((s - m) * exp2_scale) for s in scores]
        total = functools.reduce(lambda a, b: a + b, [jnp.sum(e, axis=-1, keepdims=True) for e in es])
        o = None
        for e, (_, v_ref) in zip(es, seg_refs):
            part = _dot(e.astype(BF16), v_ref[0, :, hd * V_HEAD:(hd + 1) * V_HEAD])
            o = part if o is None else o + part
        heads.append((o * (1.0 / total)).astype(BF16))
    mixed = _dot(jnp.concatenate(heads, axis=1), wo_ref[...])
    o_ref[0] = _finish(x_ref[0], mixed, mod_ref, lng_ref, lnb_ref)


def _attn_layer(x, cond, qcat, wo, segments):
    b, t, d = x.shape
    tq = min(ATTN_TILE, t)
    tok = lambda w: pl.BlockSpec((1, tq, w), lambda i, j: (i, j, 0))
    in_specs = [tok(d), *_cond_specs(cond, 0, 2),
                tok(qcat.shape[2]), _const_spec(wo.shape, 2)]
    args = [x, *cond.arrays, qcat, wo]
    for seg in segments:
        for a in seg:
            in_specs.append(pl.BlockSpec((1,) + a.shape[1:], lambda i, j: (i, 0, 0)))
            args.append(a)
    return pl.pallas_call(
        functools.partial(_attn_kernel, n_seg=len(segments)),
        grid=(b, t // tq),
        in_specs=in_specs,
        out_specs=tok(d),
        out_shape=jax.ShapeDtypeStruct(x.shape, x.dtype),
        compiler_params=_params("arbitrary", "arbitrary"),
        name="mla_attention",
    )(*args)


def _lru_kernel(x_ref, mod_ref, lng_ref, lnb_ref, wu_ref, wy_ref, cw_ref, cb_ref, wg_ref, bg_ref,
                lam_ref, h0_ref, wout_ref, o_ref, st_ref,
                perm_scr, h_scr, acc_scr, pad_scr, af_scr, bf_scr, ab_scr, bb_scr):
    t_len = x_ref.shape[1]
    n_blk = t_len // SUBLANES
    pitch = n_blk + SEG_PITCH_PAD
    n_slab = D_MODEL // LANES
    j = pl.program_id(1)
    slab = lambda c: slice(c * LANES, (c + 1) * LANES)

    @pl.when(j == 0)
    def _():
        for c in range(n_slab):
            for s in range(SUBLANES):
                perm_scr[c, s * pitch:s * pitch + n_blk, :] = x_ref[0, s * n_blk:(s + 1) * n_blk, slab(c)]

        def gather(k, carry):
            r = pl.multiple_of(k * SUBLANES, SUBLANES)
            for c in range(n_slab):
                acc_scr[pl.ds(r, SUBLANES), slab(c)] = perm_scr[c, pl.ds(k, SUBLANES, stride=pitch), :]
            return carry

        lax.fori_loop(0, n_blk, gather, 0, unroll=2)
        h_scr[...] = _modulate(acc_scr[...], mod_ref).astype(BF16)
        acc_scr[...] = jnp.zeros_like(acc_scr)

    h = h_scr[...]
    u = _dot(h, wu_ref[...])
    y = _gelu_tanh(_dot(h, wy_ref[...]))

    sub = lax.broadcasted_iota(jnp.int32, (SUBLANES, LRU_COLS), 0)
    lead = CONV_LEFT * SUBLANES
    for i in range(1, CONV_LEFT + 1):
        blk = u[(n_blk - i) * SUBLANES:(n_blk - i + 1) * SUBLANES]
        pad_scr[lead - i * SUBLANES:lead - (i - 1) * SUBLANES, :] = jnp.where(
            sub >= 1, pltpu.roll(blk, 1, axis=0), 0.0)
    pad_scr[lead:lead + t_len, :] = u
    for i in range(CONV_W - 1 - CONV_LEFT):
        blk = u[i * SUBLANES:(i + 1) * SUBLANES]
        pad_scr[lead + t_len + i * SUBLANES:lead + t_len + (i + 1) * SUBLANES, :] = jnp.where(
            sub < SUBLANES - 1, pltpu.roll(blk, SUBLANES - 1, axis=0), 0.0)
    uc = None
    for k in range(CONV_W):
        term = pad_scr[k * SUBLANES:k * SUBLANES + t_len, :] * cw_ref[k:k + 1, :]
        uc = term if uc is None else uc + term
    uc = uc + cb_ref[...]

    a_scr = (af_scr, ab_scr)
    b_scr = (bf_scr, bb_scr)
    for bl in range(LRU_COLS // LRU_BW):
        c0 = bl * LRU_BW
        ub = uc[:, c0:c0 + LRU_BW]
        tg = jnp.tanh(_dot(ub.astype(BF16), wg_ref[bl]) + bg_ref[bl])
        for dr in range(2):
            tr = tg[:, dr * LRU_BW:(dr + 1) * LRU_BW]
            ti = tg[:, (2 + dr) * LRU_BW:(3 + dr) * LRU_BW]
            half_sp = (-0.25 * LRU_C) * jax.nn.softplus(-lam_ref[dr:dr + 1, c0:c0 + LRU_BW])
            t = jnp.tanh(half_sp * (1.0 + tr))
            inv = 1.0 / (1.0 - t)
            nt = -t
            root = jnp.where(nt > 0.0, nt * lax.rsqrt(nt), 0.0)
            a_scr[dr][:, c0:c0 + LRU_BW] = (1.0 + t) * inv
            b_scr[dr][:, c0:c0 + LRU_BW] = (root * inv) * ((1.0 + ti) * ub)

    def step(k, carry):
        a_f, b_f, a_b, b_b = carry
        rf = pl.multiple_of(k * SUBLANES, SUBLANES)
        rb = pl.multiple_of((n_blk - 1 - k) * SUBLANES, SUBLANES)
        a = af_scr[pl.ds(rf, SUBLANES), :]
        b_f = a * b_f + bf_scr[pl.ds(rf, SUBLANES), :]
        a_f = a * a_f
        af_scr[pl.ds(rf, SUBLANES), :] = a_f
        bf_scr[pl.ds(rf, SUBLANES), :] = b_f
        a = ab_scr[pl.ds(rb, SUBLANES), :]
        b_b = a * b_b + bb_scr[pl.ds(rb, SUBLANES), :]
        a_b = a * a_b
        ab_scr[pl.ds(rb, SUBLANES), :] = a_b
        bb_scr[pl.ds(rb, SUBLANES), :] = b_b
        return a_f, b_f, a_b, b_b

    one = jnp.ones((SUBLANES, LRU_COLS), F32)
    zero = jnp.zeros((SUBLANES, LRU_COLS), F32)
    a_f, b_f, a_b, b_b = lax.fori_loop(0, n_blk, step, (one, zero, one, zero), unroll=4)

    hf = h0_ref[0, 0:1, :]
    hf_in = zero
    for s in range(SUBLANES):
        hf_in = jnp.where(sub == s, hf, hf_in)
        hf = a_f[s:s + 1] * hf + b_f[s:s + 1]
    hb = h0_ref[0, 1:2, :]
    hb_in = zero
    for s in reversed(range(SUBLANES)):
        hb_in = jnp.where(sub == s, hb, hb_in)
        hb = a_b[s:s + 1] * hb + b_b[s:s + 1]
    st_ref[0, 0:1, :] = hf
    st_ref[0, 1:2, :] = hb

    blocks = lambda ref: ref[...].reshape(n_blk, SUBLANES, LRU_COLS)
    h_sum = (blocks(af_scr) * hf_in + blocks(bf_scr)) + (blocks(ab_scr) * hb_in + blocks(bb_scr))
    mixed = (h_sum.reshape(t_len, LRU_COLS) * y).astype(BF16)
    acc_scr[...] += _dot(mixed, wout_ref[...])

    @pl.when(j == pl.num_programs(1) - 1)
    def _():
        def scatter(k, carry):
            r = pl.multiple_of(k * SUBLANES, SUBLANES)
            for c in range(n_slab):
                perm_scr[c, pl.ds(k, SUBLANES, stride=pitch), :] = acc_scr[pl.ds(r, SUBLANES), slab(c)]
            return carry

        lax.fori_loop(0, n_blk, scatter, 0, unroll=2)
        for s in range(SUBLANES):
            rows = slice(s * n_blk, (s + 1) * n_blk)
            mixed_s = jnp.concatenate(
                [perm_scr[c, s * pitch:s * pitch + n_blk, :] for c in range(n_slab)], axis=1)
            o_ref[0, rows, :] = _finish(x_ref[0, rows, :], mixed_s, mod_ref, lng_ref, lnb_ref)


def _lru_layer(x, cond, w_in, conv_w, conv_b, wg, bg, lam, h0, w_out):
    b, t, d = x.shape
    cw = LRU_COLS
    ncb = D_RNN // cw
    bpc = cw // LRU_BW
    pitch = t // SUBLANES + SEG_PITCH_PAD
    seq = lambda w: pl.BlockSpec((1, t, w), lambda i, j: (i, 0, 0))
    col = lambda rows: pl.BlockSpec((rows, cw), lambda i, j: (0, j))
    out, state = pl.pallas_call(
        _lru_kernel,
        grid=(b, ncb),
        in_specs=[
            seq(d),
            *_cond_specs(cond, 0, 2),
            pl.BlockSpec((d, cw), lambda i, j: (0, j)),
            pl.BlockSpec((d, cw), lambda i, j: (0, ncb + j)),
            col(CONV_W),
            col(1),
            pl.BlockSpec((bpc, LRU_BW, 4 * LRU_BW), lambda i, j: (j, 0, 0)),
            pl.BlockSpec((bpc, 1, 4 * LRU_BW), lambda i, j: (j, 0, 0)),
            col(2),
            pl.BlockSpec((1, 2, cw), lambda i, j: (i, 0, j)),
            pl.BlockSpec((cw, d), lambda i, j: (j, 0)),
        ],
        out_specs=[seq(d), pl.BlockSpec((1, 2, cw), lambda i, j: (i, 0, j))],
        out_shape=[jax.ShapeDtypeStruct(x.shape, x.dtype),
                   jax.ShapeDtypeStruct((b, 2, D_RNN), x.dtype)],
        scratch_shapes=[pltpu.VMEM((d // LANES, SUBLANES * pitch, LANES), F32),
                        pltpu.VMEM((t, d), BF16), pltpu.VMEM((t, d), F32),
                        pltpu.VMEM((t + (CONV_W - 1) * SUBLANES, cw), F32)]
                       + [pltpu.VMEM((t, cw), F32)] * 4,
        compiler_params=_params("arbitrary", "arbitrary"),
        name="lru_layer",
    )(x, *cond.arrays, w_in, w_in, conv_w, conv_b, wg, bg, lam, h0, w_out)
    return out, state


def _rope_rotate_cols(w):
    nf = ROPE_NF
    return jnp.concatenate([-w[..., nf:2 * nf], w[..., :nf], -w[..., 3 * nf:], w[..., 2 * nf:3 * nf]], axis=-1)


def _rope_tables(t_len):
    t = jnp.arange(t_len)
    rows = (t // GRID_W).astype(F32)
    cols = (t % GRID_W).astype(F32)
    inv = ROPE_THETA ** (-jnp.arange(ROPE_NF, dtype=F32) / ROPE_NF)
    ar, ac = rows[:, None] * inv, cols[:, None] * inv
    cos = jnp.concatenate([jnp.cos(ar), jnp.cos(ar), jnp.cos(ac), jnp.cos(ac)], axis=1)
    sin = jnp.concatenate([jnp.sin(ar), jnp.sin(ar), jnp.sin(ac), jnp.sin(ac)], axis=1)
    return jnp.tile(cos, (1, 2)), jnp.tile(sin, (1, 2))


def _mla_weights(w_dq, w_uq, w_dkv, w_uk, w_uv, w_o, rope):
    r = w_uq.shape[0]
    w_nope, w_rope = w_uq[:, :, :QK_NOPE], w_uq[:, :, QK_NOPE:]
    wq = jnp.concatenate([w_nope.reshape(r, -1), w_rope.reshape(r, -1)], axis=1).astype(BF16)
    wk = w_dkv[:, KV_LORA:]
    wdkv = [w_dkv[:, :KV_LORA], wk, wk]
    wq_rot = None
    if rope:
        wq_rot = _rope_rotate_cols(w_rope).reshape(r, -1).astype(BF16)
        wk_rot = _rope_rotate_cols(wk)
        wdkv += [wk_rot, wk_rot]
    return (w_dq.astype(BF16), wq, jnp.concatenate(wdkv, axis=1).astype(BF16),
            w_uk.reshape(KV_LORA, -1).astype(BF16), w_uv.reshape(KV_LORA, -1).astype(BF16),
            w_o.astype(BF16), wq_rot)


def _lru_gate_weights(w_a, b_a, w_i, b_i):
    wg = (0.5 * jnp.concatenate([w_a[0], w_a[1], w_i[0], w_i[1]], axis=-1)).astype(BF16)
    blk = lambda v: v.reshape(LRU_BLOCKS, 1, LRU_BW)
    bg = 0.5 * jnp.concatenate([blk(b_a[0]), blk(b_a[1]), blk(b_i[0]), blk(b_i[1])], axis=-1)
    return wg, bg


def kernel(x_prompt, x_sample, cache_mla_ckv, cache_mla_krope, state_lru, c, c_ctx, w_ada, b_ada, ln_g, ln_b, w_ffn_in, w_ffn_out, w_pool, pool_scale, w_dq, g_q, w_uq, w_dkv, g_kv, w_uk, w_uv, w_mla_o, w_lru_in, lru_conv_w, lru_conv_b, w_lru_a, b_lru_a, w_lru_i, b_lru_i, lru_lambda, w_lru_out):
    d = D_MODEL
    n_lat = c.shape[0]
    cond = jnp.concatenate([c_ctx[None], c, jnp.zeros((MOD_ROWS - 1 - n_lat, d), F32)], axis=0)
    mod = _modulation(cond, w_ada, b_ada).reshape(DEPTH * 6, MOD_ROWS, 1, d)
    lng, lnb = ln_g.reshape(DEPTH * 2, 1, d), ln_b.reshape(DEPTH * 2, 1, d)

    xs = [x_prompt, x_sample]
    ckv_out, krope_out, lru_out = [], [], []
    cos, sin = _rope_tables(x_sample.shape[1])
    win, wout = w_ffn_in.astype(BF16), w_ffn_out.astype(BF16)
    for i in range(DEPTH):
        kind, j = i % N_MIXERS, i // N_MIXERS
        row = lambda v: v.reshape(1, -1)
        if kind == 0:
            wp, ps = w_pool[j].astype(BF16), row(pool_scale[j])
        elif kind == 1:
            mla_w = [_mla_weights(w_dq[j], w_uq[j], w_dkv[j], w_uk[j], w_uv[j], w_mla_o[j], rope)
                     for rope in (False, True)]
            gq, gkv = row(g_q[j]), row(g_kv[j])
        else:
            w_in = w_lru_in[j].astype(BF16)
            wg, bg = _lru_gate_weights(w_lru_a[j], b_lru_a[j], w_lru_i[j], b_lru_i[j])
            w_out = w_lru_out[j].astype(BF16)
        for p in range(2):
            x, m = xs[p], _Cond(mod, lng, lnb, i, p == 1)
            if kind == 0:
                x = _pool_layer(x, m, wp, ps)
            elif kind == 1:
                wdq, wq, wdkv, wuk, wuv, wo, wq_rot = mla_w[p]
                if p == 0:
                    qcat, kcat, v, ckv, kr64 = _mla_project(
                        x, m, wdq, gq, wq, wdkv, gkv, wuk, wuv, None, True)
                    ckv_out.append(ckv)
                    krope_out.append(kr64)
                    segments = [(kcat, v)]
                else:
                    qcat, kcat, v = _mla_project(
                        x, m, wdq, gq, wq, wdkv, gkv, wuk, wuv, (wq_rot, cos, sin), False)
                    kr_c = cache_mla_krope[:, j]
                    kr_c = jnp.concatenate([kr_c, kr_c], axis=-1).astype(BF16)
                    kcat_c, v_c = _kv_up(cache_mla_ckv[:, j], kr_c, wuk, wuv)
                    segments = [(kcat_c, v_c), (kcat, v)]
                x = _attn_layer(x, m, qcat, wo, segments)
            else:
                if p == 0:
                    h0 = jnp.zeros((x.shape[0], 2, D_RNN), F32)
                else:
                    h0 = state_lru[:, j]
                x, st = _lru_layer(x, m, w_in, lru_conv_w[j], row(lru_conv_b[j]),
                                   wg, bg, lru_lambda[j], h0, w_out)
                if p == 0:
                    lru_out.append(st)
            xs[p] = _ffn_layer(x, m, win, wout)
    return (xs[0], xs[1], jnp.stack(ckv_out, axis=1), jnp.stack(krope_out, axis=1),
            jnp.stack(lru_out, axis=1))
```

```python
import functools
import math
from typing import NamedTuple

import jax
import jax.numpy as jnp
from jax import lax
from jax.experimental import pallas as pl
from jax.experimental.pallas import tpu as pltpu

F32 = jnp.float32
BF16 = jnp.bfloat16

D_MODEL = 1024
DEPTH = 4
GRID_W = 64
N_MIXERS = 3
POOL_WINDOWS = (2, 4, 8, 16)
POOL_GROUPS = 4
POOL_GC = D_MODEL // POOL_GROUPS
MLA_HEADS = 8
Q_LORA = 384
KV_LORA = 256
QK_NOPE = 128
QK_ROPE = 64
V_HEAD = 128
ROPE_NF = QK_ROPE // 4
ROPE_THETA = 10000.0
MLA_SCALE = (QK_NOPE + QK_ROPE) ** -0.5
D_RNN = D_MODEL
LRU_BLOCKS = 8
LRU_BW = D_RNN // LRU_BLOCKS
CONV_W = 4
CONV_LEFT = 1
LRU_C = 8.0
D_FF = ((8 * D_MODEL // 3 + 255) // 256) * 256
ALPHA = (2.0 * DEPTH) ** 0.25
EPS = 1e-6

LANES = 128
SUBLANES = 8
VMEM_LIMIT_BYTES = 56 * 1024 * 1024
PAD_ROWS = SUBLANES
MOD_ROWS = 16
MOD_CHUNKS = 2
LRU_COLS = 512
SEG_PITCH_PAD = 4
TOKEN_TILE = 256
FFN_TILE = 512
ATTN_TILE = 512
QK_CAT = 2 * LANES


def _params(*sem):
    return pltpu.CompilerParams(dimension_semantics=sem, vmem_limit_bytes=VMEM_LIMIT_BYTES)


def _dot(a, b):
    return jnp.dot(a, b, preferred_element_type=F32)


def _layer_norm(y, g, b):
    mu = jnp.mean(y, axis=-1, keepdims=True)
    d = y - mu
    var = jnp.mean(d * d, axis=-1, keepdims=True)
    return d * lax.rsqrt(var + EPS) * g + b


def _gelu_tanh(x):
    c = math.sqrt(2.0 / math.pi)
    return (0.5 * x) * (1.0 + jnp.tanh(x * (c + (c * 0.044715) * (x * x))))


def _rms_norm(y, g):
    return y * lax.rsqrt(jnp.mean(y * y, axis=-1, keepdims=True) + EPS) * g


def _modulate(x, mod_ref):
    return x * (1.0 + mod_ref[1, 0]) + mod_ref[0, 0]


def _finish(x, mixed, mod_ref, lng_ref, lnb_ref):
    return _layer_norm(ALPHA * x + mod_ref[2, 0] * mixed, lng_ref[0], lnb_ref[0])


class _Cond(NamedTuple):
    mod: jax.Array
    lng: jax.Array
    lnb: jax.Array
    layer: int
    latent: bool

    @property
    def arrays(self):
        return self.mod, self.lng, self.lnb


def _cond_specs(cond, sub, nargs):
    k = 2 * cond.layer + sub
    row = (lambda b: 1 + b) if cond.latent else (lambda b: 0)
    if nargs == 1:
        mod = pl.BlockSpec((3, 1, 1, D_MODEL), lambda b: (k, row(b), 0, 0))
        ln = pl.BlockSpec((1, 1, D_MODEL), lambda b: (k, 0, 0))
    else:
        mod = pl.BlockSpec((3, 1, 1, D_MODEL), lambda b, t: (k, row(b), 0, 0))
        ln = pl.BlockSpec((1, 1, D_MODEL), lambda b, t: (k, 0, 0))
    return [mod, ln, ln]


def _const_spec(shape, nargs):
    zeros = (0,) * len(shape)
    if nargs == 1:
        return pl.BlockSpec(shape, lambda b: zeros)
    return pl.BlockSpec(shape, lambda b, t: zeros)


def _mod_kernel(cond_ref, w_ref, b_ref, o_ref):
    s = jax.nn.silu(cond_ref[...]).astype(BF16)
    for k in range(o_ref.shape[1]):
        cols = slice(k * D_MODEL, (k + 1) * D_MODEL)
        o_ref[0, k] = _dot(s, w_ref[0, :, cols].astype(BF16)) + b_ref[0, k]


def _modulation(cond, w_ada, b_ada):
    d = D_MODEL
    nk = MOD_CHUNKS
    return pl.pallas_call(
        _mod_kernel,
        grid=(DEPTH, 6 // nk),
        in_specs=[
            pl.BlockSpec((MOD_ROWS, d), lambda i, j: (0, 0)),
            pl.BlockSpec((1, d, nk * d), lambda i, j: (i, 0, j)),
            pl.BlockSpec((1, nk, 1, d), lambda i, j: (i, j, 0, 0)),
        ],
        out_specs=pl.BlockSpec((1, nk, MOD_ROWS, d), lambda i, j: (i, j, 0, 0)),
        out_shape=jax.ShapeDtypeStruct((DEPTH, 6, MOD_ROWS, d), F32),
        compiler_params=_params("arbitrary", "arbitrary"),
        name="adaln_modulation",
    )(cond, w_ada, b_ada.reshape(DEPTH, 6, 1, d))


def _ffn_kernel(x_ref, mod_ref, lng_ref, lnb_ref, win_ref, wout_ref, o_ref):
    x = x_ref[0]
    h = _modulate(x, mod_ref).astype(BF16)
    ab = _dot(h, win_ref[0])
    u = (jax.nn.silu(ab[:, :D_FF]) * ab[:, D_FF:]).astype(BF16)
    o_ref[0] = _finish(x, _dot(u, wout_ref[0]), mod_ref, lng_ref, lnb_ref)


def _ffn_layer(x, cond, win, wout):
    layer = cond.layer
    shape = x.shape
    d = shape[-1]
    if not cond.latent and shape[1] < FFN_TILE:
        x = x.reshape(-1, FFN_TILE, d)
    b, t, _ = x.shape
    tm = min(FFN_TILE, t)
    slab = lambda r, c: pl.BlockSpec((1, r, c), lambda i, j: (layer, 0, 0), pipeline_mode=pl.Buffered(1))
    return pl.pallas_call(
        _ffn_kernel,
        grid=(b, t // tm),
        in_specs=[
            pl.BlockSpec((1, tm, d), lambda i, j: (i, j, 0)),
            *_cond_specs(cond, 1, 2),
            slab(d, 2 * D_FF),
            slab(D_FF, d),
        ],
        out_specs=pl.BlockSpec((1, tm, d), lambda i, j: (i, j, 0)),
        out_shape=jax.ShapeDtypeStruct(x.shape, x.dtype),
        compiler_params=_params("arbitrary", "arbitrary"),
        name="ffn_layer",
    )(x, *cond.arrays, win, wout).reshape(shape)


def _pool_kernel(x_ref, mod_ref, lng_ref, lnb_ref, wp_ref, ps_ref, o_ref):
    t_len = x_ref.shape[1]
    n_pad = t_len + 2 * PAD_ROWS
    x = x_ref[0]
    h = _modulate(x, mod_ref)
    edge = jnp.zeros((PAD_ROWS, POOL_GC), F32)
    t = lax.broadcasted_iota(jnp.int32, (t_len, 1), 0)
    outs = []
    for gi, w in enumerate(POOL_WINDOWS):
        assert w & (w - 1) == 0 and w // 2 <= PAD_ROWS
        hg = h[:, gi * POOL_GC:(gi + 1) * POOL_GC]
        run = jnp.concatenate([edge, hg, edge], axis=0)
        span = 1
        while 2 * span < w:
            run = run + pltpu.roll(run, n_pad - span, axis=0)
            span *= 2
        s = (run + pltpu.roll(run, span, axis=0))[PAD_ROWS:PAD_ROWS + t_len]
        lo, hi = -(w // 2), w - w // 2
        cnt = (jnp.minimum(t + hi, t_len) - jnp.maximum(t + lo, 0)).astype(F32)
        pooled = s / cnt - hg
        outs.append(_dot(pooled.astype(BF16), wp_ref[gi]))
    mixed = jnp.concatenate(outs, axis=1) * ps_ref[...]
    o_ref[0] = _finish(x, mixed, mod_ref, lng_ref, lnb_ref)


def _pool_layer(x, cond, wp, ps):
    b, t, d = x.shape
    return pl.pallas_call(
        _pool_kernel,
        grid=(b,),
        in_specs=[
            pl.BlockSpec((1, t, d), lambda i: (i, 0, 0)),
            *_cond_specs(cond, 0, 1),
            _const_spec((POOL_GROUPS, POOL_GC, POOL_GC), 1),
            _const_spec((1, d), 1),
        ],
        out_specs=pl.BlockSpec((1, t, d), lambda i: (i, 0, 0)),
        out_shape=jax.ShapeDtypeStruct(x.shape, x.dtype),
        compiler_params=_params("arbitrary"),
        name="pool_layer",
    )(x, *cond.arrays, wp, ps)


def _store_k_cat(kcat_ref, k_nope, k_rope2):
    for hd in range(MLA_HEADS):
        kcat_ref[0, :, hd * QK_CAT:hd * QK_CAT + QK_NOPE] = k_nope[:, hd * QK_NOPE:(hd + 1) * QK_NOPE]
        kcat_ref[0, :, hd * QK_CAT + QK_NOPE:(hd + 1) * QK_CAT] = k_rope2


def _mla_proj_kernel(*refs, rope, emit_cache):
    x_ref, mod_ref, wdq_ref, gq_ref, wq_ref, wdkv_ref, gkv_ref, wuk_ref, wuv_ref = refs[:9]
    refs = refs[9:]
    if rope:
        wqrot_ref, cos_ref, sin_ref = refs[:3]
        refs = refs[3:]
    qcat_ref, kcat_ref, v_ref = refs[:3]
    h = _modulate(x_ref[0], mod_ref).astype(BF16)
    q_lat = _rms_norm(_dot(h, wdq_ref[...]), gq_ref[...]).astype(BF16)
    nope_w = MLA_HEADS * QK_NOPE
    q = _dot(q_lat, wq_ref[...])
    kv = _dot(h, wdkv_ref[...])
    k_rope2 = kv[:, KV_LORA:KV_LORA + 2 * QK_ROPE]
    if rope:
        cos, sin = cos_ref[...], sin_ref[...]
        q_rot = _dot(q_lat, wqrot_ref[...])
        k_rope2 = k_rope2 * cos + kv[:, KV_LORA + 2 * QK_ROPE:] * sin
    lane = lax.broadcasted_iota(jnp.int32, (1, LANES), 1)
    for pair in range(MLA_HEADS // 2):
        both = q[:, nope_w + pair * LANES:nope_w + (pair + 1) * LANES]
        if rope:
            both = both * cos + q_rot[:, pair * LANES:(pair + 1) * LANES] * sin
        for hd, keep in ((2 * pair, lane < QK_ROPE), (2 * pair + 1, lane >= QK_ROPE)):
            c0 = hd * QK_CAT
            qcat_ref[0, :, c0:c0 + QK_NOPE] = q[:, hd * QK_NOPE:(hd + 1) * QK_NOPE].astype(BF16)
            qcat_ref[0, :, c0 + QK_NOPE:c0 + QK_CAT] = jnp.where(keep, both, 0.0).astype(BF16)
    c_kv = _rms_norm(kv[:, :KV_LORA], gkv_ref[...])
    c_kv16 = c_kv.astype(BF16)
    _store_k_cat(kcat_ref, _dot(c_kv16, wuk_ref[...]).astype(BF16), k_rope2.astype(BF16))
    v_ref[0] = _dot(c_kv16, wuv_ref[...]).astype(BF16)
    if emit_cache:
        ckv_out_ref, kr_out_ref = refs[3:5]
        ckv_out_ref[0] = c_kv
        kr_out_ref[0] = kv[:, KV_LORA:KV_LORA + QK_ROPE]


def _mla_project(x, cond, wdq, gq, wq, wdkv, gkv, wuk, wuv, rope_args, emit_cache):
    b, t, d = x.shape
    tm = TOKEN_TILE
    rope = rope_args is not None
    tok = lambda w: pl.BlockSpec((1, tm, w), lambda i, j: (i, j, 0))
    args = [x, cond.mod, wdq, gq, wq, wdkv, gkv, wuk, wuv]
    in_specs = [tok(d), _cond_specs(cond, 0, 2)[0]] + [_const_spec(a.shape, 2) for a in args[2:]]
    if rope:
        wqrot, cos, sin = rope_args
        args += [wqrot, cos, sin]
        in_specs += [_const_spec(wqrot.shape, 2)] + [pl.BlockSpec((tm, LANES), lambda i, j: (j, 0))] * 2
    widths = [MLA_HEADS * QK_CAT, MLA_HEADS * QK_CAT, MLA_HEADS * V_HEAD]
    out_specs = [tok(w) for w in widths]
    out_shape = [jax.ShapeDtypeStruct((b, t, w), BF16) for w in widths]
    if emit_cache:
        out_specs += [tok(KV_LORA), tok(QK_ROPE)]
        out_shape += [jax.ShapeDtypeStruct((b, t, KV_LORA), F32),
                      jax.ShapeDtypeStruct((b, t, QK_ROPE), F32)]
    return pl.pallas_call(
        functools.partial(_mla_proj_kernel, rope=rope, emit_cache=emit_cache),
        grid=(b, t // tm),
        in_specs=in_specs,
        out_specs=out_specs,
        out_shape=out_shape,
        compiler_params=_params("arbitrary", "arbitrary"),
        name="mla_project",
    )(*args)


def _kv_up_kernel(ckv_ref, kr2_ref, wuk_ref, wuv_ref, kcat_ref, v_ref):
    c = ckv_ref[0].astype(BF16)
    _store_k_cat(kcat_ref, _dot(c, wuk_ref[...]).astype(BF16), kr2_ref[0])
    v_ref[0] = _dot(c, wuv_ref[...]).astype(BF16)


def _kv_up(ckv, kr2, wuk, wuv):
    b, s, r = ckv.shape
    seq = lambda w: pl.BlockSpec((1, s, w), lambda i: (i, 0, 0))
    widths = [MLA_HEADS * QK_CAT, MLA_HEADS * V_HEAD]
    return pl.pallas_call(
        _kv_up_kernel,
        grid=(b,),
        in_specs=[seq(r), seq(kr2.shape[2]), _const_spec(wuk.shape, 1), _const_spec(wuv.shape, 1)],
        out_specs=[seq(w) for w in widths],
        out_shape=[jax.ShapeDtypeStruct((b, s, w), BF16) for w in widths],
        compiler_params=_params("arbitrary"),
        name="mla_cache_kv_up",
    )(ckv, kr2, wuk, wuv)


def _attn_kernel(*refs, n_seg):
    x_ref, mod_ref, lng_ref, lnb_ref, q_ref, wo_ref = refs[:6]
    seg_refs = [refs[6 + 2 * i:8 + 2 * i] for i in range(n_seg)]
    o_ref = refs[6 + 2 * n_seg]
    nt = (((1,), (1,)), ((), ()))
    exp2_scale = MLA_SCALE * math.log2(math.e)
    heads = []
    for hd in range(MLA_HEADS):
        q = q_ref[0, :, hd * QK_CAT:(hd + 1) * QK_CAT]
        scores = [lax.dot_general(q, k_ref[0, :, hd * QK_CAT:(hd + 1) * QK_CAT], nt,
                                  preferred_element_type=F32) for k_ref, _ in seg_refs]
        m = functools.reduce(jnp.maximum, [jnp.max(s, axis=-1, keepdims=True) for s in scores])
        es = [jnp.exp2((s - m) * exp2_scale) for s in scores]
        total = functools.reduce(lambda a, b: a + b, [jnp.sum(e, axis=-1, keepdims=True) for e in es])
        o = None
        for e, (_, v_ref) in zip(es, seg_refs):
            part = _dot(e.astype(BF16), v_ref[0, :, hd * V_HEAD:(hd + 1) * V_HEAD])
            o = part if o is None else o + part
        heads.append((o * (1.0 / total)).astype(BF16))
    mixed = _dot(jnp.concatenate(heads, axis=1), wo_ref[...])
    o_ref[0] = _finish(x_ref[0], mixed, mod_ref, lng_ref, lnb_ref)


def _attn_layer(x, cond, qcat, wo, segments):
    b, t, d = x.shape
    tq = min(ATTN_TILE, t)
    tok = lambda w: pl.BlockSpec((1, tq, w), lambda i, j: (i, j, 0))
    in_specs = [tok(d), *_cond_specs(cond, 0, 2),
                tok(qcat.shape[2]), _const_spec(wo.shape, 2)]
    args = [x, *cond.arrays, qcat, wo]
    for seg in segments:
        for a in seg:
            in_specs.append(pl.BlockSpec((1,) + a.shape[1:], lambda i, j: (i, 0, 0)))
            args.append(a)
    return pl.pallas_call(
        functools.partial(_attn_kernel, n_seg=len(segments)),
        grid=(b, t // tq),
        in_specs=in_specs,
        out_specs=tok(d),
        out_shape=jax.ShapeDtypeStruct(x.shape, x.dtype),
        compiler_params=_params("arbitrary", "arbitrary"),
        name="mla_attention",
    )(*args)


def _lru_kernel(x_ref, mod_ref, lng_ref, lnb_ref, wu_ref, wy_ref, cw_ref, cb_ref, wg_ref, bg_ref,
                lam_ref, h0_ref, wout_ref, o_ref, st_ref,
                perm_scr, h_scr, acc_scr, pad_scr, af_scr, bf_scr, ab_scr, bb_scr):
    t_len = x_ref.shape[1]
    n_blk = t_len // SUBLANES
    pitch = n_blk + SEG_PITCH_PAD
    n_slab = D_MODEL // LANES
    j = pl.program_id(1)
    slab = lambda c: slice(c * LANES, (c + 1) * LANES)

    @pl.when(j == 0)
    def _():
        for c in range(n_slab):
            for s in range(SUBLANES):
                perm_scr[c, s * pitch:s * pitch + n_blk, :] = x_ref[0, s * n_blk:(s + 1) * n_blk, slab(c)]

        def gather(k, carry):
            r = pl.multiple_of(k * SUBLANES, SUBLANES)
            for c in range(n_slab):
                acc_scr[pl.ds(r, SUBLANES), slab(c)] = perm_scr[c, pl.ds(k, SUBLANES, stride=pitch), :]
            return carry

        lax.fori_loop(0, n_blk, gather, 0, unroll=2)
        h_scr[...] = _modulate(acc_scr[...], mod_ref).astype(BF16)
        acc_scr[...] = jnp.zeros_like(acc_scr)

    h = h_scr[...]
    u = _dot(h, wu_ref[...])
    y = _gelu_tanh(_dot(h, wy_ref[...]))

    sub = lax.broadcasted_iota(jnp.int32, (SUBLANES, LRU_COLS), 0)
    lead = CONV_LEFT * SUBLANES
    for i in range(1, CONV_LEFT + 1):
        blk = u[(n_blk - i) * SUBLANES:(n_blk - i + 1) * SUBLANES]
        pad_scr[lead - i * SUBLANES:lead - (i - 1) * SUBLANES, :] = jnp.where(
            sub >= 1, pltpu.roll(blk, 1, axis=0), 0.0)
    pad_scr[lead:lead + t_len, :] = u
    for i in range(CONV_W - 1 - CONV_LEFT):
        blk = u[i * SUBLANES:(i + 1) * SUBLANES]
        pad_scr[lead + t_len + i * SUBLANES:lead + t_len + (i + 1) * SUBLANES, :] = jnp.where(
            sub < SUBLANES - 1, pltpu.roll(blk, SUBLANES - 1, axis=0), 0.0)
    uc = None
    for k in range(CONV_W):
        term = pad_scr[k * SUBLANES:k * SUBLANES + t_len, :] * cw_ref[k:k + 1, :]
        uc = term if uc is None else uc + term
    uc = uc + cb_ref[...]

    a_scr = (af_scr, ab_scr)
    b_scr = (bf_scr, bb_scr)
    for bl in range(LRU_COLS // LRU_BW):
        c0 = bl * LRU_BW
        ub = uc[:, c0:c0 + LRU_BW]
        tg = jnp.tanh(_dot(ub.astype(BF16), wg_ref[bl]) + bg_ref[bl])
        for dr in range(2):
            tr = tg[:, dr * LRU_BW:(dr + 1) * LRU_BW]
            ti = tg[:, (2 + dr) * LRU_BW:(3 + dr) * LRU_BW]
            half_sp = (-0.25 * LRU_C) * jax.nn.softplus(-lam_ref[dr:dr + 1, c0:c0 + LRU_BW])
            t = jnp.tanh(half_sp * (1.0 + tr))
            inv = 1.0 / (1.0 - t)
            nt = -t
            root = jnp.where(nt > 0.0, nt * lax.rsqrt(nt), 0.0)
            a_scr[dr][:, c0:c0 + LRU_BW] = (1.0 + t) * inv
            b_scr[dr][:, c0:c0 + LRU_BW] = (root * inv) * ((1.0 + ti) * ub)

    def step(k, carry):
        a_f, b_f, a_b, b_b = carry
        rf = pl.multiple_of(k * SUBLANES, SUBLANES)
        rb = pl.multiple_of((n_blk - 1 - k) * SUBLANES, SUBLANES)
        a = af_scr[pl.ds(rf, SUBLANES), :]
        b_f = a * b_f + bf_scr[pl.ds(rf, SUBLANES), :]
        a_f = a * a_f
        af_scr[pl.ds(rf, SUBLANES), :] = a_f
        bf_scr[pl.ds(rf, SUBLANES), :] = b_f
        a = ab_scr[pl.ds(rb, SUBLANES), :]
        b_b = a * b_b + bb_scr[pl.ds(rb, SUBLANES), :]
        a_b = a * a_b
        ab_scr[pl.ds(rb, SUBLANES), :] = a_b
        bb_scr[pl.ds(rb, SUBLANES), :] = b_b
        return a_f, b_f, a_b, b_b

    one = jnp.ones((SUBLANES, LRU_COLS), F32)
    zero = jnp.zeros((SUBLANES, LRU_COLS), F32)
    a_f, b_f, a_b, b_b = lax.fori_loop(0, n_blk, step, (one, zero, one, zero), unroll=4)

    hf = h0_ref[0, 0:1, :]
    hf_in = zero
    for s in range(SUBLANES):
        hf_in = jnp.where(sub == s, hf, hf_in)
        hf = a_f[s:s + 1] * hf + b_f[s:s + 1]
    hb = h0_ref[0, 1:2, :]
    hb_in = zero
    for s in reversed(range(SUBLANES)):
        hb_in = jnp.where(sub == s, hb, hb_in)
        hb = a_b[s:s + 1] * hb + b_b[s:s + 1]
    st_ref[0, 0:1, :] = hf
    st_ref[0, 1:2, :] = hb

    blocks = lambda ref: ref[...].reshape(n_blk, SUBLANES, LRU_COLS)
    h_sum = (blocks(af_scr) * hf_in + blocks(bf_scr)) + (blocks(ab_scr) * hb_in + blocks(bb_scr))
    mixed = (h_sum.reshape(t_len, LRU_COLS) * y).astype(BF16)
    acc_scr[...] += _dot(mixed, wout_ref[...])

    @pl.when(j == pl.num_programs(1) - 1)
    def _():
        def scatter(k, carry):
            r = pl.multiple_of(k * SUBLANES, SUBLANES)
            for c in range(n_slab):
                perm_scr[c, pl.ds(k, SUBLANES, stride=pitch), :] = acc_scr[pl.ds(r, SUBLANES), slab(c)]
            return carry

        lax.fori_loop(0, n_blk, scatter, 0, unroll=2)
        for s in range(SUBLANES):
            rows = slice(s * n_blk, (s + 1) * n_blk)
            mixed_s = jnp.concatenate(
                [perm_scr[c, s * pitch:s * pitch + n_blk, :] for c in range(n_slab)], axis=1)
            o_ref[0, rows, :] = _finish(x_ref[0, rows, :], mixed_s, mod_ref, lng_ref, lnb_ref)


def _lru_layer(x, cond, w_in, conv_w, conv_b, wg, bg, lam, h0, w_out):
    b, t, d = x.shape
    cw = LRU_COLS
    ncb = D_RNN // cw
    bpc = cw // LRU_BW
    pitch = t // SUBLANES + SEG_PITCH_PAD
    seq = lambda w: pl.BlockSpec((1, t, w), lambda i, j: (i, 0, 0))
    col = lambda rows: pl.BlockSpec((rows, cw), lambda i, j: (0, j))
    out, state = pl.pallas_call(
        _lru_kernel,
        grid=(b, ncb),
        in_specs=[
            seq(d),
            *_cond_specs(cond, 0, 2),
            pl.BlockSpec((d, cw), lambda i, j: (0, j)),
            pl.BlockSpec((d, cw), lambda i, j: (0, ncb + j)),
            col(CONV_W),
            col(1),
            pl.BlockSpec((bpc, LRU_BW, 4 * LRU_BW), lambda i, j: (j, 0, 0)),
            pl.BlockSpec((bpc, 1, 4 * LRU_BW), lambda i, j: (j, 0, 0)),
            col(2),
            pl.BlockSpec((1, 2, cw), lambda i, j: (i, 0, j)),
            pl.BlockSpec((cw, d), lambda i, j: (j, 0)),
        ],
        out_specs=[seq(d), pl.BlockSpec((1, 2, cw), lambda i, j: (i, 0, j))],
        out_shape=[jax.ShapeDtypeStruct(x.shape, x.dtype),
                   jax.ShapeDtypeStruct((b, 2, D_RNN), x.dtype)],
        scratch_shapes=[pltpu.VMEM((d // LANES, SUBLANES * pitch, LANES), F32),
                        pltpu.VMEM((t, d), BF16), pltpu.VMEM((t, d), F32),
                        pltpu.VMEM((t + (CONV_W - 1) * SUBLANES, cw), F32)]
                       + [pltpu.VMEM((t, cw), F32)] * 4,
        compiler_params=_params("arbitrary", "arbitrary"),
        name="lru_layer",
    )(x, *cond.arrays, w_in, w_in, conv_w, conv_b, wg, bg, lam, h0, w_out)
    return out, state


def _rope_rotate_cols(w):
    nf = ROPE_NF
    return jnp.concatenate([-w[..., nf:2 * nf], w[..., :nf], -w[..., 3 * nf:], w[..., 2 * nf:3 * nf]], axis=-1)


def _rope_tables(t_len):
    t = jnp.arange(t_len)
    rows = (t // GRID_W).astype(F32)
    cols = (t % GRID_W).astype(F32)
    inv = ROPE_THETA ** (-jnp.arange(ROPE_NF, dtype=F32) / ROPE_NF)
    ar, ac = rows[:, None] * inv, cols[:, None] * inv
    cos = jnp.concatenate([jnp.cos(ar), jnp.cos(ar), jnp.cos(ac), jnp.cos(ac)], axis=1)
    sin = jnp.concatenate([jnp.sin(ar), jnp.sin(ar), jnp.sin(ac), jnp.sin(ac)], axis=1)
    return jnp.tile(cos, (1, 2)), jnp.tile(sin, (1, 2))


def _mla_weights(w_dq, w_uq, w_dkv, w_uk, w_uv, w_o, rope):
    r = w_uq.shape[0]
    w_nope, w_rope = w_uq[:, :, :QK_NOPE], w_uq[:, :, QK_NOPE:]
    wq = jnp.concatenate([w_nope.reshape(r, -1), w_rope.reshape(r, -1)], axis=1).astype(BF16)
    wk = w_dkv[:, KV_LORA:]
    wdkv = [w_dkv[:, :KV_LORA], wk, wk]
    wq_rot = None
    if rope:
        wq_rot = _rope_rotate_cols(w_rope).reshape(r, -1).astype(BF16)
        wk_rot = _rope_rotate_cols(wk)
        wdkv += [wk_rot, wk_rot]
    return (w_dq.astype(BF16), wq, jnp.concatenate(wdkv, axis=1).astype(BF16),
            w_uk.reshape(KV_LORA, -1).astype(BF16), w_uv.reshape(KV_LORA, -1).astype(BF16),
            w_o.astype(BF16), wq_rot)


def _lru_gate_weights(w_a, b_a, w_i, b_i):
    wg = (0.5 * jnp.concatenate([w_a[0], w_a[1], w_i[0], w_i[1]], axis=-1)).astype(BF16)
    blk = lambda v: v.reshape(LRU_BLOCKS, 1, LRU_BW)
    bg = 0.5 * jnp.concatenate([blk(b_a[0]), blk(b_a[1]), blk(b_i[0]), blk(b_i[1])], axis=-1)
    return wg, bg


def kernel(x_prompt, x_sample, cache_mla_ckv, cache_mla_krope, state_lru, c, c_ctx, w_ada, b_ada, ln_g, ln_b, w_ffn_in, w_ffn_out, w_pool, pool_scale, w_dq, g_q, w_uq, w_dkv, g_kv, w_uk, w_uv, w_mla_o, w_lru_in, lru_conv_w, lru_conv_b, w_lru_a, b_lru_a, w_lru_i, b_lru_i, lru_lambda, w_lru_out):
    d = D_MODEL
    n_lat = c.shape[0]
    cond = jnp.concatenate([c_ctx[None], c, jnp.zeros((MOD_ROWS - 1 - n_lat, d), F32)], axis=0)
    mod = _modulation(cond, w_ada, b_ada).reshape(DEPTH * 6, MOD_ROWS, 1, d)
    lng, lnb = ln_g.reshape(DEPTH * 2, 1, d), ln_b.reshape(DEPTH * 2, 1, d)

    xs = [x_prompt, x_sample]
    ckv_out, krope_out, lru_out = [], [], []
    cos, sin = _rope_tables(x_sample.shape[1])
    win, wout = w_ffn_in.astype(BF16), w_ffn_out.astype(BF16)
    for i in range(DEPTH):
        kind, j = i % N_MIXERS, i // N_MIXERS
        row = lambda v: v.reshape(1, -1)
        if kind == 0:
            wp, ps = w_pool[j].astype(BF16), row(pool_scale[j])
        elif kind == 1:
            mla_w = [_mla_weights(w_dq[j], w_uq[j], w_dkv[j], w_uk[j], w_uv[j], w_mla_o[j], rope)
                     for rope in (False, True)]
            gq, gkv = row(g_q[j]), row(g_kv[j])
        else:
            w_in = w_lru_in[j].astype(BF16)
            wg, bg = _lru_gate_weights(w_lru_a[j], b_lru_a[j], w_lru_i[j], b_lru_i[j])
            w_out = w_lru_out[j].astype(BF16)
        for p in range(2):
            x, m = xs[p], _Cond(mod, lng, lnb, i, p == 1)
            if kind == 0:
                x = _pool_layer(x, m, wp, ps)
            elif kind == 1:
                wdq, wq, wdkv, wuk, wuv, wo, wq_rot = mla_w[p]
                if p == 0:
                    qcat, kcat, v, ckv, kr64 = _mla_project(
                        x, m, wdq, gq, wq, wdkv, gkv, wuk, wuv, None, True)
                    ckv_out.append(ckv)
                    krope_out.append(kr64)
                    segments = [(kcat, v)]
                else:
                    qcat, kcat, v = _mla_project(
                        x, m, wdq, gq, wq, wdkv, gkv, wuk, wuv, (wq_rot, cos, sin), False)
                    kr_c = cache_mla_krope[:, j]
                    kr_c = jnp.concatenate([kr_c, kr_c], axis=-1).astype(BF16)
                    kcat_c, v_c = _kv_up(cache_mla_ckv[:, j], kr_c, wuk, wuv)
                    segments = [(kcat_c, v_c), (kcat, v)]
                x = _attn_layer(x, m, qcat, wo, segments)
            else:
                if p == 0:
                    h0 = jnp.zeros((x.shape[0], 2, D_RNN), F32)
                else:
                    h0 = state_lru[:, j]
                x, st = _lru_layer(x, m, w_in, lru_conv_w[j], row(lru_conv_b[j]),
                                   wg, bg, lru_lambda[j], h0, w_out)
                if p == 0:
                    lru_out.append(st)
            xs[p] = _ffn_layer(x, m, win, wout)
    return (xs[0], xs[1], jnp.stack(ckv_out, axis=1), jnp.stack(krope_out, axis=1),
            jnp.stack(lru_out, axis=1))
```

```python
import functools
import math
from typing import NamedTuple

import jax
import jax.numpy as jnp
from jax import lax
from jax.experimental import pallas as pl
from jax.experimental.pallas import tpu as pltpu

F32 = jnp.float32
BF16 = jnp.bfloat16

D_MODEL = 1024
DEPTH = 4
GRID_W = 64
N_MIXERS = 3
POOL_WINDOWS = (2, 4, 8, 16)
POOL_GROUPS = 4
POOL_GC = D_MODEL // POOL_GROUPS
MLA_HEADS = 8
Q_LORA = 384
KV_LORA = 256
QK_NOPE = 128
QK_ROPE = 64
V_HEAD = 128
ROPE_NF = QK_ROPE // 4
ROPE_THETA = 10000.0
MLA_SCALE = (QK_NOPE + QK_ROPE) ** -0.5
D_RNN = D_MODEL
LRU_BLOCKS = 8
LRU_BW = D_RNN // LRU_BLOCKS
CONV_W = 4
CONV_LEFT = 1
LRU_C = 8.0
D_FF = ((8 * D_MODEL // 3 + 255) // 256) * 256
ALPHA = (2.0 * DEPTH) ** 0.25
EPS = 1e-6

LANES = 128
SUBLANES = 8
VMEM_LIMIT_BYTES = 56 * 1024 * 1024
PAD_ROWS = SUBLANES
MOD_ROWS = 16
MOD_CHUNKS = 2
LRU_COLS = 512
SEG_PITCH_PAD = 4
PROJ_TILE = 512
FFN_TILE = 512
ATTN_TILE = 512
QK_CAT = 2 * LANES


def _params(*sem):
    return pltpu.CompilerParams(dimension_semantics=sem, vmem_limit_bytes=VMEM_LIMIT_BYTES)


def _dot(a, b):
    return jnp.dot(a, b, preferred_element_type=F32)


def _layer_norm(y, g, b):
    mu = jnp.mean(y, axis=-1, keepdims=True)
    d = y - mu
    var = jnp.mean(d * d, axis=-1, keepdims=True)
    return d * lax.rsqrt(var + EPS) * g + b


def _gelu_tanh(x):
    c = math.sqrt(2.0 / math.pi)
    return (0.5 * x) * (1.0 + jnp.tanh(x * (c + (c * 0.044715) * (x * x))))


def _rms_norm(y, g):
    return y * lax.rsqrt(jnp.mean(y * y, axis=-1, keepdims=True) + EPS) * g


def _modulate(x, mod_ref):
    return x * (1.0 + mod_ref[1, 0]) + mod_ref[0, 0]


def _finish(x, mixed, mod_ref, lng_ref, lnb_ref):
    return _layer_norm(ALPHA * x + mod_ref[2, 0] * mixed, lng_ref[0], lnb_ref[0])


class _Cond(NamedTuple):
    mod: jax.Array
    lng: jax.Array
    lnb: jax.Array
    layer: int
    latent: bool

    @property
    def arrays(self):
        return self.mod, self.lng, self.lnb


def _cond_specs(cond, sub, nargs):
    k = 2 * cond.layer + sub
    row = (lambda b: 1 + b) if cond.latent else (lambda b: 0)
    if nargs == 1:
        mod = pl.BlockSpec((3, 1, 1, D_MODEL), lambda b: (k, row(b), 0, 0))
        ln = pl.BlockSpec((1, 1, D_MODEL), lambda b: (k, 0, 0))
    else:
        mod = pl.BlockSpec((3, 1, 1, D_MODEL), lambda b, t: (k, row(b), 0, 0))
        ln = pl.BlockSpec((1, 1, D_MODEL), lambda b, t: (k, 0, 0))
    return [mod, ln, ln]


def _const_spec(shape, nargs):
    zeros = (0,) * len(shape)
    if nargs == 1:
        return pl.BlockSpec(shape, lambda b: zeros)
    return pl.BlockSpec(shape, lambda b, t: zeros)


def _mod_kernel(cond_ref, w_ref, b_ref, o_ref):
    s = jax.nn.silu(cond_ref[...]).astype(BF16)
    for k in range(o_ref.shape[1]):
        cols = slice(k * D_MODEL, (k + 1) * D_MODEL)
        o_ref[0, k] = _dot(s, w_ref[0, :, cols].astype(BF16)) + b_ref[0, k]


def _modulation(cond, w_ada, b_ada):
    d = D_MODEL
    nk = MOD_CHUNKS
    return pl.pallas_call(
        _mod_kernel,
        grid=(DEPTH, 6 // nk),
        in_specs=[
            pl.BlockSpec((MOD_ROWS, d), lambda i, j: (0, 0)),
            pl.BlockSpec((1, d, nk * d), lambda i, j: (i, 0, j)),
            pl.BlockSpec((1, nk, 1, d), lambda i, j: (i, j, 0, 0)),
        ],
        out_specs=pl.BlockSpec((1, nk, MOD_ROWS, d), lambda i, j: (i, j, 0, 0)),
        out_shape=jax.ShapeDtypeStruct((DEPTH, 6, MOD_ROWS, d), F32),
        compiler_params=_params("arbitrary", "arbitrary"),
        name="adaln_modulation",
    )(cond, w_ada, b_ada.reshape(DEPTH, 6, 1, d))


def _ffn_kernel(x_ref, mod_ref, lng_ref, lnb_ref, win_ref, wout_ref, o_ref):
    x = x_ref[0]
    h = _modulate(x, mod_ref).astype(BF16)
    ab = _dot(h, win_ref[0])
    u = (jax.nn.silu(ab[:, :D_FF]) * ab[:, D_FF:]).astype(BF16)
    o_ref[0] = _finish(x, _dot(u, wout_ref[0]), mod_ref, lng_ref, lnb_ref)


def _ffn_layer(x, cond, win, wout):
    layer = cond.layer
    shape = x.shape
    d = shape[-1]
    if not cond.latent and shape[1] < FFN_TILE:
        x = x.reshape(-1, FFN_TILE, d)
    b, t, _ = x.shape
    tm = min(FFN_TILE, t)
    slab = lambda r, c: pl.BlockSpec((1, r, c), lambda i, j: (layer, 0, 0), pipeline_mode=pl.Buffered(1))
    return pl.pallas_call(
        _ffn_kernel,
        grid=(b, t // tm),
        in_specs=[
            pl.BlockSpec((1, tm, d), lambda i, j: (i, j, 0)),
            *_cond_specs(cond, 1, 2),
            slab(d, 2 * D_FF),
            slab(D_FF, d),
        ],
        out_specs=pl.BlockSpec((1, tm, d), lambda i, j: (i, j, 0)),
        out_shape=jax.ShapeDtypeStruct(x.shape, x.dtype),
        compiler_params=_params("arbitrary", "arbitrary"),
        name="ffn_layer",
    )(x, *cond.arrays, win, wout).reshape(shape)


def _pool_kernel(x_ref, mod_ref, lng_ref, lnb_ref, wp_ref, ps_ref, o_ref):
    t_len = x_ref.shape[1]
    n_pad = t_len + 2 * PAD_ROWS
    x = x_ref[0]
    h = _modulate(x, mod_ref)
    edge = jnp.zeros((PAD_ROWS, POOL_GC), F32)
    edge_row = lax.broadcasted_iota(jnp.int32, (PAD_ROWS, 1), 0)
    outs = []
    for gi, w in enumerate(POOL_WINDOWS):
        assert w & (w - 1) == 0 and w // 2 <= PAD_ROWS
        hg = h[:, gi * POOL_GC:(gi + 1) * POOL_GC]
        run = jnp.concatenate([edge, hg, edge], axis=0)
        span = 1
        while 2 * span < w:
            run = run + pltpu.roll(run, n_pad - span, axis=0)
            span *= 2
        s = (run + pltpu.roll(run, span, axis=0))[PAD_ROWS:PAD_ROWS + t_len]
        lo, hi = -(w // 2), w - w // 2
        parts = []
        for r0 in (0, t_len - PAD_ROWS):
            t = r0 + edge_row
            cnt = (jnp.minimum(t + hi, t_len) - jnp.maximum(t + lo, 0)).astype(F32)
            parts.append(s[r0:r0 + PAD_ROWS] / cnt - hg[r0:r0 + PAD_ROWS])
        mid = s[PAD_ROWS:t_len - PAD_ROWS] * (1.0 / w) - hg[PAD_ROWS:t_len - PAD_ROWS]
        pooled = jnp.concatenate([parts[0], mid, parts[1]], axis=0)
        outs.append(_dot(pooled.astype(BF16), wp_ref[gi]))
    mixed = jnp.concatenate(outs, axis=1) * ps_ref[...]
    o_ref[0] = _finish(x, mixed, mod_ref, lng_ref, lnb_ref)


def _pool_layer(x, cond, wp, ps):
    b, t, d = x.shape
    return pl.pallas_call(
        _pool_kernel,
        grid=(b,),
        in_specs=[
            pl.BlockSpec((1, t, d), lambda i: (i, 0, 0)),
            *_cond_specs(cond, 0, 1),
            _const_spec((POOL_GROUPS, POOL_GC, POOL_GC), 1),
            _const_spec((1, d), 1),
        ],
        out_specs=pl.BlockSpec((1, t, d), lambda i: (i, 0, 0)),
        out_shape=jax.ShapeDtypeStruct(x.shape, x.dtype),
        compiler_params=_params("arbitrary"),
        name="pool_layer",
    )(x, *cond.arrays, wp, ps)


def _store_k_cat(kcat_ref, k_nope, k_rope2):
    for hd in range(MLA_HEADS):
        kcat_ref[0, :, hd * QK_CAT:hd * QK_CAT + QK_NOPE] = k_nope[:, hd * QK_NOPE:(hd + 1) * QK_NOPE]
        kcat_ref[0, :, hd * QK_CAT + QK_NOPE:(hd + 1) * QK_CAT] = k_rope2


def _mla_proj_kernel(*refs, rope, emit_cache):
    x_ref, mod_ref, wdq_ref, gq_ref, wq_ref, wdkv_ref, gkv_ref, wuk_ref, wuv_ref = refs[:9]
    refs = refs[9:]
    if rope:
        wqrot_ref, cos_ref, sin_ref = refs[:3]
        refs = refs[3:]
    qcat_ref, kcat_ref, v_ref = refs[:3]
    h = _modulate(x_ref[0], mod_ref).astype(BF16)
    q_lat = _rms_norm(_dot(h, wdq_ref[...]), gq_ref[...]).astype(BF16)
    nope_w = MLA_HEADS * QK_NOPE
    q = _dot(q_lat, wq_ref[...])
    kv = _dot(h, wdkv_ref[...])
    k_rope2 = kv[:, KV_LORA:KV_LORA + 2 * QK_ROPE]
    if rope:
        cos, sin = cos_ref[...], sin_ref[...]
        q_rot = _dot(q_lat, wqrot_ref[...])
        k_rope2 = k_rope2 * cos + kv[:, KV_LORA + 2 * QK_ROPE:] * sin
    lane = lax.broadcasted_iota(jnp.int32, (1, LANES), 1)
    for pair in range(MLA_HEADS // 2):
        both = q[:, nope_w + pair * LANES:nope_w + (pair + 1) * LANES]
        if rope:
            both = both * cos + q_rot[:, pair * LANES:(pair + 1) * LANES] * sin
        for hd, keep in ((2 * pair, lane < QK_ROPE), (2 * pair + 1, lane >= QK_ROPE)):
            c0 = hd * QK_CAT
            qcat_ref[0, :, c0:c0 + QK_NOPE] = q[:, hd * QK_NOPE:(hd + 1) * QK_NOPE].astype(BF16)
            qcat_ref[0, :, c0 + QK_NOPE:c0 + QK_CAT] = jnp.where(keep, both, 0.0).astype(BF16)
    c_kv = _rms_norm(kv[:, :KV_LORA], gkv_ref[...])
    c_kv16 = c_kv.astype(BF16)
    _store_k_cat(kcat_ref, _dot(c_kv16, wuk_ref[...]).astype(BF16), k_rope2.astype(BF16))
    v_ref[0] = _dot(c_kv16, wuv_ref[...]).astype(BF16)
    if emit_cache:
        ckv_out_ref, kr_out_ref = refs[3:5]
        ckv_out_ref[0] = c_kv
        kr_out_ref[0] = kv[:, KV_LORA:KV_LORA + QK_ROPE]


def _mla_project(x, cond, wdq, gq, wq, wdkv, gkv, wuk, wuv, rope_args, emit_cache):
    b, t, d = x.shape
    tm = min(PROJ_TILE, t)
    rope = rope_args is not None
    tok = lambda w: pl.BlockSpec((1, tm, w), lambda i, j: (i, j, 0))
    args = [x, cond.mod, wdq, gq, wq, wdkv, gkv, wuk, wuv]
    in_specs = [tok(d), _cond_specs(cond, 0, 2)[0]] + [_const_spec(a.shape, 2) for a in args[2:]]
    if rope:
        wqrot, cos, sin = rope_args
        args += [wqrot, cos, sin]
        in_specs += [_const_spec(wqrot.shape, 2)] + [pl.BlockSpec((tm, LANES), lambda i, j: (j, 0))] * 2
    widths = [MLA_HEADS * QK_CAT, MLA_HEADS * QK_CAT, MLA_HEADS * V_HEAD]
    out_specs = [tok(w) for w in widths]
    out_shape = [jax.ShapeDtypeStruct((b, t, w), BF16) for w in widths]
    if emit_cache:
        out_specs += [tok(KV_LORA), tok(QK_ROPE)]
        out_shape += [jax.ShapeDtypeStruct((b, t, KV_LORA), F32),
                      jax.ShapeDtypeStruct((b, t, QK_ROPE), F32)]
    return pl.pallas_call(
        functools.partial(_mla_proj_kernel, rope=rope, emit_cache=emit_cache),
        grid=(b, t // tm),
        in_specs=in_specs,
        out_specs=out_specs,
        out_shape=out_shape,
        compiler_params=_params("arbitrary", "arbitrary"),
        name="mla_project",
    )(*args)


def _kv_up_kernel(ckv_ref, kr2_ref, wuk_ref, wuv_ref, kcat_ref, v_ref):
    c = ckv_ref[0].astype(BF16)
    _store_k_cat(kcat_ref, _dot(c, wuk_ref[...]).astype(BF16), kr2_ref[0])
    v_ref[0] = _dot(c, wuv_ref[...]).astype(BF16)


def _kv_up(ckv, kr2, wuk, wuv):
    b, s, r = ckv.shape
    seq = lambda w: pl.BlockSpec((1, s, w), lambda i: (i, 0, 0))
    widths = [MLA_HEADS * QK_CAT, MLA_HEADS * V_HEAD]
    return pl.pallas_call(
        _kv_up_kernel,
        grid=(b,),
        in_specs=[seq(r), seq(kr2.shape[2]), _const_spec(wuk.shape, 1), _const_spec(wuv.shape, 1)],
        out_specs=[seq(w) for w in widths],
        out_shape=[jax.ShapeDtypeStruct((b, s, w), BF16) for w in widths],
        compiler_params=_params("arbitrary"),
        name="mla_cache_kv_up",
    )(ckv, kr2, wuk, wuv)


def _attn_kernel(*refs, n_seg):
    x_ref, mod_ref, lng_ref, lnb_ref, q_ref, wo_ref = refs[:6]
    seg_refs = [refs[6 + 2 * i:8 + 2 * i] for i in range(n_seg)]
    o_ref = refs[6 + 2 * n_seg]
    nt = (((1,), (1,)), ((), ()))
    exp2_scale = MLA_SCALE * math.log2(math.e)
    heads = []
    for hd in range(MLA_HEADS):
        q = q_ref[0, :, hd * QK_CAT:(hd + 1) * QK_CAT]
        scores = [lax.dot_general(q, k_ref[0, :, hd * QK_CAT:(hd + 1) * QK_CAT], nt,
                                  preferred_element_type=F32) for k_ref, _ in seg_refs]
        m = functools.reduce(jnp.maximum, [jnp.max(s, axis=-1, keepdims=True) for s in scores])
        es = [jnp.exp2((s - m) * exp2_scale) for s in scores]
        total = functools.reduce(lambda a, b: a + b, [jnp.sum(e, axis=-1, keepdims=True) for e in es])
        o = None
        for e, (_, v_ref) in zip(es, seg_refs):
            part = _dot(e.astype(BF16), v_ref[0, :, hd * V_HEAD:(hd + 1) * V_HEAD])
            o = part if o is None else o + part
        heads.append((o * (1.0 / total)).astype(BF16))
    mixed = _dot(jnp.concatenate(heads, axis=1), wo_ref[...])
    o_ref[0] = _finish(x_ref[0], mixed, mod_ref, lng_ref, lnb_ref)


def _attn_layer(x, cond, qcat, wo, segments):
    b, t, d = x.shape
    tq = min(ATTN_TILE, t)
    tok = lambda w: pl.BlockSpec((1, tq, w), lambda i, j: (i, j, 0))
    in_specs = [tok(d), *_cond_specs(cond, 0, 2),
                tok(qcat.shape[2]), _const_spec(wo.shape, 2)]
    args = [x, *cond.arrays, qcat, wo]
    for seg in segments:
        for a in seg:
            in_specs.append(pl.BlockSpec((1,) + a.shape[1:], lambda i, j: (i, 0, 0)))
            args.append(a)
    return pl.pallas_call(
        functools.partial(_attn_kernel, n_seg=len(segments)),
        grid=(b, t // tq),
        in_specs=in_specs,
        out_specs=tok(d),
        out_shape=jax.ShapeDtypeStruct(x.shape, x.dtype),
        compiler_params=_params("arbitrary", "arbitrary"),
        name="mla_attention",
    )(*args)


def _lru_kernel(x_ref, mod_ref, lng_ref, lnb_ref, wu_ref, wy_ref, cw_ref, cb_ref, wg_ref, bg_ref,
                lam_ref, h0_ref, wout_ref, o_ref, st_ref,
                perm_scr, h_scr, acc_scr, pad_scr, af_scr, bf_scr, ab_scr, bb_scr):
    t_len = x_ref.shape[1]
    n_blk = t_len // SUBLANES
    pitch = n_blk + SEG_PITCH_PAD
    n_slab = D_MODEL // LANES
    j = pl.program_id(1)
    slab = lambda c: slice(c * LANES, (c + 1) * LANES)

    @pl.when(j == 0)
    def _():
        for c in range(n_slab):
            for s in range(SUBLANES):
                perm_scr[c, s * pitch:s * pitch + n_blk, :] = x_ref[0, s * n_blk:(s + 1) * n_blk, slab(c)]

        def gather(k, carry):
            r = pl.multiple_of(k * SUBLANES, SUBLANES)
            for c in range(n_slab):
                acc_scr[pl.ds(r, SUBLANES), slab(c)] = perm_scr[c, pl.ds(k, SUBLANES, stride=pitch), :]
            return carry

        lax.fori_loop(0, n_blk, gather, 0, unroll=2)
        h_scr[...] = _modulate(acc_scr[...], mod_ref).astype(BF16)
        acc_scr[...] = jnp.zeros_like(acc_scr)

    h = h_scr[...]
    u = _dot(h, wu_ref[...])
    y = _gelu_tanh(_dot(h, wy_ref[...]))

    sub = lax.broadcasted_iota(jnp.int32, (SUBLANES, LRU_COLS), 0)
    lead = CONV_LEFT * SUBLANES
    for i in range(1, CONV_LEFT + 1):
        blk = u[(n_blk - i) * SUBLANES:(n_blk - i + 1) * SUBLANES]
        pad_scr[lead - i * SUBLANES:lead - (i - 1) * SUBLANES, :] = jnp.where(
            sub >= 1, pltpu.roll(blk, 1, axis=0), 0.0)
    pad_scr[lead:lead + t_len, :] = u
    for i in range(CONV_W - 1 - CONV_LEFT):
        blk = u[i * SUBLANES:(i + 1) * SUBLANES]
        pad_scr[lead + t_len + i * SUBLANES:lead + t_len + (i + 1) * SUBLANES, :] = jnp.where(
            sub < SUBLANES - 1, pltpu.roll(blk, SUBLANES - 1, axis=0), 0.0)
    uc = None
    for k in range(CONV_W):
        term = pad_scr[k * SUBLANES:k * SUBLANES + t_len, :] * cw_ref[k:k + 1, :]
        uc = term if uc is None else uc + term
    uc = uc + cb_ref[...]

    a_scr = (af_scr, ab_scr)
    b_scr = (bf_scr, bb_scr)
    for bl in range(LRU_COLS // LRU_BW):
        c0 = bl * LRU_BW
        ub = uc[:, c0:c0 + LRU_BW]
        tg = jnp.tanh(_dot(ub.astype(BF16), wg_ref[bl]) + bg_ref[bl])
        for dr in range(2):
            tr = tg[:, dr * LRU_BW:(dr + 1) * LRU_BW]
            ti = tg[:, (2 + dr) * LRU_BW:(3 + dr) * LRU_BW]
            half_sp = (-0.25 * LRU_C) * jax.nn.softplus(-lam_ref[dr:dr + 1, c0:c0 + LRU_BW])
            t = jnp.tanh(half_sp * (1.0 + tr))
            inv = 1.0 / (1.0 - t)
            nt = -t
            root = jnp.where(nt > 0.0, nt * lax.rsqrt(nt), 0.0)
            a_scr[dr][:, c0:c0 + LRU_BW] = (1.0 + t) * inv
            b_scr[dr][:, c0:c0 + LRU_BW] = (root * inv) * ((1.0 + ti) * ub)

    def step(k, carry):
        a_f, b_f, a_b, b_b = carry
        rf = pl.multiple_of(k * SUBLANES, SUBLANES)
        rb = pl.multiple_of((n_blk - 1 - k) * SUBLANES, SUBLANES)
        a = af_scr[pl.ds(rf, SUBLANES), :]
        b_f = a * b_f + bf_scr[pl.ds(rf, SUBLANES), :]
        a_f = a * a_f
        af_scr[pl.ds(rf, SUBLANES), :] = a_f
        bf_scr[pl.ds(rf, SUBLANES), :] = b_f
        a = ab_scr[pl.ds(rb, SUBLANES), :]
        b_b = a * b_b + bb_scr[pl.ds(rb, SUBLANES), :]
        a_b = a * a_b
        ab_scr[pl.ds(rb, SUBLANES), :] = a_b
        bb_scr[pl.ds(rb, SUBLANES), :] = b_b
        return a_f, b_f, a_b, b_b

    one = jnp.ones((SUBLANES, LRU_COLS), F32)
    zero = jnp.zeros((SUBLANES, LRU_COLS), F32)
    a_f, b_f, a_b, b_b = lax.fori_loop(0, n_blk, step, (one, zero, one, zero), unroll=4)

    hf = h0_ref[0, 0:1, :]
    hf_in = zero
    for s in range(SUBLANES):
        hf_in = jnp.where(sub == s, hf, hf_in)
        hf = a_f[s:s + 1] * hf + b_f[s:s + 1]
    hb = h0_ref[0, 1:2, :]
    hb_in = zero
    for s in reversed(range(SUBLANES)):
        hb_in = jnp.where(sub == s, hb, hb_in)
        hb = a_b[s:s + 1] * hb + b_b[s:s + 1]
    st_ref[0, 0:1, :] = hf
    st_ref[0, 1:2, :] = hb

    blocks = lambda ref: ref[...].reshape(n_blk, SUBLANES, LRU_COLS)
    h_sum = (blocks(af_scr) * hf_in + blocks(bf_scr)) + (blocks(ab_scr) * hb_in + blocks(bb_scr))
    mixed = (h_sum.reshape(t_len, LRU_COLS) * y).astype(BF16)
    acc_scr[...] += _dot(mixed, wout_ref[...])

    @pl.when(j == pl.num_programs(1) - 1)
    def _():
        def scatter(k, carry):
            r = pl.multiple_of(k * SUBLANES, SUBLANES)
            for c in range(n_slab):
                perm_scr[c, pl.ds(k, SUBLANES, stride=pitch), :] = acc_scr[pl.ds(r, SUBLANES), slab(c)]
            return carry

        lax.fori_loop(0, n_blk, scatter, 0, unroll=2)
        for s in range(SUBLANES):
            rows = slice(s * n_blk, (s + 1) * n_blk)
            mixed_s = jnp.concatenate(
                [perm_scr[c, s * pitch:s * pitch + n_blk, :] for c in range(n_slab)], axis=1)
            o_ref[0, rows, :] = _finish(x_ref[0, rows, :], mixed_s, mod_ref, lng_ref, lnb_ref)


def _lru_layer(x, cond, w_in, conv_w, conv_b, wg, bg, lam, h0, w_out):
    b, t, d = x.shape
    cw = LRU_COLS
    ncb = D_RNN // cw
    bpc = cw // LRU_BW
    pitch = t // SUBLANES + SEG_PITCH_PAD
    seq = lambda w: pl.BlockSpec((1, t, w), lambda i, j: (i, 0, 0))
    col = lambda rows: pl.BlockSpec((rows, cw), lambda i, j: (0, j))
    out, state = pl.pallas_call(
        _lru_kernel,
        grid=(b, ncb),
        in_specs=[
            seq(d),
            *_cond_specs(cond, 0, 2),
            pl.BlockSpec((d, cw), lambda i, j: (0, j)),
            pl.BlockSpec((d, cw), lambda i, j: (0, ncb + j)),
            col(CONV_W),
            col(1),
            pl.BlockSpec((bpc, LRU_BW, 4 * LRU_BW), lambda i, j: (j, 0, 0)),
            pl.BlockSpec((bpc, 1, 4 * LRU_BW), lambda i, j: (j, 0, 0)),
            col(2),
            pl.BlockSpec((1, 2, cw), lambda i, j: (i, 0, j)),
            pl.BlockSpec((cw, d), lambda i, j: (j, 0)),
        ],
        out_specs=[seq(d), pl.BlockSpec((1, 2, cw), lambda i, j: (i, 0, j))],
        out_shape=[jax.ShapeDtypeStruct(x.shape, x.dtype),
                   jax.ShapeDtypeStruct((b, 2, D_RNN), x.dtype)],
        scratch_shapes=[pltpu.VMEM((d // LANES, SUBLANES * pitch, LANES), F32),
                        pltpu.VMEM((t, d), BF16), pltpu.VMEM((t, d), F32),
                        pltpu.VMEM((t + (CONV_W - 1) * SUBLANES, cw), F32)]
                       + [pltpu.VMEM((t, cw), F32)] * 4,
        compiler_params=_params("arbitrary", "arbitrary"),
        name="lru_layer",
    )(x, *cond.arrays, w_in, w_in, conv_w, conv_b, wg, bg, lam, h0, w_out)
    return out, state


def _rope_rotate_cols(w):
    nf = ROPE_NF
    return jnp.concatenate([-w[..., nf:2 * nf], w[..., :nf], -w[..., 3 * nf:], w[..., 2 * nf:3 * nf]], axis=-1)


def _rope_tables(t_len):
    t = jnp.arange(t_len)
    rows = (t // GRID_W).astype(F32)
    cols = (t % GRID_W).astype(F32)
    inv = ROPE_THETA ** (-jnp.arange(ROPE_NF, dtype=F32) / ROPE_NF)
    ar, ac = rows[:, None] * inv, cols[:, None] * inv
    cos = jnp.concatenate([jnp.cos(ar), jnp.cos(ar), jnp.cos(ac), jnp.cos(ac)], axis=1)
    sin = jnp.concatenate([jnp.sin(ar), jnp.sin(ar), jnp.sin(ac), jnp.sin(ac)], axis=1)
    return jnp.tile(cos, (1, 2)), jnp.tile(sin, (1, 2))


def _mla_weights(w_dq, w_uq, w_dkv, w_uk, w_uv, w_o, rope):
    r = w_uq.shape[0]
    w_nope, w_rope = w_uq[:, :, :QK_NOPE], w_uq[:, :, QK_NOPE:]
    wq = jnp.concatenate([w_nope.reshape(r, -1), w_rope.reshape(r, -1)], axis=1).astype(BF16)
    wk = w_dkv[:, KV_LORA:]
    wdkv = [w_dkv[:, :KV_LORA], wk, wk]
    wq_rot = None
    if rope:
        wq_rot = _rope_rotate_cols(w_rope).reshape(r, -1).astype(BF16)
        wk_rot = _rope_rotate_cols(wk)
        wdkv += [wk_rot, wk_rot]
    return (w_dq.astype(BF16), wq, jnp.concatenate(wdkv, axis=1).astype(BF16),
            w_uk.reshape(KV_LORA, -1).astype(BF16), w_uv.reshape(KV_LORA, -1).astype(BF16),
            w_o.astype(BF16), wq_rot)


def _lru_gate_weights(w_a, b_a, w_i, b_i):
    wg = (0.5 * jnp.concatenate([w_a[0], w_a[1], w_i[0], w_i[1]], axis=-1)).astype(BF16)
    blk = lambda v: v.reshape(LRU_BLOCKS, 1, LRU_BW)
    bg = 0.5 * jnp.concatenate([blk(b_a[0]), blk(b_a[1]), blk(b_i[0]), blk(b_i[1])], axis=-1)
    return wg, bg


def kernel(x_prompt, x_sample, cache_mla_ckv, cache_mla_krope, state_lru, c, c_ctx, w_ada, b_ada, ln_g, ln_b, w_ffn_in, w_ffn_out, w_pool, pool_scale, w_dq, g_q, w_uq, w_dkv, g_kv, w_uk, w_uv, w_mla_o, w_lru_in, lru_conv_w, lru_conv_b, w_lru_a, b_lru_a, w_lru_i, b_lru_i, lru_lambda, w_lru_out):
    d = D_MODEL
    n_lat = c.shape[0]
    cond = jnp.concatenate([c_ctx[None], c, jnp.zeros((MOD_ROWS - 1 - n_lat, d), F32)], axis=0)
    mod = _modulation(cond, w_ada, b_ada).reshape(DEPTH * 6, MOD_ROWS, 1, d)
    lng, lnb = ln_g.reshape(DEPTH * 2, 1, d), ln_b.reshape(DEPTH * 2, 1, d)

    xs = [x_prompt, x_sample]
    ckv_out, krope_out, lru_out = [], [], []
    cos, sin = _rope_tables(x_sample.shape[1])
    win, wout = w_ffn_in.astype(BF16), w_ffn_out.astype(BF16)
    for i in range(DEPTH):
        kind, j = i % N_MIXERS, i // N_MIXERS
        row = lambda v: v.reshape(1, -1)
        if kind == 0:
            wp, ps = w_pool[j].astype(BF16), row(pool_scale[j])
        elif kind == 1:
            mla_w = [_mla_weights(w_dq[j], w_uq[j], w_dkv[j], w_uk[j], w_uv[j], w_mla_o[j], rope)
                     for rope in (False, True)]
            gq, gkv = row(g_q[j]), row(g_kv[j])
        else:
            w_in = w_lru_in[j].astype(BF16)
            wg, bg = _lru_gate_weights(w_lru_a[j], b_lru_a[j], w_lru_i[j], b_lru_i[j])
            w_out = w_lru_out[j].astype(BF16)
        for p in range(2):
            x, m = xs[p], _Cond(mod, lng, lnb, i, p == 1)
            if kind == 0:
                x = _pool_layer(x, m, wp, ps)
            elif kind == 1:
                wdq, wq, wdkv, wuk, wuv, wo, wq_rot = mla_w[p]
                if p == 0:
                    qcat, kcat, v, ckv, kr64 = _mla_project(
                        x, m, wdq, gq, wq, wdkv, gkv, wuk, wuv, None, True)
                    ckv_out.append(ckv)
                    krope_out.append(kr64)
                    segments = [(kcat, v)]
                else:
                    qcat, kcat, v = _mla_project(
                        x, m, wdq, gq, wq, wdkv, gkv, wuk, wuv, (wq_rot, cos, sin), False)
                    kr_c = cache_mla_krope[:, j]
                    kr_c = jnp.concatenate([kr_c, kr_c], axis=-1).astype(BF16)
                    kcat_c, v_c = _kv_up(cache_mla_ckv[:, j], kr_c, wuk, wuv)
                    segments = [(kcat_c, v_c), (kcat, v)]
                x = _attn_layer(x, m, qcat, wo, segments)
            else:
                if p == 0:
                    h0 = jnp.zeros((x.shape[0], 2, D_RNN), F32)
                else:
                    h0 = state_lru[:, j]
                x, st = _lru_layer(x, m, w_in, lru_conv_w[j], row(lru_conv_b[j]),
                                   wg, bg, lru_lambda[j], h0, w_out)
                if p == 0:
                    lru_out.append(st)
            xs[p] = _ffn_layer(x, m, win, wout)
    return (xs[0], xs[1], jnp.stack(ckv_out, axis=1), jnp.stack(krope_out, axis=1),
            jnp.stack(lru_out, axis=1))
```

```python
import functools
import math
from typing import NamedTuple

import jax
import jax.numpy as jnp
from jax import lax
from jax.experimental import pallas as pl
from jax.experimental.pallas import tpu as pltpu

F32 = jnp.float32
BF16 = jnp.bfloat16

D_MODEL = 1024
DEPTH = 4
GRID_W = 64
N_MIXERS = 3
POOL_WINDOWS = (2, 4, 8, 16)
POOL_GROUPS = 4
POOL_GC = D_MODEL // POOL_GROUPS
MLA_HEADS = 8
Q_LORA = 384
KV_LORA = 256
QK_NOPE = 128
QK_ROPE = 64
V_HEAD = 128
ROPE_NF = QK_ROPE // 4
ROPE_THETA = 10000.0
MLA_SCALE = (QK_NOPE + QK_ROPE) ** -0.5
D_RNN = D_MODEL
LRU_BLOCKS = 8
LRU_BW = D_RNN // LRU_BLOCKS
CONV_W = 4
CONV_LEFT = 1
LRU_C = 8.0
D_FF = ((8 * D_MODEL // 3 + 255) // 256) * 256
ALPHA = (2.0 * DEPTH) ** 0.25
EPS = 1e-6

LANES = 128
SUBLANES = 8
VMEM_LIMIT_BYTES = 56 * 1024 * 1024
PAD_ROWS = SUBLANES
MOD_ROWS = 16
MOD_CHUNKS = 2
LRU_COLS = 512
SEG_PITCH_PAD = 4
SHORT_SEQ_ROWS = 512
PROJ_TILE = 512
FFN_TILE = 512
ATTN_TILE = 512
QK_CAT = 2 * LANES


def _params(*sem):
    return pltpu.CompilerParams(dimension_semantics=sem, vmem_limit_bytes=VMEM_LIMIT_BYTES)


def _dot(a, b):
    return jnp.dot(a, b, preferred_element_type=F32)


def _layer_norm(y, g, b):
    mu = jnp.mean(y, axis=-1, keepdims=True)
    d = y - mu
    var = jnp.mean(d * d, axis=-1, keepdims=True)
    return d * lax.rsqrt(var + EPS) * g + b


def _gelu_tanh(x):
    c = math.sqrt(2.0 / math.pi)
    return (0.5 * x) * (1.0 + jnp.tanh(x * (c + (c * 0.044715) * (x * x))))


def _rms_norm(y, g):
    return y * lax.rsqrt(jnp.mean(y * y, axis=-1, keepdims=True) + EPS) * g


def _modulate(x, mod_ref):
    return x * (1.0 + mod_ref[1, 0]) + mod_ref[0, 0]


def _finish(x, mixed, mod_ref, lng_ref, lnb_ref):
    return _layer_norm(ALPHA * x + mod_ref[2, 0] * mixed, lng_ref[0], lnb_ref[0])


class _Cond(NamedTuple):
    mod: jax.Array
    lng: jax.Array
    lnb: jax.Array
    layer: int
    latent: bool

    @property
    def arrays(self):
        return self.mod, self.lng, self.lnb


def _cond_specs(cond, sub, nargs):
    k = 2 * cond.layer + sub
    row = (lambda b: 1 + b) if cond.latent else (lambda b: 0)
    if nargs == 1:
        mod = pl.BlockSpec((3, 1, 1, D_MODEL), lambda b: (k, row(b), 0, 0))
        ln = pl.BlockSpec((1, 1, D_MODEL), lambda b: (k, 0, 0))
    else:
        mod = pl.BlockSpec((3, 1, 1, D_MODEL), lambda b, t: (k, row(b), 0, 0))
        ln = pl.BlockSpec((1, 1, D_MODEL), lambda b, t: (k, 0, 0))
    return [mod, ln, ln]


def _const_spec(shape, nargs):
    zeros = (0,) * len(shape)
    if nargs == 1:
        return pl.BlockSpec(shape, lambda b: zeros)
    return pl.BlockSpec(shape, lambda b, t: zeros)


def _mod_kernel(cond_ref, w_ref, b_ref, o_ref):
    s = jax.nn.silu(cond_ref[...]).astype(BF16)
    for k in range(o_ref.shape[1]):
        cols = slice(k * D_MODEL, (k + 1) * D_MODEL)
        o_ref[0, k] = _dot(s, w_ref[0, :, cols].astype(BF16)) + b_ref[0, k]


def _modulation(cond, w_ada, b_ada):
    d = D_MODEL
    nk = MOD_CHUNKS
    return pl.pallas_call(
        _mod_kernel,
        grid=(DEPTH, 6 // nk),
        in_specs=[
            pl.BlockSpec((MOD_ROWS, d), lambda i, j: (0, 0)),
            pl.BlockSpec((1, d, nk * d), lambda i, j: (i, 0, j)),
            pl.BlockSpec((1, nk, 1, d), lambda i, j: (i, j, 0, 0)),
        ],
        out_specs=pl.BlockSpec((1, nk, MOD_ROWS, d), lambda i, j: (i, j, 0, 0)),
        out_shape=jax.ShapeDtypeStruct((DEPTH, 6, MOD_ROWS, d), F32),
        compiler_params=_params("arbitrary", "arbitrary"),
        name="adaln_modulation",
    )(cond, w_ada, b_ada.reshape(DEPTH, 6, 1, d))


def _ffn_kernel(x_ref, mod_ref, lng_ref, lnb_ref, win_ref, wout_ref, o_ref):
    x = x_ref[0]
    h = _modulate(x, mod_ref).astype(BF16)
    ab = _dot(h, win_ref[0])
    u = (jax.nn.silu(ab[:, :D_FF]) * ab[:, D_FF:]).astype(BF16)
    o_ref[0] = _finish(x, _dot(u, wout_ref[0]), mod_ref, lng_ref, lnb_ref)


def _ffn_layer(x, cond, win, wout):
    layer = cond.layer
    shape = x.shape
    d = shape[-1]
    if not cond.latent and shape[1] < FFN_TILE:
        x = x.reshape(-1, FFN_TILE, d)
    b, t, _ = x.shape
    tm = min(FFN_TILE, t)
    slab = lambda r, c: pl.BlockSpec((1, r, c), lambda i, j: (layer, 0, 0), pipeline_mode=pl.Buffered(1))
    return pl.pallas_call(
        _ffn_kernel,
        grid=(b, t // tm),
        in_specs=[
            pl.BlockSpec((1, tm, d), lambda i, j: (i, j, 0)),
            *_cond_specs(cond, 1, 2),
            slab(d, 2 * D_FF),
            slab(D_FF, d),
        ],
        out_specs=pl.BlockSpec((1, tm, d), lambda i, j: (i, j, 0)),
        out_shape=jax.ShapeDtypeStruct(x.shape, x.dtype),
        compiler_params=_params("arbitrary", "arbitrary"),
        name="ffn_layer",
    )(x, *cond.arrays, win, wout).reshape(shape)


def _pool_kernel(x_ref, mod_ref, lng_ref, lnb_ref, wp_ref, ps_ref, o_ref):
    for q in range(x_ref.shape[0]):
        o_ref[q] = _pool_sequence(x_ref[q], mod_ref, lng_ref, lnb_ref, wp_ref, ps_ref)


def _pool_sequence(x, mod_ref, lng_ref, lnb_ref, wp_ref, ps_ref):
    t_len = x.shape[0]
    n_pad = t_len + 2 * PAD_ROWS
    h = _modulate(x, mod_ref)
    edge = jnp.zeros((PAD_ROWS, POOL_GC), F32)
    edge_row = lax.broadcasted_iota(jnp.int32, (PAD_ROWS, 1), 0)
    outs = []
    for gi, w in enumerate(POOL_WINDOWS):
        assert w & (w - 1) == 0 and w // 2 <= PAD_ROWS
        hg = h[:, gi * POOL_GC:(gi + 1) * POOL_GC]
        run = jnp.concatenate([edge, hg, edge], axis=0)
        span = 1
        while 2 * span < w:
            run = run + pltpu.roll(run, n_pad - span, axis=0)
            span *= 2
        s = (run + pltpu.roll(run, span, axis=0))[PAD_ROWS:PAD_ROWS + t_len]
        lo, hi = -(w // 2), w - w // 2
        parts = []
        for r0 in (0, t_len - PAD_ROWS):
            t = r0 + edge_row
            cnt = (jnp.minimum(t + hi, t_len) - jnp.maximum(t + lo, 0)).astype(F32)
            parts.append(s[r0:r0 + PAD_ROWS] / cnt - hg[r0:r0 + PAD_ROWS])
        mid = s[PAD_ROWS:t_len - PAD_ROWS] * (1.0 / w) - hg[PAD_ROWS:t_len - PAD_ROWS]
        pooled = jnp.concatenate([parts[0], mid, parts[1]], axis=0)
        outs.append(_dot(pooled.astype(BF16), wp_ref[gi]))
    mixed = jnp.concatenate(outs, axis=1) * ps_ref[...]
    return _finish(x, mixed, mod_ref, lng_ref, lnb_ref)


def _pool_layer(x, cond, wp, ps):
    b, t, d = x.shape
    nq = 1 if cond.latent else max(1, SHORT_SEQ_ROWS // t)
    return pl.pallas_call(
        _pool_kernel,
        grid=(b // nq,),
        in_specs=[
            pl.BlockSpec((nq, t, d), lambda i: (i, 0, 0)),
            *_cond_specs(cond, 0, 1),
            _const_spec((POOL_GROUPS, POOL_GC, POOL_GC), 1),
            _const_spec((1, d), 1),
        ],
        out_specs=pl.BlockSpec((nq, t, d), lambda i: (i, 0, 0)),
        out_shape=jax.ShapeDtypeStruct(x.shape, x.dtype),
        compiler_params=_params("arbitrary"),
        name="pool_layer",
    )(x, *cond.arrays, wp, ps)


def _store_k_cat(kcat_ref, k_nope, k_rope2):
    for hd in range(MLA_HEADS):
        kcat_ref[0, :, hd * QK_CAT:hd * QK_CAT + QK_NOPE] = k_nope[:, hd * QK_NOPE:(hd + 1) * QK_NOPE]
        kcat_ref[0, :, hd * QK_CAT + QK_NOPE:(hd + 1) * QK_CAT] = k_rope2


def _mla_proj_kernel(*refs, rope, emit_cache):
    x_ref, mod_ref, wdq_ref, gq_ref, wq_ref, wdkv_ref, gkv_ref, wuk_ref, wuv_ref = refs[:9]
    refs = refs[9:]
    if rope:
        wqrot_ref, cos_ref, sin_ref = refs[:3]
        refs = refs[3:]
    qcat_ref, kcat_ref, v_ref = refs[:3]
    h = _modulate(x_ref[0], mod_ref).astype(BF16)
    q_lat = _rms_norm(_dot(h, wdq_ref[...]), gq_ref[...]).astype(BF16)
    nope_w = MLA_HEADS * QK_NOPE
    q = _dot(q_lat, wq_ref[...])
    kv = _dot(h, wdkv_ref[...])
    k_rope2 = kv[:, KV_LORA:KV_LORA + 2 * QK_ROPE]
    if rope:
        cos, sin = cos_ref[...], sin_ref[...]
        q_rot = _dot(q_lat, wqrot_ref[...])
        k_rope2 = k_rope2 * cos + kv[:, KV_LORA + 2 * QK_ROPE:] * sin
    lane = lax.broadcasted_iota(jnp.int32, (1, LANES), 1)
    for pair in range(MLA_HEADS // 2):
        both = q[:, nope_w + pair * LANES:nope_w + (pair + 1) * LANES]
        if rope:
            both = both * cos + q_rot[:, pair * LANES:(pair + 1) * LANES] * sin
        for hd, keep in ((2 * pair, lane < QK_ROPE), (2 * pair + 1, lane >= QK_ROPE)):
            c0 = hd * QK_CAT
            qcat_ref[0, :, c0:c0 + QK_NOPE] = q[:, hd * QK_NOPE:(hd + 1) * QK_NOPE].astype(BF16)
            qcat_ref[0, :, c0 + QK_NOPE:c0 + QK_CAT] = jnp.where(keep, both, 0.0).astype(BF16)
    c_kv = _rms_norm(kv[:, :KV_LORA], gkv_ref[...])
    c_kv16 = c_kv.astype(BF16)
    _store_k_cat(kcat_ref, _dot(c_kv16, wuk_ref[...]).astype(BF16), k_rope2.astype(BF16))
    v_ref[0] = _dot(c_kv16, wuv_ref[...]).astype(BF16)
    if emit_cache:
        ckv_out_ref, kr_out_ref = refs[3:5]
        ckv_out_ref[0] = c_kv
        kr_out_ref[0] = kv[:, KV_LORA:KV_LORA + QK_ROPE]


def _mla_project(x, cond, wdq, gq, wq, wdkv, gkv, wuk, wuv, rope_args, emit_cache):
    b0, t0, d = x.shape
    rope = rope_args is not None
    if not cond.latent and not rope and t0 < PROJ_TILE:
        x = x.reshape(-1, PROJ_TILE, d)
    b, t, _ = x.shape
    tm = min(PROJ_TILE, t)
    tok = lambda w: pl.BlockSpec((1, tm, w), lambda i, j: (i, j, 0))
    args = [x, cond.mod, wdq, gq, wq, wdkv, gkv, wuk, wuv]
    in_specs = [tok(d), _cond_specs(cond, 0, 2)[0]] + [_const_spec(a.shape, 2) for a in args[2:]]
    if rope:
        wqrot, cos, sin = rope_args
        args += [wqrot, cos, sin]
        in_specs += [_const_spec(wqrot.shape, 2)] + [pl.BlockSpec((tm, LANES), lambda i, j: (j, 0))] * 2
    widths = [MLA_HEADS * QK_CAT, MLA_HEADS * QK_CAT, MLA_HEADS * V_HEAD]
    out_specs = [tok(w) for w in widths]
    out_shape = [jax.ShapeDtypeStruct((b, t, w), BF16) for w in widths]
    if emit_cache:
        out_specs += [tok(KV_LORA), tok(QK_ROPE)]
        out_shape += [jax.ShapeDtypeStruct((b, t, KV_LORA), F32),
                      jax.ShapeDtypeStruct((b, t, QK_ROPE), F32)]
    outs = pl.pallas_call(
        functools.partial(_mla_proj_kernel, rope=rope, emit_cache=emit_cache),
        grid=(b, t // tm),
        in_specs=in_specs,
        out_specs=out_specs,
        out_shape=out_shape,
        compiler_params=_params("arbitrary", "arbitrary"),
        name="mla_project",
    )(*args)
    return [o.reshape(b0, t0, o.shape[-1]) for o in outs]


def _kv_up_kernel(ckv_ref, kr2_ref, wuk_ref, wuv_ref, kcat_ref, v_ref):
    c = ckv_ref[0].astype(BF16)
    _store_k_cat(kcat_ref, _dot(c, wuk_ref[...]).astype(BF16), kr2_ref[0])
    v_ref[0] = _dot(c, wuv_ref[...]).astype(BF16)


def _kv_up(ckv, kr2, wuk, wuv):
    b, s, r = ckv.shape
    seq = lambda w: pl.BlockSpec((1, s, w), lambda i: (i, 0, 0))
    widths = [MLA_HEADS * QK_CAT, MLA_HEADS * V_HEAD]
    return pl.pallas_call(
        _kv_up_kernel,
        grid=(b,),
        in_specs=[seq(r), seq(kr2.shape[2]), _const_spec(wuk.shape, 1), _const_spec(wuv.shape, 1)],
        out_specs=[seq(w) for w in widths],
        out_shape=[jax.ShapeDtypeStruct((b, s, w), BF16) for w in widths],
        compiler_params=_params("arbitrary"),
        name="mla_cache_kv_up",
    )(ckv, kr2, wuk, wuv)


def _attn_kernel(*refs, n_seg):
    x_ref, mod_ref, lng_ref, lnb_ref, q_ref, wo_ref = refs[:6]
    seg_refs = [refs[6 + 2 * i:8 + 2 * i] for i in range(n_seg)]
    o_ref = refs[6 + 2 * n_seg]
    nt = (((1,), (1,)), ((), ()))
    exp2_scale = MLA_SCALE * math.log2(math.e)
    heads = []
    for hd in range(MLA_HEADS):
        q = q_ref[0, :, hd * QK_CAT:(hd + 1) * QK_CAT]
        scores = [lax.dot_general(q, k_ref[0, :, hd * QK_CAT:(hd + 1) * QK_CAT], nt,
                                  preferred_element_type=F32) for k_ref, _ in seg_refs]
        m = functools.reduce(jnp.maximum, [jnp.max(s, axis=-1, keepdims=True) for s in scores])
        es = [jnp.exp2((s - m) * exp2_scale) for s in scores]
        total = functools.reduce(lambda a, b: a + b, [jnp.sum(e, axis=-1, keepdims=True) for e in es])
        o = None
        for e, (_, v_ref) in zip(es, seg_refs):
            part = _dot(e.astype(BF16), v_ref[0, :, hd * V_HEAD:(hd + 1) * V_HEAD])
            o = part if o is None else o + part
        heads.append((o * (1.0 / total)).astype(BF16))
    mixed = _dot(jnp.concatenate(heads, axis=1), wo_ref[...])
    o_ref[0] = _finish(x_ref[0], mixed, mod_ref, lng_ref, lnb_ref)


def _attn_layer(x, cond, qcat, wo, segments):
    b, t, d = x.shape
    tq = min(ATTN_TILE, t)
    tok = lambda w: pl.BlockSpec((1, tq, w), lambda i, j: (i, j, 0))
    in_specs = [tok(d), *_cond_specs(cond, 0, 2),
                tok(qcat.shape[2]), _const_spec(wo.shape, 2)]
    args = [x, *cond.arrays, qcat, wo]
    for seg in segments:
        for a in seg:
            in_specs.append(pl.BlockSpec((1,) + a.shape[1:], lambda i, j: (i, 0, 0)))
            args.append(a)
    return pl.pallas_call(
        functools.partial(_attn_kernel, n_seg=len(segments)),
        grid=(b, t // tq),
        in_specs=in_specs,
        out_specs=tok(d),
        out_shape=jax.ShapeDtypeStruct(x.shape, x.dtype),
        compiler_params=_params("arbitrary", "arbitrary"),
        name="mla_attention",
    )(*args)


def _lru_kernel(x_ref, mod_ref, lng_ref, lnb_ref, wu_ref, wy_ref, cw_ref, cb_ref, wg_ref, bg_ref,
                lam_ref, h0_ref, wout_ref, o_ref, st_ref,
                perm_scr, h_scr, acc_scr, pad_scr, af_scr, bf_scr, ab_scr, bb_scr):
    t_len = x_ref.shape[1]
    n_blk = t_len // SUBLANES
    pitch = n_blk + SEG_PITCH_PAD
    n_slab = D_MODEL // LANES
    j = pl.program_id(1)
    slab = lambda c: slice(c * LANES, (c + 1) * LANES)

    @pl.when(j == 0)
    def _():
        for c in range(n_slab):
            for s in range(SUBLANES):
                perm_scr[c, s * pitch:s * pitch + n_blk, :] = x_ref[0, s * n_blk:(s + 1) * n_blk, slab(c)]

        def gather(k, carry):
            r = pl.multiple_of(k * SUBLANES, SUBLANES)
            for c in range(n_slab):
                acc_scr[pl.ds(r, SUBLANES), slab(c)] = perm_scr[c, pl.ds(k, SUBLANES, stride=pitch), :]
            return carry

        lax.fori_loop(0, n_blk, gather, 0, unroll=2)
        h_scr[...] = _modulate(acc_scr[...], mod_ref).astype(BF16)
        acc_scr[...] = jnp.zeros_like(acc_scr)

    h = h_scr[...]
    u = _dot(h, wu_ref[...])
    y = _gelu_tanh(_dot(h, wy_ref[...]))

    sub = lax.broadcasted_iota(jnp.int32, (SUBLANES, LRU_COLS), 0)
    lead = CONV_LEFT * SUBLANES
    for i in range(1, CONV_LEFT + 1):
        blk = u[(n_blk - i) * SUBLANES:(n_blk - i + 1) * SUBLANES]
        pad_scr[lead - i * SUBLANES:lead - (i - 1) * SUBLANES, :] = jnp.where(
            sub >= 1, pltpu.roll(blk, 1, axis=0), 0.0)
    pad_scr[lead:lead + t_len, :] = u
    for i in range(CONV_W - 1 - CONV_LEFT):
        blk = u[i * SUBLANES:(i + 1) * SUBLANES]
        pad_scr[lead + t_len + i * SUBLANES:lead + t_len + (i + 1) * SUBLANES, :] = jnp.where(
            sub < SUBLANES - 1, pltpu.roll(blk, SUBLANES - 1, axis=0), 0.0)
    uc = None
    for k in range(CONV_W):
        term = pad_scr[k * SUBLANES:k * SUBLANES + t_len, :] * cw_ref[k:k + 1, :]
        uc = term if uc is None else uc + term
    uc = uc + cb_ref[...]

    a_scr = (af_scr, ab_scr)
    b_scr = (bf_scr, bb_scr)
    for bl in range(LRU_COLS // LRU_BW):
        c0 = bl * LRU_BW
        ub = uc[:, c0:c0 + LRU_BW]
        tg = jnp.tanh(_dot(ub.astype(BF16), wg_ref[bl]) + bg_ref[bl])
        for dr in range(2):
            tr = tg[:, dr * LRU_BW:(dr + 1) * LRU_BW]
            ti = tg[:, (2 + dr) * LRU_BW:(3 + dr) * LRU_BW]
            half_sp = (-0.25 * LRU_C) * jax.nn.softplus(-lam_ref[dr:dr + 1, c0:c0 + LRU_BW])
            t = jnp.tanh(half_sp * (1.0 + tr))
            inv = 1.0 / (1.0 - t)
            nt = -t
            root = jnp.where(nt > 0.0, nt * lax.rsqrt(nt), 0.0)
            a_scr[dr][:, c0:c0 + LRU_BW] = (1.0 + t) * inv
            b_scr[dr][:, c0:c0 + LRU_BW] = (root * inv) * ((1.0 + ti) * ub)

    def step(k, carry):
        a_f, b_f, a_b, b_b = carry
        rf = pl.multiple_of(k * SUBLANES, SUBLANES)
        rb = pl.multiple_of((n_blk - 1 - k) * SUBLANES, SUBLANES)
        a = af_scr[pl.ds(rf, SUBLANES), :]
        b_f = a * b_f + bf_scr[pl.ds(rf, SUBLANES), :]
        a_f = a * a_f
        af_scr[pl.ds(rf, SUBLANES), :] = a_f
        bf_scr[pl.ds(rf, SUBLANES), :] = b_f
        a = ab_scr[pl.ds(rb, SUBLANES), :]
        b_b = a * b_b + bb_scr[pl.ds(rb, SUBLANES), :]
        a_b = a * a_b
        ab_scr[pl.ds(rb, SUBLANES), :] = a_b
        bb_scr[pl.ds(rb, SUBLANES), :] = b_b
        return a_f, b_f, a_b, b_b

    one = jnp.ones((SUBLANES, LRU_COLS), F32)
    zero = jnp.zeros((SUBLANES, LRU_COLS), F32)
    a_f, b_f, a_b, b_b = lax.fori_loop(0, n_blk, step, (one, zero, one, zero), unroll=4)

    hf = h0_ref[0, 0:1, :]
    hf_in = zero
    for s in range(SUBLANES):
        hf_in = jnp.where(sub == s, hf, hf_in)
        hf = a_f[s:s + 1] * hf + b_f[s:s + 1]
    hb = h0_ref[0, 1:2, :]
    hb_in = zero
    for s in reversed(range(SUBLANES)):
        hb_in = jnp.where(sub == s, hb, hb_in)
        hb = a_b[s:s + 1] * hb + b_b[s:s + 1]
    st_ref[0, 0:1, :] = hf
    st_ref[0, 1:2, :] = hb

    blocks = lambda ref: ref[...].reshape(n_blk, SUBLANES, LRU_COLS)
    h_sum = (blocks(af_scr) * hf_in + blocks(bf_scr)) + (blocks(ab_scr) * hb_in + blocks(bb_scr))
    mixed = (h_sum.reshape(t_len, LRU_COLS) * y).astype(BF16)
    acc_scr[...] += _dot(mixed, wout_ref[...])

    @pl.when(j == pl.num_programs(1) - 1)
    def _():
        def scatter(k, carry):
            r = pl.multiple_of(k * SUBLANES, SUBLANES)
            for c in range(n_slab):
                perm_scr[c, pl.ds(k, SUBLANES, stride=pitch), :] = acc_scr[pl.ds(r, SUBLANES), slab(c)]
            return carry

        lax.fori_loop(0, n_blk, scatter, 0, unroll=2)
        for s in range(SUBLANES):
            rows = slice(s * n_blk, (s + 1) * n_blk)
            mixed_s = jnp.concatenate(
                [perm_scr[c, s * pitch:s * pitch + n_blk, :] for c in range(n_slab)], axis=1)
            o_ref[0, rows, :] = _finish(x_ref[0, rows, :], mixed_s, mod_ref, lng_ref, lnb_ref)


def _lru_layer(x, cond, w_in, conv_w, conv_b, wg, bg, lam, h0, w_out):
    b, t, d = x.shape
    cw = LRU_COLS
    ncb = D_RNN // cw
    bpc = cw // LRU_BW
    pitch = t // SUBLANES + SEG_PITCH_PAD
    seq = lambda w: pl.BlockSpec((1, t, w), lambda i, j: (i, 0, 0))
    col = lambda rows: pl.BlockSpec((rows, cw), lambda i, j: (0, j))
    out, state = pl.pallas_call(
        _lru_kernel,
        grid=(b, ncb),
        in_specs=[
            seq(d),
            *_cond_specs(cond, 0, 2),
            pl.BlockSpec((d, cw), lambda i, j: (0, j)),
            pl.BlockSpec((d, cw), lambda i, j: (0, ncb + j)),
            col(CONV_W),
            col(1),
            pl.BlockSpec((bpc, LRU_BW, 4 * LRU_BW), lambda i, j: (j, 0, 0)),
            pl.BlockSpec((bpc, 1, 4 * LRU_BW), lambda i, j: (j, 0, 0)),
            col(2),
            pl.BlockSpec((1, 2, cw), lambda i, j: (i, 0, j)),
            pl.BlockSpec((cw, d), lambda i, j: (j, 0)),
        ],
        out_specs=[seq(d), pl.BlockSpec((1, 2, cw), lambda i, j: (i, 0, j))],
        out_shape=[jax.ShapeDtypeStruct(x.shape, x.dtype),
                   jax.ShapeDtypeStruct((b, 2, D_RNN), x.dtype)],
        scratch_shapes=[pltpu.VMEM((d // LANES, SUBLANES * pitch, LANES), F32),
                        pltpu.VMEM((t, d), BF16), pltpu.VMEM((t, d), F32),
                        pltpu.VMEM((t + (CONV_W - 1) * SUBLANES, cw), F32)]
                       + [pltpu.VMEM((t, cw), F32)] * 4,
        compiler_params=_params("arbitrary", "arbitrary"),
        name="lru_layer",
    )(x, *cond.arrays, w_in, w_in, conv_w, conv_b, wg, bg, lam, h0, w_out)
    return out, state


def _rope_rotate_cols(w):
    nf = ROPE_NF
    return jnp.concatenate([-w[..., nf:2 * nf], w[..., :nf], -w[..., 3 * nf:], w[..., 2 * nf:3 * nf]], axis=-1)


def _rope_tables(t_len):
    t = jnp.arange(t_len)
    rows = (t // GRID_W).astype(F32)
    cols = (t % GRID_W).astype(F32)
    inv = ROPE_THETA ** (-jnp.arange(ROPE_NF, dtype=F32) / ROPE_NF)
    ar, ac = rows[:, None] * inv, cols[:, None] * inv
    cos = jnp.concatenate([jnp.cos(ar), jnp.cos(ar), jnp.cos(ac), jnp.cos(ac)], axis=1)
    sin = jnp.concatenate([jnp.sin(ar), jnp.sin(ar), jnp.sin(ac), jnp.sin(ac)], axis=1)
    return jnp.tile(cos, (1, 2)), jnp.tile(sin, (1, 2))


def _mla_weights(w_dq, w_uq, w_dkv, w_uk, w_uv, w_o, rope):
    r = w_uq.shape[0]
    w_nope, w_rope = w_uq[:, :, :QK_NOPE], w_uq[:, :, QK_NOPE:]
    wq = jnp.concatenate([w_nope.reshape(r, -1), w_rope.reshape(r, -1)], axis=1).astype(BF16)
    wk = w_dkv[:, KV_LORA:]
    wdkv = [w_dkv[:, :KV_LORA], wk, wk]
    wq_rot = None
    if rope:
        wq_rot = _rope_rotate_cols(w_rope).reshape(r, -1).astype(BF16)
        wk_rot = _rope_rotate_cols(wk)
        wdkv += [wk_rot, wk_rot]
    return (w_dq.astype(BF16), wq, jnp.concatenate(wdkv, axis=1).astype(BF16),
            w_uk.reshape(KV_LORA, -1).astype(BF16), w_uv.reshape(KV_LORA, -1).astype(BF16),
            w_o.astype(BF16), wq_rot)


def _lru_gate_weights(w_a, b_a, w_i, b_i):
    wg = (0.5 * jnp.concatenate([w_a[0], w_a[1], w_i[0], w_i[1]], axis=-1)).astype(BF16)
    blk = lambda v: v.reshape(LRU_BLOCKS, 1, LRU_BW)
    bg = 0.5 * jnp.concatenate([blk(b_a[0]), blk(b_a[1]), blk(b_i[0]), blk(b_i[1])], axis=-1)
    return wg, bg


def kernel(x_prompt, x_sample, cache_mla_ckv, cache_mla_krope, state_lru, c, c_ctx, w_ada, b_ada, ln_g, ln_b, w_ffn_in, w_ffn_out, w_pool, pool_scale, w_dq, g_q, w_uq, w_dkv, g_kv, w_uk, w_uv, w_mla_o, w_lru_in, lru_conv_w, lru_conv_b, w_lru_a, b_lru_a, w_lru_i, b_lru_i, lru_lambda, w_lru_out):
    d = D_MODEL
    n_lat = c.shape[0]
    cond = jnp.concatenate([c_ctx[None], c, jnp.zeros((MOD_ROWS - 1 - n_lat, d), F32)], axis=0)
    mod = _modulation(cond, w_ada, b_ada).reshape(DEPTH * 6, MOD_ROWS, 1, d)
    lng, lnb = ln_g.reshape(DEPTH * 2, 1, d), ln_b.reshape(DEPTH * 2, 1, d)

    xs = [x_prompt, x_sample]
    ckv_out, krope_out, lru_out = [], [], []
    cos, sin = _rope_tables(x_sample.shape[1])
    win, wout = w_ffn_in.astype(BF16), w_ffn_out.astype(BF16)
    for i in range(DEPTH):
        kind, j = i % N_MIXERS, i // N_MIXERS
        row = lambda v: v.reshape(1, -1)
        if kind == 0:
            wp, ps = w_pool[j].astype(BF16), row(pool_scale[j])
        elif kind == 1:
            mla_w = [_mla_weights(w_dq[j], w_uq[j], w_dkv[j], w_uk[j], w_uv[j], w_mla_o[j], rope)
                     for rope in (False, True)]
            gq, gkv = row(g_q[j]), row(g_kv[j])
        else:
            w_in = w_lru_in[j].astype(BF16)
            wg, bg = _lru_gate_weights(w_lru_a[j], b_lru_a[j], w_lru_i[j], b_lru_i[j])
            w_out = w_lru_out[j].astype(BF16)
        for p in range(2):
            x, m = xs[p], _Cond(mod, lng, lnb, i, p == 1)
            if kind == 0:
                x = _pool_layer(x, m, wp, ps)
            elif kind == 1:
                wdq, wq, wdkv, wuk, wuv, wo, wq_rot = mla_w[p]
                if p == 0:
                    qcat, kcat, v, ckv, kr64 = _mla_project(
                        x, m, wdq, gq, wq, wdkv, gkv, wuk, wuv, None, True)
                    ckv_out.append(ckv)
                    krope_out.append(kr64)
                    segments = [(kcat, v)]
                else:
                    qcat, kcat, v = _mla_project(
                        x, m, wdq, gq, wq, wdkv, gkv, wuk, wuv, (wq_rot, cos, sin), False)
                    kr_c = cache_mla_krope[:, j]
                    kr_c = jnp.concatenate([kr_c, kr_c], axis=-1).astype(BF16)
                    kcat_c, v_c = _kv_up(cache_mla_ckv[:, j], kr_c, wuk, wuv)
                    segments = [(kcat_c, v_c), (kcat, v)]
                x = _attn_layer(x, m, qcat, wo, segments)
            else:
                if p == 0:
                    h0 = jnp.zeros((x.shape[0], 2, D_RNN), F32)
                else:
                    h0 = state_lru[:, j]
                x, st = _lru_layer(x, m, w_in, lru_conv_w[j], row(lru_conv_b[j]),
                                   wg, bg, lru_lambda[j], h0, w_out)
                if p == 0:
                    lru_out.append(st)
            xs[p] = _ffn_layer(x, m, win, wout)
    return (xs[0], xs[1], jnp.stack(ckv_out, axis=1), jnp.stack(krope_out, axis=1),
            jnp.stack(lru_out, axis=1))
```

```python
import functools
import math
from typing import NamedTuple

import jax
import jax.numpy as jnp
from jax import lax
from jax.experimental import pallas as pl
from jax.experimental.pallas import tpu as pltpu

F32 = jnp.float32
BF16 = jnp.bfloat16

D_MODEL = 1024
DEPTH = 4
GRID_W = 64
N_MIXERS = 3
POOL_WINDOWS = (2, 4, 8, 16)
POOL_GROUPS = 4
POOL_GC = D_MODEL // POOL_GROUPS
MLA_HEADS = 8
Q_LORA = 384
KV_LORA = 256
QK_NOPE = 128
QK_ROPE = 64
V_HEAD = 128
ROPE_NF = QK_ROPE // 4
ROPE_THETA = 10000.0
MLA_SCALE = (QK_NOPE + QK_ROPE) ** -0.5
D_RNN = D_MODEL
LRU_BLOCKS = 8
LRU_BW = D_RNN // LRU_BLOCKS
CONV_W = 4
CONV_LEFT = 1
LRU_C = 8.0
D_FF = ((8 * D_MODEL // 3 + 255) // 256) * 256
ALPHA = (2.0 * DEPTH) ** 0.25
EPS = 1e-6

LANES = 128
SUBLANES = 8
VMEM_LIMIT_BYTES = 56 * 1024 * 1024
PAD_ROWS = SUBLANES
MOD_ROWS = 16
MOD_CHUNKS = 2
LRU_COLS = 512
SEG_PITCH_PAD = 4
SHORT_SEQ_ROWS = 512
PROJ_TILE = 512
FFN_TILE = 512
ATTN_TILE = 512
QK_CAT = 2 * LANES


def _params(*sem):
    return pltpu.CompilerParams(dimension_semantics=sem, vmem_limit_bytes=VMEM_LIMIT_BYTES)


def _dot(a, b):
    return jnp.dot(a, b, preferred_element_type=F32)


def _layer_norm(y, g, b):
    mu = jnp.mean(y, axis=-1, keepdims=True)
    d = y - mu
    var = jnp.mean(d * d, axis=-1, keepdims=True)
    return d * lax.rsqrt(var + EPS) * g + b


def _gelu_tanh(x):
    c = math.sqrt(2.0 / math.pi)
    return (0.5 * x) * (1.0 + jnp.tanh(x * (c + (c * 0.044715) * (x * x))))


def _rms_norm(y, g):
    return y * lax.rsqrt(jnp.mean(y * y, axis=-1, keepdims=True) + EPS) * g


def _modulate(x, mod_ref):
    return x * (1.0 + mod_ref[1, 0]) + mod_ref[0, 0]


def _finish(x, mixed, mod_ref, lng_ref, lnb_ref):
    return _layer_norm(ALPHA * x + mod_ref[2, 0] * mixed, lng_ref[0], lnb_ref[0])


class _Cond(NamedTuple):
    mod: jax.Array
    lng: jax.Array
    lnb: jax.Array
    layer: int
    latent: bool

    @property
    def arrays(self):
        return self.mod, self.lng, self.lnb


def _cond_specs(cond, sub, nargs):
    k = 2 * cond.layer + sub
    row = (lambda b: 1 + b) if cond.latent else (lambda b: 0)
    if nargs == 1:
        mod = pl.BlockSpec((3, 1, 1, D_MODEL), lambda b: (k, row(b), 0, 0))
        ln = pl.BlockSpec((1, 1, D_MODEL), lambda b: (k, 0, 0))
    else:
        mod = pl.BlockSpec((3, 1, 1, D_MODEL), lambda b, t: (k, row(b), 0, 0))
        ln = pl.BlockSpec((1, 1, D_MODEL), lambda b, t: (k, 0, 0))
    return [mod, ln, ln]


def _const_spec(shape, nargs):
    zeros = (0,) * len(shape)
    if nargs == 1:
        return pl.BlockSpec(shape, lambda b: zeros)
    return pl.BlockSpec(shape, lambda b, t: zeros)


def _mod_kernel(cond_ref, w_ref, b_ref, o_ref):
    s = jax.nn.silu(cond_ref[...]).astype(BF16)
    for k in range(o_ref.shape[1]):
        cols = slice(k * D_MODEL, (k + 1) * D_MODEL)
        o_ref[0, k] = _dot(s, w_ref[0, :, cols].astype(BF16)) + b_ref[0, k]


def _modulation(cond, w_ada, b_ada):
    d = D_MODEL
    nk = MOD_CHUNKS
    return pl.pallas_call(
        _mod_kernel,
        grid=(DEPTH, 6 // nk),
        in_specs=[
            pl.BlockSpec((MOD_ROWS, d), lambda i, j: (0, 0)),
            pl.BlockSpec((1, d, nk * d), lambda i, j: (i, 0, j)),
            pl.BlockSpec((1, nk, 1, d), lambda i, j: (i, j, 0, 0)),
        ],
        out_specs=pl.BlockSpec((1, nk, MOD_ROWS, d), lambda i, j: (i, j, 0, 0)),
        out_shape=jax.ShapeDtypeStruct((DEPTH, 6, MOD_ROWS, d), F32),
        compiler_params=_params("arbitrary", "arbitrary"),
        name="adaln_modulation",
    )(cond, w_ada, b_ada.reshape(DEPTH, 6, 1, d))


def _ffn_kernel(x_ref, mod_ref, lng_ref, lnb_ref, win_ref, wout_ref, o_ref):
    x = x_ref[0]
    h = _modulate(x, mod_ref).astype(BF16)
    ab = _dot(h, win_ref[0])
    u = (jax.nn.silu(ab[:, :D_FF]) * ab[:, D_FF:]).astype(BF16)
    o_ref[0] = _finish(x, _dot(u, wout_ref[0]), mod_ref, lng_ref, lnb_ref)


def _ffn_layer(x, cond, win, wout):
    layer = cond.layer
    shape = x.shape
    d = shape[-1]
    if not cond.latent and shape[1] < FFN_TILE:
        x = x.reshape(-1, FFN_TILE, d)
    b, t, _ = x.shape
    tm = min(FFN_TILE, t)
    slab = lambda r, c: pl.BlockSpec((1, r, c), lambda i, j: (layer, 0, 0), pipeline_mode=pl.Buffered(1))
    return pl.pallas_call(
        _ffn_kernel,
        grid=(b, t // tm),
        in_specs=[
            pl.BlockSpec((1, tm, d), lambda i, j: (i, j, 0)),
            *_cond_specs(cond, 1, 2),
            slab(d, 2 * D_FF),
            slab(D_FF, d),
        ],
        out_specs=pl.BlockSpec((1, tm, d), lambda i, j: (i, j, 0)),
        out_shape=jax.ShapeDtypeStruct(x.shape, x.dtype),
        compiler_params=_params("arbitrary", "arbitrary"),
        name="ffn_layer",
    )(x, *cond.arrays, win, wout).reshape(shape)


def _pool_kernel(x_ref, mod_ref, lng_ref, lnb_ref, wp_ref, ps_ref, o_ref):
    for q in range(x_ref.shape[0]):
        o_ref[q] = _pool_sequence(x_ref[q], mod_ref, lng_ref, lnb_ref, wp_ref, ps_ref)


def _pool_sequence(x, mod_ref, lng_ref, lnb_ref, wp_ref, ps_ref):
    t_len = x.shape[0]
    n_pad = t_len + 2 * PAD_ROWS
    h = _modulate(x, mod_ref)
    edge = jnp.zeros((PAD_ROWS, POOL_GC), F32)
    edge_row = lax.broadcasted_iota(jnp.int32, (PAD_ROWS, 1), 0)
    outs = []
    for gi, w in enumerate(POOL_WINDOWS):
        assert w & (w - 1) == 0 and w // 2 <= PAD_ROWS
        hg = h[:, gi * POOL_GC:(gi + 1) * POOL_GC]
        run = jnp.concatenate([edge, hg, edge], axis=0)
        span = 1
        while 2 * span < w:
            run = run + pltpu.roll(run, n_pad - span, axis=0)
            span *= 2
        s = (run + pltpu.roll(run, span, axis=0))[PAD_ROWS:PAD_ROWS + t_len]
        lo, hi = -(w // 2), w - w // 2
        parts = []
        for r0 in (0, t_len - PAD_ROWS):
            t = r0 + edge_row
            cnt = (jnp.minimum(t + hi, t_len) - jnp.maximum(t + lo, 0)).astype(F32)
            parts.append(s[r0:r0 + PAD_ROWS] / cnt - hg[r0:r0 + PAD_ROWS])
        mid = s[PAD_ROWS:t_len - PAD_ROWS] * (1.0 / w) - hg[PAD_ROWS:t_len - PAD_ROWS]
        pooled = jnp.concatenate([parts[0], mid, parts[1]], axis=0)
        outs.append(_dot(pooled.astype(BF16), wp_ref[gi]))
    mixed = jnp.concatenate(outs, axis=1) * ps_ref[...]
    return _finish(x, mixed, mod_ref, lng_ref, lnb_ref)


def _pool_layer(x, cond, wp, ps):
    b, t, d = x.shape
    nq = 1 if cond.latent else max(1, SHORT_SEQ_ROWS // t)
    return pl.pallas_call(
        _pool_kernel,
        grid=(b // nq,),
        in_specs=[
            pl.BlockSpec((nq, t, d), lambda i: (i, 0, 0)),
            *_cond_specs(cond, 0, 1),
            _const_spec((POOL_GROUPS, POOL_GC, POOL_GC), 1),
            _const_spec((1, d), 1),
        ],
        out_specs=pl.BlockSpec((nq, t, d), lambda i: (i, 0, 0)),
        out_shape=jax.ShapeDtypeStruct(x.shape, x.dtype),
        compiler_params=_params("arbitrary"),
        name="pool_layer",
    )(x, *cond.arrays, wp, ps)


def _store_k_cat(kcat_ref, k_nope, k_rope2):
    for hd in range(MLA_HEADS):
        kcat_ref[0, :, hd * QK_CAT:hd * QK_CAT + QK_NOPE] = k_nope[:, hd * QK_NOPE:(hd + 1) * QK_NOPE]
        kcat_ref[0, :, hd * QK_CAT + QK_NOPE:(hd + 1) * QK_CAT] = k_rope2


def _mla_proj_kernel(*refs, rope, emit_cache):
    x_ref, mod_ref, wdq_ref, gq_ref, wq_ref, wdkv_ref, gkv_ref, wuk_ref, wuv_ref = refs[:9]
    refs = refs[9:]
    if rope:
        wqrot_ref, cos_ref, sin_ref = refs[:3]
        refs = refs[3:]
    qcat_ref, kcat_ref, v_ref = refs[:3]
    h = _modulate(x_ref[0], mod_ref).astype(BF16)
    q_lat = _rms_norm(_dot(h, wdq_ref[...]), gq_ref[...]).astype(BF16)
    nope_w = MLA_HEADS * QK_NOPE
    q = _dot(q_lat, wq_ref[...])
    kv = _dot(h, wdkv_ref[...])
    k_rope2 = kv[:, KV_LORA:KV_LORA + 2 * QK_ROPE]
    if rope:
        cos, sin = cos_ref[...], sin_ref[...]
        q_rot = _dot(q_lat, wqrot_ref[...])
        k_rope2 = k_rope2 * cos + kv[:, KV_LORA + 2 * QK_ROPE:] * sin
    lane = lax.broadcasted_iota(jnp.int32, (1, LANES), 1)
    for pair in range(MLA_HEADS // 2):
        both = q[:, nope_w + pair * LANES:nope_w + (pair + 1) * LANES]
        if rope:
            both = both * cos + q_rot[:, pair * LANES:(pair + 1) * LANES] * sin
        for hd, keep in ((2 * pair, lane < QK_ROPE), (2 * pair + 1, lane >= QK_ROPE)):
            c0 = hd * QK_CAT
            qcat_ref[0, :, c0:c0 + QK_NOPE] = q[:, hd * QK_NOPE:(hd + 1) * QK_NOPE].astype(BF16)
            qcat_ref[0, :, c0 + QK_NOPE:c0 + QK_CAT] = jnp.where(keep, both, 0.0).astype(BF16)
    c_kv = _rms_norm(kv[:, :KV_LORA], gkv_ref[...])
    c_kv16 = c_kv.astype(BF16)
    _store_k_cat(kcat_ref, _dot(c_kv16, wuk_ref[...]).astype(BF16), k_rope2.astype(BF16))
    v_ref[0] = _dot(c_kv16, wuv_ref[...]).astype(BF16)
    if emit_cache:
        ckv_out_ref, kr_out_ref = refs[3:5]
        ckv_out_ref[0] = c_kv
        kr_out_ref[0] = kv[:, KV_LORA:KV_LORA + QK_ROPE]


def _mla_project(x, cond, wdq, gq, wq, wdkv, gkv, wuk, wuv, rope_args, emit_cache):
    b0, t0, d = x.shape
    rope = rope_args is not None
    if not cond.latent and not rope and t0 < PROJ_TILE:
        x = x.reshape(-1, PROJ_TILE, d)
    b, t, _ = x.shape
    tm = min(PROJ_TILE, t)
    tok = lambda w: pl.BlockSpec((1, tm, w), lambda i, j: (i, j, 0))
    args = [x, cond.mod, wdq, gq, wq, wdkv, gkv, wuk, wuv]
    in_specs = [tok(d), _cond_specs(cond, 0, 2)[0]] + [_const_spec(a.shape, 2) for a in args[2:]]
    if rope:
        wqrot, cos, sin = rope_args
        args += [wqrot, cos, sin]
        in_specs += [_const_spec(wqrot.shape, 2)] + [pl.BlockSpec((tm, LANES), lambda i, j: (j, 0))] * 2
    widths = [MLA_HEADS * QK_CAT, MLA_HEADS * QK_CAT, MLA_HEADS * V_HEAD]
    out_specs = [tok(w) for w in widths]
    out_shape = [jax.ShapeDtypeStruct((b, t, w), BF16) for w in widths]
    if emit_cache:
        out_specs += [tok(KV_LORA), tok(QK_ROPE)]
        out_shape += [jax.ShapeDtypeStruct((b, t, KV_LORA), F32),
                      jax.ShapeDtypeStruct((b, t, QK_ROPE), F32)]
    outs = pl.pallas_call(
        functools.partial(_mla_proj_kernel, rope=rope, emit_cache=emit_cache),
        grid=(b, t // tm),
        in_specs=in_specs,
        out_specs=out_specs,
        out_shape=out_shape,
        compiler_params=_params("arbitrary", "arbitrary"),
        name="mla_project",
    )(*args)
    return [o.reshape(b0, t0, o.shape[-1]) for o in outs]


def _kv_up_kernel(ckv_ref, kr2_ref, wuk_ref, wuv_ref, kcat_ref, v_ref):
    c = ckv_ref[0].astype(BF16)
    _store_k_cat(kcat_ref, _dot(c, wuk_ref[...]).astype(BF16), kr2_ref[0])
    v_ref[0] = _dot(c, wuv_ref[...]).astype(BF16)


def _kv_up(ckv, kr2, wuk, wuv):
    b, s, r = ckv.shape
    seq = lambda w: pl.BlockSpec((1, s, w), lambda i: (i, 0, 0))
    widths = [MLA_HEADS * QK_CAT, MLA_HEADS * V_HEAD]
    return pl.pallas_call(
        _kv_up_kernel,
        grid=(b,),
        in_specs=[seq(r), seq(kr2.shape[2]), _const_spec(wuk.shape, 1), _const_spec(wuv.shape, 1)],
        out_specs=[seq(w) for w in widths],
        out_shape=[jax.ShapeDtypeStruct((b, s, w), BF16) for w in widths],
        compiler_params=_params("arbitrary"),
        name="mla_cache_kv_up",
    )(ckv, kr2, wuk, wuv)


def _attn_kernel(*refs, n_seg):
    x_ref, mod_ref, lng_ref, lnb_ref, q_ref, wo_ref = refs[:6]
    seg_refs = [refs[6 + 2 * i:8 + 2 * i] for i in range(n_seg)]
    o_ref = refs[6 + 2 * n_seg]
    nt = (((1,), (1,)), ((), ()))
    exp2_scale = MLA_SCALE * math.log2(math.e)
    for sq in range(x_ref.shape[0]):
        heads = []
        for hd in range(MLA_HEADS):
            q = q_ref[sq, :, hd * QK_CAT:(hd + 1) * QK_CAT]
            scores = [lax.dot_general(q, k_ref[sq, :, hd * QK_CAT:(hd + 1) * QK_CAT], nt,
                                      preferred_element_type=F32) for k_ref, _ in seg_refs]
            m = functools.reduce(jnp.maximum, [jnp.max(s, axis=-1, keepdims=True) for s in scores])
            es = [jnp.exp2((s - m) * exp2_scale) for s in scores]
            total = functools.reduce(lambda a, b: a + b, [jnp.sum(e, axis=-1, keepdims=True) for e in es])
            o = None
            for e, (_, v_ref) in zip(es, seg_refs):
                part = _dot(e.astype(BF16), v_ref[sq, :, hd * V_HEAD:(hd + 1) * V_HEAD])
                o = part if o is None else o + part
            heads.append((o * (1.0 / total)).astype(BF16))
        mixed = _dot(jnp.concatenate(heads, axis=1), wo_ref[...])
        o_ref[sq] = _finish(x_ref[sq], mixed, mod_ref, lng_ref, lnb_ref)


def _attn_layer(x, cond, qcat, wo, segments):
    b, t, d = x.shape
    tq = min(ATTN_TILE, t)
    nq = 1 if cond.latent else max(1, SHORT_SEQ_ROWS // t)
    tok = lambda w: pl.BlockSpec((nq, tq, w), lambda i, j: (i, j, 0))
    in_specs = [tok(d), *_cond_specs(cond, 0, 2),
                tok(qcat.shape[2]), _const_spec(wo.shape, 2)]
    args = [x, *cond.arrays, qcat, wo]
    for seg in segments:
        for a in seg:
            in_specs.append(pl.BlockSpec((nq,) + a.shape[1:], lambda i, j: (i, 0, 0)))
            args.append(a)
    return pl.pallas_call(
        functools.partial(_attn_kernel, n_seg=len(segments)),
        grid=(b // nq, t // tq),
        in_specs=in_specs,
        out_specs=tok(d),
        out_shape=jax.ShapeDtypeStruct(x.shape, x.dtype),
        compiler_params=_params("arbitrary", "arbitrary"),
        name="mla_attention",
    )(*args)


def _lru_kernel(x_ref, mod_ref, lng_ref, lnb_ref, wu_ref, wy_ref, cw_ref, cb_ref, wg_ref, bg_ref,
                lam_ref, h0_ref, wout_ref, o_ref, st_ref,
                perm_scr, h_scr, acc_scr, pad_scr, af_scr, bf_scr, ab_scr, bb_scr):
    t_len = x_ref.shape[1]
    n_blk = t_len // SUBLANES
    pitch = n_blk + SEG_PITCH_PAD
    n_slab = D_MODEL // LANES
    cols = wu_ref.shape[1]
    j = pl.program_id(1)
    slab = lambda c: slice(c * LANES, (c + 1) * LANES)

    @pl.when(j == 0)
    def _():
        for c in range(n_slab):
            for s in range(SUBLANES):
                perm_scr[c, s * pitch:s * pitch + n_blk, :] = x_ref[0, s * n_blk:(s + 1) * n_blk, slab(c)]

        def gather(k, carry):
            r = pl.multiple_of(k * SUBLANES, SUBLANES)
            for c in range(n_slab):
                acc_scr[pl.ds(r, SUBLANES), slab(c)] = perm_scr[c, pl.ds(k, SUBLANES, stride=pitch), :]
            return carry

        lax.fori_loop(0, n_blk, gather, 0, unroll=2)
        h_scr[...] = _modulate(acc_scr[...], mod_ref).astype(BF16)
        acc_scr[...] = jnp.zeros_like(acc_scr)

    h = h_scr[...]
    u = _dot(h, wu_ref[...])
    y = _gelu_tanh(_dot(h, wy_ref[...]))

    sub = lax.broadcasted_iota(jnp.int32, (SUBLANES, cols), 0)
    lead = CONV_LEFT * SUBLANES
    for i in range(1, CONV_LEFT + 1):
        blk = u[(n_blk - i) * SUBLANES:(n_blk - i + 1) * SUBLANES]
        pad_scr[lead - i * SUBLANES:lead - (i - 1) * SUBLANES, :] = jnp.where(
            sub >= 1, pltpu.roll(blk, 1, axis=0), 0.0)
    pad_scr[lead:lead + t_len, :] = u
    for i in range(CONV_W - 1 - CONV_LEFT):
        blk = u[i * SUBLANES:(i + 1) * SUBLANES]
        pad_scr[lead + t_len + i * SUBLANES:lead + t_len + (i + 1) * SUBLANES, :] = jnp.where(
            sub < SUBLANES - 1, pltpu.roll(blk, SUBLANES - 1, axis=0), 0.0)
    uc = None
    for k in range(CONV_W):
        term = pad_scr[k * SUBLANES:k * SUBLANES + t_len, :] * cw_ref[k:k + 1, :]
        uc = term if uc is None else uc + term
    uc = uc + cb_ref[...]

    a_scr = (af_scr, ab_scr)
    b_scr = (bf_scr, bb_scr)
    for bl in range(cols // LRU_BW):
        c0 = bl * LRU_BW
        ub = uc[:, c0:c0 + LRU_BW]
        tg = jnp.tanh(_dot(ub.astype(BF16), wg_ref[bl]) + bg_ref[bl])
        for dr in range(2):
            tr = tg[:, dr * LRU_BW:(dr + 1) * LRU_BW]
            ti = tg[:, (2 + dr) * LRU_BW:(3 + dr) * LRU_BW]
            half_sp = (-0.25 * LRU_C) * jax.nn.softplus(-lam_ref[dr:dr + 1, c0:c0 + LRU_BW])
            t = jnp.tanh(half_sp * (1.0 + tr))
            inv = 1.0 / (1.0 - t)
            nt = -t
            root = jnp.where(nt > 0.0, nt * lax.rsqrt(nt), 0.0)
            a_scr[dr][:, c0:c0 + LRU_BW] = (1.0 + t) * inv
            b_scr[dr][:, c0:c0 + LRU_BW] = (root * inv) * ((1.0 + ti) * ub)

    def step(k, carry):
        a_f, b_f, a_b, b_b = carry
        rf = pl.multiple_of(k * SUBLANES, SUBLANES)
        rb = pl.multiple_of((n_blk - 1 - k) * SUBLANES, SUBLANES)
        a = af_scr[pl.ds(rf, SUBLANES), :]
        b_f = a * b_f + bf_scr[pl.ds(rf, SUBLANES), :]
        a_f = a * a_f
        af_scr[pl.ds(rf, SUBLANES), :] = a_f
        bf_scr[pl.ds(rf, SUBLANES), :] = b_f
        a = ab_scr[pl.ds(rb, SUBLANES), :]
        b_b = a * b_b + bb_scr[pl.ds(rb, SUBLANES), :]
        a_b = a * a_b
        ab_scr[pl.ds(rb, SUBLANES), :] = a_b
        bb_scr[pl.ds(rb, SUBLANES), :] = b_b
        return a_f, b_f, a_b, b_b

    one = jnp.ones((SUBLANES, cols), F32)
    zero = jnp.zeros((SUBLANES, cols), F32)
    a_f, b_f, a_b, b_b = lax.fori_loop(0, n_blk, step, (one, zero, one, zero), unroll=4)

    hf = h0_ref[0, 0:1, :]
    hf_in = zero
    for s in range(SUBLANES):
        hf_in = jnp.where(sub == s, hf, hf_in)
        hf = a_f[s:s + 1] * hf + b_f[s:s + 1]
    hb = h0_ref[0, 1:2, :]
    hb_in = zero
    for s in reversed(range(SUBLANES)):
        hb_in = jnp.where(sub == s, hb, hb_in)
        hb = a_b[s:s + 1] * hb + b_b[s:s + 1]
    st_ref[0, 0:1, :] = hf
    st_ref[0, 1:2, :] = hb

    blocks = lambda ref: ref[...].reshape(n_blk, SUBLANES, cols)
    h_sum = (blocks(af_scr) * hf_in + blocks(bf_scr)) + (blocks(ab_scr) * hb_in + blocks(bb_scr))
    mixed = (h_sum.reshape(t_len, cols) * y).astype(BF16)
    acc_scr[...] += _dot(mixed, wout_ref[...])

    @pl.when(j == pl.num_programs(1) - 1)
    def _():
        def scatter(k, carry):
            r = pl.multiple_of(k * SUBLANES, SUBLANES)
            for c in range(n_slab):
                perm_scr[c, pl.ds(k, SUBLANES, stride=pitch), :] = acc_scr[pl.ds(r, SUBLANES), slab(c)]
            return carry

        lax.fori_loop(0, n_blk, scatter, 0, unroll=2)
        for s in range(SUBLANES):
            rows = slice(s * n_blk, (s + 1) * n_blk)
            mixed_s = jnp.concatenate(
                [perm_scr[c, s * pitch:s * pitch + n_blk, :] for c in range(n_slab)], axis=1)
            o_ref[0, rows, :] = _finish(x_ref[0, rows, :], mixed_s, mod_ref, lng_ref, lnb_ref)


def _lru_layer(x, cond, w_in, conv_w, conv_b, wg, bg, lam, h0, w_out):
    b, t, d = x.shape
    cw = LRU_COLS if t > SHORT_SEQ_ROWS else D_RNN
    ncb = D_RNN // cw
    bpc = cw // LRU_BW
    pitch = t // SUBLANES + SEG_PITCH_PAD
    seq = lambda w: pl.BlockSpec((1, t, w), lambda i, j: (i, 0, 0))
    col = lambda rows: pl.BlockSpec((rows, cw), lambda i, j: (0, j))
    out, state = pl.pallas_call(
        _lru_kernel,
        grid=(b, ncb),
        in_specs=[
            seq(d),
            *_cond_specs(cond, 0, 2),
            pl.BlockSpec((d, cw), lambda i, j: (0, j)),
            pl.BlockSpec((d, cw), lambda i, j: (0, ncb + j)),
            col(CONV_W),
            col(1),
            pl.BlockSpec((bpc, LRU_BW, 4 * LRU_BW), lambda i, j: (j, 0, 0)),
            pl.BlockSpec((bpc, 1, 4 * LRU_BW), lambda i, j: (j, 0, 0)),
            col(2),
            pl.BlockSpec((1, 2, cw), lambda i, j: (i, 0, j)),
            pl.BlockSpec((cw, d), lambda i, j: (j, 0)),
        ],
        out_specs=[seq(d), pl.BlockSpec((1, 2, cw), lambda i, j: (i, 0, j))],
        out_shape=[jax.ShapeDtypeStruct(x.shape, x.dtype),
                   jax.ShapeDtypeStruct((b, 2, D_RNN), x.dtype)],
        scratch_shapes=[pltpu.VMEM((d // LANES, SUBLANES * pitch, LANES), F32),
                        pltpu.VMEM((t, d), BF16), pltpu.VMEM((t, d), F32),
                        pltpu.VMEM((t + (CONV_W - 1) * SUBLANES, cw), F32)]
                       + [pltpu.VMEM((t, cw), F32)] * 4,
        compiler_params=_params("arbitrary", "arbitrary"),
        name="lru_layer",
    )(x, *cond.arrays, w_in, w_in, conv_w, conv_b, wg, bg, lam, h0, w_out)
    return out, state


def _rope_rotate_cols(w):
    nf = ROPE_NF
    return jnp.concatenate([-w[..., nf:2 * nf], w[..., :nf], -w[..., 3 * nf:], w[..., 2 * nf:3 * nf]], axis=-1)


def _rope_tables(t_len):
    t = jnp.arange(t_len)
    rows = (t // GRID_W).astype(F32)
    cols = (t % GRID_W).astype(F32)
    inv = ROPE_THETA ** (-jnp.arange(ROPE_NF, dtype=F32) / ROPE_NF)
    ar, ac = rows[:, None] * inv, cols[:, None] * inv
    cos = jnp.concatenate([jnp.cos(ar), jnp.cos(ar), jnp.cos(ac), jnp.cos(ac)], axis=1)
    sin = jnp.concatenate([jnp.sin(ar), jnp.sin(ar), jnp.sin(ac), jnp.sin(ac)], axis=1)
    return jnp.tile(cos, (1, 2)), jnp.tile(sin, (1, 2))


def _mla_weights(w_dq, w_uq, w_dkv, w_uk, w_uv, w_o, rope):
    r = w_uq.shape[0]
    w_nope, w_rope = w_uq[:, :, :QK_NOPE], w_uq[:, :, QK_NOPE:]
    wq = jnp.concatenate([w_nope.reshape(r, -1), w_rope.reshape(r, -1)], axis=1).astype(BF16)
    wk = w_dkv[:, KV_LORA:]
    wdkv = [w_dkv[:, :KV_LORA], wk, wk]
    wq_rot = None
    if rope:
        wq_rot = _rope_rotate_cols(w_rope).reshape(r, -1).astype(BF16)
        wk_rot = _rope_rotate_cols(wk)
        wdkv += [wk_rot, wk_rot]
    return (w_dq.astype(BF16), wq, jnp.concatenate(wdkv, axis=1).astype(BF16),
            w_uk.reshape(KV_LORA, -1).astype(BF16), w_uv.reshape(KV_LORA, -1).astype(BF16),
            w_o.astype(BF16), wq_rot)


def _lru_gate_weights(w_a, b_a, w_i, b_i):
    wg = (0.5 * jnp.concatenate([w_a[0], w_a[1], w_i[0], w_i[1]], axis=-1)).astype(BF16)
    blk = lambda v: v.reshape(LRU_BLOCKS, 1, LRU_BW)
    bg = 0.5 * jnp.concatenate([blk(b_a[0]), blk(b_a[1]), blk(b_i[0]), blk(b_i[1])], axis=-1)
    return wg, bg


def kernel(x_prompt, x_sample, cache_mla_ckv, cache_mla_krope, state_lru, c, c_ctx, w_ada, b_ada, ln_g, ln_b, w_ffn_in, w_ffn_out, w_pool, pool_scale, w_dq, g_q, w_uq, w_dkv, g_kv, w_uk, w_uv, w_mla_o, w_lru_in, lru_conv_w, lru_conv_b, w_lru_a, b_lru_a, w_lru_i, b_lru_i, lru_lambda, w_lru_out):
    d = D_MODEL
    n_lat = c.shape[0]
    cond = jnp.concatenate([c_ctx[None], c, jnp.zeros((MOD_ROWS - 1 - n_lat, d), F32)], axis=0)
    mod = _modulation(cond, w_ada, b_ada).reshape(DEPTH * 6, MOD_ROWS, 1, d)
    lng, lnb = ln_g.reshape(DEPTH * 2, 1, d), ln_b.reshape(DEPTH * 2, 1, d)

    xs = [x_prompt, x_sample]
    ckv_out, krope_out, lru_out = [], [], []
    cos, sin = _rope_tables(x_sample.shape[1])
    win, wout = w_ffn_in.astype(BF16), w_ffn_out.astype(BF16)
    for i in range(DEPTH):
        kind, j = i % N_MIXERS, i // N_MIXERS
        row = lambda v: v.reshape(1, -1)
        if kind == 0:
            wp, ps = w_pool[j].astype(BF16), row(pool_scale[j])
        elif kind == 1:
            mla_w = [_mla_weights(w_dq[j], w_uq[j], w_dkv[j], w_uk[j], w_uv[j], w_mla_o[j], rope)
                     for rope in (False, True)]
            gq, gkv = row(g_q[j]), row(g_kv[j])
        else:
            w_in = w_lru_in[j].astype(BF16)
            wg, bg = _lru_gate_weights(w_lru_a[j], b_lru_a[j], w_lru_i[j], b_lru_i[j])
            w_out = w_lru_out[j].astype(BF16)
        for p in range(2):
            x, m = xs[p], _Cond(mod, lng, lnb, i, p == 1)
            if kind == 0:
                x = _pool_layer(x, m, wp, ps)
            elif kind == 1:
                wdq, wq, wdkv, wuk, wuv, wo, wq_rot = mla_w[p]
                if p == 0:
                    qcat, kcat, v, ckv, kr64 = _mla_project(
                        x, m, wdq, gq, wq, wdkv, gkv, wuk, wuv, None, True)
                    ckv_out.append(ckv)
                    krope_out.append(kr64)
                    segments = [(kcat, v)]
                else:
                    qcat, kcat, v = _mla_project(
                        x, m, wdq, gq, wq, wdkv, gkv, wuk, wuv, (wq_rot, cos, sin), False)
                    kr_c = cache_mla_krope[:, j]
                    kr_c = jnp.concatenate([kr_c, kr_c], axis=-1).astype(BF16)
                    kcat_c, v_c = _kv_up(cache_mla_ckv[:, j], kr_c, wuk, wuv)
                    segments = [(kcat_c, v_c), (kcat, v)]
                x = _attn_layer(x, m, qcat, wo, segments)
            else:
                if p == 0:
                    h0 = jnp.zeros((x.shape[0], 2, D_RNN), F32)
                else:
                    h0 = state_lru[:, j]
                x, st = _lru_layer(x, m, w_in, lru_conv_w[j], row(lru_conv_b[j]),
                                   wg, bg, lru_lambda[j], h0, w_out)
                if p == 0:
                    lru_out.append(st)
            xs[p] = _ffn_layer(x, m, win, wout)
    return (xs[0], xs[1], jnp.stack(ckv_out, axis=1), jnp.stack(krope_out, axis=1),
            jnp.stack(lru_out, axis=1))
```

```python
import functools
import math
from typing import NamedTuple

import jax
import jax.numpy as jnp
from jax import lax
from jax.experimental import pallas as pl
from jax.experimental.pallas import tpu as pltpu

F32 = jnp.float32
BF16 = jnp.bfloat16

D_MODEL = 1024
DEPTH = 4
GRID_W = 64
N_MIXERS = 3
POOL_WINDOWS = (2, 4, 8, 16)
POOL_GROUPS = 4
POOL_GC = D_MODEL // POOL_GROUPS
MLA_HEADS = 8
Q_LORA = 384
KV_LORA = 256
QK_NOPE = 128
QK_ROPE = 64
V_HEAD = 128
ROPE_NF = QK_ROPE // 4
ROPE_THETA = 10000.0
MLA_SCALE = (QK_NOPE + QK_ROPE) ** -0.5
D_RNN = D_MODEL
LRU_BLOCKS = 8
LRU_BW = D_RNN // LRU_BLOCKS
CONV_W = 4
CONV_LEFT = 1
LRU_C = 8.0
D_FF = ((8 * D_MODEL // 3 + 255) // 256) * 256
ALPHA = (2.0 * DEPTH) ** 0.25
EPS = 1e-6

LANES = 128
SUBLANES = 8
VMEM_LIMIT_BYTES = 56 * 1024 * 1024
PAD_ROWS = SUBLANES
MOD_ROWS = 16
MOD_CHUNKS = 2
LRU_COLS = 512
SEG_PITCH_PAD = 4
SHORT_SEQ_ROWS = 512
PROJ_TILE = 512
FFN_TILE = 512
ATTN_TILE = 512
QK_CAT = 2 * LANES


def _params(*sem):
    return pltpu.CompilerParams(dimension_semantics=sem, vmem_limit_bytes=VMEM_LIMIT_BYTES)


def _dot(a, b):
    return jnp.dot(a, b, preferred_element_type=F32)


def _layer_norm(y, g, b):
    mu = jnp.mean(y, axis=-1, keepdims=True)
    d = y - mu
    var = jnp.mean(d * d, axis=-1, keepdims=True)
    return d * lax.rsqrt(var + EPS) * g + b


def _gelu_tanh(x):
    c = math.sqrt(2.0 / math.pi)
    return (0.5 * x) * (1.0 + jnp.tanh(x * (c + (c * 0.044715) * (x * x))))


def _rms_norm(y, g):
    return y * lax.rsqrt(jnp.mean(y * y, axis=-1, keepdims=True) + EPS) * g


def _modulate(x, mod_ref):
    return x * (1.0 + mod_ref[1, 0]) + mod_ref[0, 0]


def _finish(x, mixed, mod_ref, lng_ref, lnb_ref):
    return _layer_norm(ALPHA * x + mod_ref[2, 0] * mixed, lng_ref[0], lnb_ref[0])


class _Cond(NamedTuple):
    mod: jax.Array
    lng: jax.Array
    lnb: jax.Array
    layer: int
    latent: bool

    @property
    def arrays(self):
        return self.mod, self.lng, self.lnb


def _cond_specs(cond, sub, nargs):
    k = 2 * cond.layer + sub
    row = (lambda b: 1 + b) if cond.latent else (lambda b: 0)
    if nargs == 1:
        mod = pl.BlockSpec((3, 1, 1, D_MODEL), lambda b: (k, row(b), 0, 0))
        ln = pl.BlockSpec((1, 1, D_MODEL), lambda b: (k, 0, 0))
    else:
        mod = pl.BlockSpec((3, 1, 1, D_MODEL), lambda b, t: (k, row(b), 0, 0))
        ln = pl.BlockSpec((1, 1, D_MODEL), lambda b, t: (k, 0, 0))
    return [mod, ln, ln]


def _const_spec(shape, nargs):
    zeros = (0,) * len(shape)
    if nargs == 1:
        return pl.BlockSpec(shape, lambda b: zeros)
    return pl.BlockSpec(shape, lambda b, t: zeros)


def _mod_kernel(cond_ref, w_ref, b_ref, o_ref):
    s = jax.nn.silu(cond_ref[...]).astype(BF16)
    for k in range(o_ref.shape[1]):
        cols = slice(k * D_MODEL, (k + 1) * D_MODEL)
        res = _dot(s, w_ref[0, :, cols].astype(BF16)) + b_ref[0, k]
        for r in range(MOD_ROWS):
            o_ref[0, k, r] = res[r:r + 1]


def _modulation(cond, w_ada, b_ada):
    d = D_MODEL
    nk = MOD_CHUNKS
    return pl.pallas_call(
        _mod_kernel,
        grid=(DEPTH, 6 // nk),
        in_specs=[
            pl.BlockSpec((MOD_ROWS, d), lambda i, j: (0, 0)),
            pl.BlockSpec((1, d, nk * d), lambda i, j: (i, 0, j)),
            pl.BlockSpec((1, nk, 1, d), lambda i, j: (i, j, 0, 0)),
        ],
        out_specs=pl.BlockSpec((1, nk, MOD_ROWS, 1, d), lambda i, j: (i, j, 0, 0, 0)),
        out_shape=jax.ShapeDtypeStruct((DEPTH, 6, MOD_ROWS, 1, d), F32),
        compiler_params=_params("arbitrary", "arbitrary"),
        name="adaln_modulation",
    )(cond, w_ada, b_ada.reshape(DEPTH, 6, 1, d)).reshape(DEPTH * 6, MOD_ROWS, 1, d)


def _ffn_kernel(x_ref, mod_ref, lng_ref, lnb_ref, win_ref, wout_ref, o_ref):
    x = x_ref[0]
    h = _modulate(x, mod_ref).astype(BF16)
    ab = _dot(h, win_ref[0])
    u = (jax.nn.silu(ab[:, :D_FF]) * ab[:, D_FF:]).astype(BF16)
    o_ref[0] = _finish(x, _dot(u, wout_ref[0]), mod_ref, lng_ref, lnb_ref)


def _ffn_layer(x, cond, win, wout):
    layer = cond.layer
    shape = x.shape
    d = shape[-1]
    if not cond.latent and shape[1] < FFN_TILE:
        x = x.reshape(-1, FFN_TILE, d)
    b, t, _ = x.shape
    tm = min(FFN_TILE, t)
    slab = lambda r, c: pl.BlockSpec((1, r, c), lambda i, j: (layer, 0, 0), pipeline_mode=pl.Buffered(1))
    return pl.pallas_call(
        _ffn_kernel,
        grid=(b, t // tm),
        in_specs=[
            pl.BlockSpec((1, tm, d), lambda i, j: (i, j, 0)),
            *_cond_specs(cond, 1, 2),
            slab(d, 2 * D_FF),
            slab(D_FF, d),
        ],
        out_specs=pl.BlockSpec((1, tm, d), lambda i, j: (i, j, 0)),
        out_shape=jax.ShapeDtypeStruct(x.shape, x.dtype),
        compiler_params=_params("arbitrary", "arbitrary"),
        name="ffn_layer",
    )(x, *cond.arrays, win, wout).reshape(shape)


def _pool_kernel(x_ref, mod_ref, lng_ref, lnb_ref, wp_ref, ps_ref, o_ref):
    for q in range(x_ref.shape[0]):
        o_ref[q] = _pool_sequence(x_ref[q], mod_ref, lng_ref, lnb_ref, wp_ref, ps_ref)


def _pool_sequence(x, mod_ref, lng_ref, lnb_ref, wp_ref, ps_ref):
    t_len = x.shape[0]
    n_pad = t_len + 2 * PAD_ROWS
    h = _modulate(x, mod_ref)
    edge = jnp.zeros((PAD_ROWS, POOL_GC), F32)
    edge_row = lax.broadcasted_iota(jnp.int32, (PAD_ROWS, 1), 0)
    outs = []
    for gi, w in enumerate(POOL_WINDOWS):
        assert w & (w - 1) == 0 and w // 2 <= PAD_ROWS
        hg = h[:, gi * POOL_GC:(gi + 1) * POOL_GC]
        run = jnp.concatenate([edge, hg, edge], axis=0)
        span = 1
        while 2 * span < w:
            run = run + pltpu.roll(run, n_pad - span, axis=0)
            span *= 2
        s = (run + pltpu.roll(run, span, axis=0))[PAD_ROWS:PAD_ROWS + t_len]
        lo, hi = -(w // 2), w - w // 2
        parts = []
        for r0 in (0, t_len - PAD_ROWS):
            t = r0 + edge_row
            cnt = (jnp.minimum(t + hi, t_len) - jnp.maximum(t + lo, 0)).astype(F32)
            parts.append(s[r0:r0 + PAD_ROWS] / cnt - hg[r0:r0 + PAD_ROWS])
        mid = s[PAD_ROWS:t_len - PAD_ROWS] * (1.0 / w) - hg[PAD_ROWS:t_len - PAD_ROWS]
        pooled = jnp.concatenate([parts[0], mid, parts[1]], axis=0)
        outs.append(_dot(pooled.astype(BF16), wp_ref[gi]))
    mixed = jnp.concatenate(outs, axis=1) * ps_ref[...]
    return _finish(x, mixed, mod_ref, lng_ref, lnb_ref)


def _pool_layer(x, cond, wp, ps):
    b, t, d = x.shape
    nq = 1 if cond.latent else max(1, SHORT_SEQ_ROWS // t)
    return pl.pallas_call(
        _pool_kernel,
        grid=(b // nq,),
        in_specs=[
            pl.BlockSpec((nq, t, d), lambda i: (i, 0, 0)),
            *_cond_specs(cond, 0, 1),
            _const_spec((POOL_GROUPS, POOL_GC, POOL_GC), 1),
            _const_spec((1, d), 1),
        ],
        out_specs=pl.BlockSpec((nq, t, d), lambda i: (i, 0, 0)),
        out_shape=jax.ShapeDtypeStruct(x.shape, x.dtype),
        compiler_params=_params("arbitrary"),
        name="pool_layer",
    )(x, *cond.arrays, wp, ps)


def _store_k_cat(kcat_ref, k_nope, k_rope2):
    for hd in range(MLA_HEADS):
        kcat_ref[0, :, hd * QK_CAT:hd * QK_CAT + QK_NOPE] = k_nope[:, hd * QK_NOPE:(hd + 1) * QK_NOPE]
        kcat_ref[0, :, hd * QK_CAT + QK_NOPE:(hd + 1) * QK_CAT] = k_rope2


def _mla_proj_kernel(*refs, rope, emit_cache):
    x_ref, mod_ref, wdq_ref, gq_ref, wq_ref, wdkv_ref, gkv_ref, wuk_ref, wuv_ref = refs[:9]
    refs = refs[9:]
    if rope:
        wqrot_ref, cos_ref, sin_ref = refs[:3]
        refs = refs[3:]
    qcat_ref, kcat_ref, v_ref = refs[:3]
    h = _modulate(x_ref[0], mod_ref).astype(BF16)
    q_lat = _rms_norm(_dot(h, wdq_ref[...]), gq_ref[...]).astype(BF16)
    nope_w = MLA_HEADS * QK_NOPE
    q = _dot(q_lat, wq_ref[...])
    kv = _dot(h, wdkv_ref[...])
    k_rope2 = kv[:, KV_LORA:KV_LORA + 2 * QK_ROPE]
    if rope:
        cos, sin = cos_ref[...], sin_ref[...]
        q_rot = _dot(q_lat, wqrot_ref[...])
        k_rope2 = k_rope2 * cos + kv[:, KV_LORA + 2 * QK_ROPE:] * sin
    lane = lax.broadcasted_iota(jnp.int32, (1, LANES), 1)
    for pair in range(MLA_HEADS // 2):
        both = q[:, nope_w + pair * LANES:nope_w + (pair + 1) * LANES]
        if rope:
            both = both * cos + q_rot[:, pair * LANES:(pair + 1) * LANES] * sin
        for hd, keep in ((2 * pair, lane < QK_ROPE), (2 * pair + 1, lane >= QK_ROPE)):
            c0 = hd * QK_CAT
            qcat_ref[0, :, c0:c0 + QK_NOPE] = q[:, hd * QK_NOPE:(hd + 1) * QK_NOPE].astype(BF16)
            qcat_ref[0, :, c0 + QK_NOPE:c0 + QK_CAT] = jnp.where(keep, both, 0.0).astype(BF16)
    c_kv = _rms_norm(kv[:, :KV_LORA], gkv_ref[...])
    c_kv16 = c_kv.astype(BF16)
    _store_k_cat(kcat_ref, _dot(c_kv16, wuk_ref[...]).astype(BF16), k_rope2.astype(BF16))
    v_ref[0] = _dot(c_kv16, wuv_ref[...]).astype(BF16)
    if emit_cache:
        ckv_out_ref, kr_out_ref = refs[3:5]
        ckv_out_ref[0] = c_kv
        kr_out_ref[0] = kv[:, KV_LORA:KV_LORA + QK_ROPE]


def _mla_project(x, cond, wdq, gq, wq, wdkv, gkv, wuk, wuv, rope_args, emit_cache):
    b0, t0, d = x.shape
    rope = rope_args is not None
    if not cond.latent and not rope and t0 < PROJ_TILE:
        x = x.reshape(-1, PROJ_TILE, d)
    b, t, _ = x.shape
    tm = min(PROJ_TILE, t)
    tok = lambda w: pl.BlockSpec((1, tm, w), lambda i, j: (i, j, 0))
    args = [x, cond.mod, wdq, gq, wq, wdkv, gkv, wuk, wuv]
    in_specs = [tok(d), _cond_specs(cond, 0, 2)[0]] + [_const_spec(a.shape, 2) for a in args[2:]]
    if rope:
        wqrot, cos, sin = rope_args
        args += [wqrot, cos, sin]
        in_specs += [_const_spec(wqrot.shape, 2)] + [pl.BlockSpec((tm, LANES), lambda i, j: (j, 0))] * 2
    widths = [MLA_HEADS * QK_CAT, MLA_HEADS * QK_CAT, MLA_HEADS * V_HEAD]
    out_specs = [tok(w) for w in widths]
    out_shape = [jax.ShapeDtypeStruct((b, t, w), BF16) for w in widths]
    if emit_cache:
        out_specs += [tok(KV_LORA), tok(QK_ROPE)]
        out_shape += [jax.ShapeDtypeStruct((b, t, KV_LORA), F32),
                      jax.ShapeDtypeStruct((b, t, QK_ROPE), F32)]
    outs = pl.pallas_call(
        functools.partial(_mla_proj_kernel, rope=rope, emit_cache=emit_cache),
        grid=(b, t // tm),
        in_specs=in_specs,
        out_specs=out_specs,
        out_shape=out_shape,
        compiler_params=_params("arbitrary", "arbitrary"),
        name="mla_project",
    )(*args)
    return [o.reshape(b0, t0, o.shape[-1]) for o in outs]


def _kv_up_kernel(ckv_ref, kr2_ref, wuk_ref, wuv_ref, kcat_ref, v_ref):
    c = ckv_ref[0].astype(BF16)
    _store_k_cat(kcat_ref, _dot(c, wuk_ref[...]).astype(BF16), kr2_ref[0])
    v_ref[0] = _dot(c, wuv_ref[...]).astype(BF16)


def _kv_up(ckv, kr2, wuk, wuv):
    b, s, r = ckv.shape
    seq = lambda w: pl.BlockSpec((1, s, w), lambda i: (i, 0, 0))
    widths = [MLA_HEADS * QK_CAT, MLA_HEADS * V_HEAD]
    return pl.pallas_call(
        _kv_up_kernel,
        grid=(b,),
        in_specs=[seq(r), seq(kr2.shape[2]), _const_spec(wuk.shape, 1), _const_spec(wuv.shape, 1)],
        out_specs=[seq(w) for w in widths],
        out_shape=[jax.ShapeDtypeStruct((b, s, w), BF16) for w in widths],
        compiler_params=_params("arbitrary"),
        name="mla_cache_kv_up",
    )(ckv, kr2, wuk, wuv)


def _attn_kernel(*refs, n_seg):
    x_ref, mod_ref, lng_ref, lnb_ref, q_ref, wo_ref = refs[:6]
    seg_refs = [refs[6 + 2 * i:8 + 2 * i] for i in range(n_seg)]
    o_ref = refs[6 + 2 * n_seg]
    nt = (((1,), (1,)), ((), ()))
    exp2_scale = MLA_SCALE * math.log2(math.e)
    for sq in range(x_ref.shape[0]):
        heads = []
        for hd in range(MLA_HEADS):
            q = q_ref[sq, :, hd * QK_CAT:(hd + 1) * QK_CAT]
            scores = [lax.dot_general(q, k_ref[sq, :, hd * QK_CAT:(hd + 1) * QK_CAT], nt,
                                      preferred_element_type=F32) for k_ref, _ in seg_refs]
            m = functools.reduce(jnp.maximum, [jnp.max(s, axis=-1, keepdims=True) for s in scores])
            es = [jnp.exp2((s - m) * exp2_scale) for s in scores]
            total = functools.reduce(lambda a, b: a + b, [jnp.sum(e, axis=-1, keepdims=True) for e in es])
            o = None
            for e, (_, v_ref) in zip(es, seg_refs):
                part = _dot(e.astype(BF16), v_ref[sq, :, hd * V_HEAD:(hd + 1) * V_HEAD])
                o = part if o is None else o + part
            heads.append((o * (1.0 / total)).astype(BF16))
        mixed = _dot(jnp.concatenate(heads, axis=1), wo_ref[...])
        o_ref[sq] = _finish(x_ref[sq], mixed, mod_ref, lng_ref, lnb_ref)


def _attn_layer(x, cond, qcat, wo, segments):
    b, t, d = x.shape
    tq = min(ATTN_TILE, t)
    nq = 1 if cond.latent else max(1, SHORT_SEQ_ROWS // t)
    tok = lambda w: pl.BlockSpec((nq, tq, w), lambda i, j: (i, j, 0))
    in_specs = [tok(d), *_cond_specs(cond, 0, 2),
                tok(qcat.shape[2]), _const_spec(wo.shape, 2)]
    args = [x, *cond.arrays, qcat, wo]
    for seg in segments:
        for a in seg:
            in_specs.append(pl.BlockSpec((nq,) + a.shape[1:], lambda i, j: (i, 0, 0)))
            args.append(a)
    return pl.pallas_call(
        functools.partial(_attn_kernel, n_seg=len(segments)),
        grid=(b // nq, t // tq),
        in_specs=in_specs,
        out_specs=tok(d),
        out_shape=jax.ShapeDtypeStruct(x.shape, x.dtype),
        compiler_params=_params("arbitrary", "arbitrary"),
        name="mla_attention",
    )(*args)


def _lru_kernel(x_ref, mod_ref, lng_ref, lnb_ref, wu_ref, wy_ref, cw_ref, cb_ref, wg_ref, bg_ref,
                lam_ref, h0_ref, wout_ref, o_ref, st_ref,
                perm_scr, h_scr, acc_scr, pad_scr, af_scr, bf_scr, ab_scr, bb_scr):
    t_len = x_ref.shape[1]
    n_blk = t_len // SUBLANES
    pitch = n_blk + SEG_PITCH_PAD
    n_slab = D_MODEL // LANES
    cols = wu_ref.shape[1]
    j = pl.program_id(1)
    slab = lambda c: slice(c * LANES, (c + 1) * LANES)

    @pl.when(j == 0)
    def _():
        for c in range(n_slab):
            for s in range(SUBLANES):
                perm_scr[c, s * pitch:s * pitch + n_blk, :] = x_ref[0, s * n_blk:(s + 1) * n_blk, slab(c)]

        def gather(k, carry):
            r = pl.multiple_of(k * SUBLANES, SUBLANES)
            for c in range(n_slab):
                acc_scr[pl.ds(r, SUBLANES), slab(c)] = perm_scr[c, pl.ds(k, SUBLANES, stride=pitch), :]
            return carry

        lax.fori_loop(0, n_blk, gather, 0, unroll=2)
        h_scr[...] = _modulate(acc_scr[...], mod_ref).astype(BF16)
        acc_scr[...] = jnp.zeros_like(acc_scr)

    h = h_scr[...]
    u = _dot(h, wu_ref[...])
    y = _gelu_tanh(_dot(h, wy_ref[...]))

    sub = lax.broadcasted_iota(jnp.int32, (SUBLANES, cols), 0)
    lead = CONV_LEFT * SUBLANES
    for i in range(1, CONV_LEFT + 1):
        blk = u[(n_blk - i) * SUBLANES:(n_blk - i + 1) * SUBLANES]
        pad_scr[lead - i * SUBLANES:lead - (i - 1) * SUBLANES, :] = jnp.where(
            sub >= 1, pltpu.roll(blk, 1, axis=0), 0.0)
    pad_scr[lead:lead + t_len, :] = u
    for i in range(CONV_W - 1 - CONV_LEFT):
        blk = u[i * SUBLANES:(i + 1) * SUBLANES]
        pad_scr[lead + t_len + i * SUBLANES:lead + t_len + (i + 1) * SUBLANES, :] = jnp.where(
            sub < SUBLANES - 1, pltpu.roll(blk, SUBLANES - 1, axis=0), 0.0)
    uc = None
    for k in range(CONV_W):
        term = pad_scr[k * SUBLANES:k * SUBLANES + t_len, :] * cw_ref[k:k + 1, :]
        uc = term if uc is None else uc + term
    uc = uc + cb_ref[...]

    a_scr = (af_scr, ab_scr)
    b_scr = (bf_scr, bb_scr)
    for bl in range(cols // LRU_BW):
        c0 = bl * LRU_BW
        ub = uc[:, c0:c0 + LRU_BW]
        tg = jnp.tanh(_dot(ub.astype(BF16), wg_ref[bl]) + bg_ref[bl])
        for dr in range(2):
            tr = tg[:, dr * LRU_BW:(dr + 1) * LRU_BW]
            ti = tg[:, (2 + dr) * LRU_BW:(3 + dr) * LRU_BW]
            half_sp = (-0.25 * LRU_C) * jax.nn.softplus(-lam_ref[dr:dr + 1, c0:c0 + LRU_BW])
            t = jnp.tanh(half_sp * (1.0 + tr))
            inv = 1.0 / (1.0 - t)
            nt = -t
            root = jnp.where(nt > 0.0, nt * lax.rsqrt(nt), 0.0)
            a_scr[dr][:, c0:c0 + LRU_BW] = (1.0 + t) * inv
            b_scr[dr][:, c0:c0 + LRU_BW] = (root * inv) * ((1.0 + ti) * ub)

    def step(k, carry):
        a_f, b_f, a_b, b_b = carry
        rf = pl.multiple_of(k * SUBLANES, SUBLANES)
        rb = pl.multiple_of((n_blk - 1 - k) * SUBLANES, SUBLANES)
        a = af_scr[pl.ds(rf, SUBLANES), :]
        b_f = a * b_f + bf_scr[pl.ds(rf, SUBLANES), :]
        a_f = a * a_f
        af_scr[pl.ds(rf, SUBLANES), :] = a_f
        bf_scr[pl.ds(rf, SUBLANES), :] = b_f
        a = ab_scr[pl.ds(rb, SUBLANES), :]
        b_b = a * b_b + bb_scr[pl.ds(rb, SUBLANES), :]
        a_b = a * a_b
        ab_scr[pl.ds(rb, SUBLANES), :] = a_b
        bb_scr[pl.ds(rb, SUBLANES), :] = b_b
        return a_f, b_f, a_b, b_b

    one = jnp.ones((SUBLANES, cols), F32)
    zero = jnp.zeros((SUBLANES, cols), F32)
    a_f, b_f, a_b, b_b = lax.fori_loop(0, n_blk, step, (one, zero, one, zero), unroll=4)

    hf = h0_ref[0, 0:1, :]
    hf_in = zero
    for s in range(SUBLANES):
        hf_in = jnp.where(sub == s, hf, hf_in)
        hf = a_f[s:s + 1] * hf + b_f[s:s + 1]
    hb = h0_ref[0, 1:2, :]
    hb_in = zero
    for s in reversed(range(SUBLANES)):
        hb_in = jnp.where(sub == s, hb, hb_in)
        hb = a_b[s:s + 1] * hb + b_b[s:s + 1]
    st_ref[0, 0:1, :] = hf
    st_ref[0, 1:2, :] = hb

    blocks = lambda ref: ref[...].reshape(n_blk, SUBLANES, cols)
    h_sum = (blocks(af_scr) * hf_in + blocks(bf_scr)) + (blocks(ab_scr) * hb_in + blocks(bb_scr))
    mixed = (h_sum.reshape(t_len, cols) * y).astype(BF16)
    acc_scr[...] += _dot(mixed, wout_ref[...])

    @pl.when(j == pl.num_programs(1) - 1)
    def _():
        def scatter(k, carry):
            r = pl.multiple_of(k * SUBLANES, SUBLANES)
            for c in range(n_slab):
                perm_scr[c, pl.ds(k, SUBLANES, stride=pitch), :] = acc_scr[pl.ds(r, SUBLANES), slab(c)]
            return carry

        lax.fori_loop(0, n_blk, scatter, 0, unroll=2)
        for s in range(SUBLANES):
            rows = slice(s * n_blk, (s + 1) * n_blk)
            mixed_s = jnp.concatenate(
                [perm_scr[c, s * pitch:s * pitch + n_blk, :] for c in range(n_slab)], axis=1)
            o_ref[0, rows, :] = _finish(x_ref[0, rows, :], mixed_s, mod_ref, lng_ref, lnb_ref)


def _lru_layer(x, cond, w_in, conv_w, conv_b, wg, bg, lam, h0, w_out):
    b, t, d = x.shape
    cw = LRU_COLS if t > SHORT_SEQ_ROWS else D_RNN
    ncb = D_RNN // cw
    bpc = cw // LRU_BW
    pitch = t // SUBLANES + SEG_PITCH_PAD
    seq = lambda w: pl.BlockSpec((1, t, w), lambda i, j: (i, 0, 0))
    col = lambda rows: pl.BlockSpec((rows, cw), lambda i, j: (0, j))
    out, state = pl.pallas_call(
        _lru_kernel,
        grid=(b, ncb),
        in_specs=[
            seq(d),
            *_cond_specs(cond, 0, 2),
            pl.BlockSpec((d, cw), lambda i, j: (0, j)),
            pl.BlockSpec((d, cw), lambda i, j: (0, ncb + j)),
            col(CONV_W),
            col(1),
            pl.BlockSpec((bpc, LRU_BW, 4 * LRU_BW), lambda i, j: (j, 0, 0)),
            pl.BlockSpec((bpc, 1, 4 * LRU_BW), lambda i, j: (j, 0, 0)),
            col(2),
            pl.BlockSpec((1, 2, cw), lambda i, j: (i, 0, j)),
            pl.BlockSpec((cw, d), lambda i, j: (j, 0)),
        ],
        out_specs=[seq(d), pl.BlockSpec((1, 2, cw), lambda i, j: (i, 0, j))],
        out_shape=[jax.ShapeDtypeStruct(x.shape, x.dtype),
                   jax.ShapeDtypeStruct((b, 2, D_RNN), x.dtype)],
        scratch_shapes=[pltpu.VMEM((d // LANES, SUBLANES * pitch, LANES), F32),
                        pltpu.VMEM((t, d), BF16), pltpu.VMEM((t, d), F32),
                        pltpu.VMEM((t + (CONV_W - 1) * SUBLANES, cw), F32)]
                       + [pltpu.VMEM((t, cw), F32)] * 4,
        compiler_params=_params("arbitrary", "arbitrary"),
        name="lru_layer",
    )(x, *cond.arrays, w_in, w_in, conv_w, conv_b, wg, bg, lam, h0, w_out)
    return out, state


def _rope_rotate_cols(w):
    nf = ROPE_NF
    return jnp.concatenate([-w[..., nf:2 * nf], w[..., :nf], -w[..., 3 * nf:], w[..., 2 * nf:3 * nf]], axis=-1)


def _rope_tables(t_len):
    t = jnp.arange(t_len)
    rows = (t // GRID_W).astype(F32)
    cols = (t % GRID_W).astype(F32)
    inv = ROPE_THETA ** (-jnp.arange(ROPE_NF, dtype=F32) / ROPE_NF)
    ar, ac = rows[:, None] * inv, cols[:, None] * inv
    cos = jnp.concatenate([jnp.cos(ar), jnp.cos(ar), jnp.cos(ac), jnp.cos(ac)], axis=1)
    sin = jnp.concatenate([jnp.sin(ar), jnp.sin(ar), jnp.sin(ac), jnp.sin(ac)], axis=1)
    return jnp.tile(cos, (1, 2)), jnp.tile(sin, (1, 2))


def _mla_weights(w_dq, w_uq, w_dkv, w_uk, w_uv, w_o, rope):
    r = w_uq.shape[0]
    w_nope, w_rope = w_uq[:, :, :QK_NOPE], w_uq[:, :, QK_NOPE:]
    wq = jnp.concatenate([w_nope.reshape(r, -1), w_rope.reshape(r, -1)], axis=1).astype(BF16)
    wk = w_dkv[:, KV_LORA:]
    wdkv = [w_dkv[:, :KV_LORA], wk, wk]
    wq_rot = None
    if rope:
        wq_rot = _rope_rotate_cols(w_rope).reshape(r, -1).astype(BF16)
        wk_rot = _rope_rotate_cols(wk)
        wdkv += [wk_rot, wk_rot]
    return (w_dq.astype(BF16), wq, jnp.concatenate(wdkv, axis=1).astype(BF16),
            w_uk.reshape(KV_LORA, -1).astype(BF16), w_uv.reshape(KV_LORA, -1).astype(BF16),
            w_o.astype(BF16), wq_rot)


def _lru_gate_weights(w_a, b_a, w_i, b_i):
    wg = (0.5 * jnp.concatenate([w_a[0], w_a[1], w_i[0], w_i[1]], axis=-1)).astype(BF16)
    blk = lambda v: v.reshape(LRU_BLOCKS, 1, LRU_BW)
    bg = 0.5 * jnp.concatenate([blk(b_a[0]), blk(b_a[1]), blk(b_i[0]), blk(b_i[1])], axis=-1)
    return wg, bg


def kernel(x_prompt, x_sample, cache_mla_ckv, cache_mla_krope, state_lru, c, c_ctx, w_ada, b_ada, ln_g, ln_b, w_ffn_in, w_ffn_out, w_pool, pool_scale, w_dq, g_q, w_uq, w_dkv, g_kv, w_uk, w_uv, w_mla_o, w_lru_in, lru_conv_w, lru_conv_b, w_lru_a, b_lru_a, w_lru_i, b_lru_i, lru_lambda, w_lru_out):
    d = D_MODEL
    n_lat = c.shape[0]
    cond = jnp.concatenate([c_ctx[None], c, jnp.zeros((MOD_ROWS - 1 - n_lat, d), F32)], axis=0)
    mod = _modulation(cond, w_ada, b_ada)
    lng, lnb = ln_g.reshape(DEPTH * 2, 1, d), ln_b.reshape(DEPTH * 2, 1, d)

    xs = [x_prompt, x_sample]
    ckv_out, krope_out, lru_out = [], [], []
    cos, sin = _rope_tables(x_sample.shape[1])
    win, wout = w_ffn_in.astype(BF16), w_ffn_out.astype(BF16)
    for i in range(DEPTH):
        kind, j = i % N_MIXERS, i // N_MIXERS
        row = lambda v: v.reshape(1, -1)
        if kind == 0:
            wp, ps = w_pool[j].astype(BF16), row(pool_scale[j])
        elif kind == 1:
            mla_w = [_mla_weights(w_dq[j], w_uq[j], w_dkv[j], w_uk[j], w_uv[j], w_mla_o[j], rope)
                     for rope in (False, True)]
            gq, gkv = row(g_q[j]), row(g_kv[j])
        else:
            w_in = w_lru_in[j].astype(BF16)
            wg, bg = _lru_gate_weights(w_lru_a[j], b_lru_a[j], w_lru_i[j], b_lru_i[j])
            w_out = w_lru_out[j].astype(BF16)
        for p in range(2):
            x, m = xs[p], _Cond(mod, lng, lnb, i, p == 1)
            if kind == 0:
                x = _pool_layer(x, m, wp, ps)
            elif kind == 1:
                wdq, wq, wdkv, wuk, wuv, wo, wq_rot = mla_w[p]
                if p == 0:
                    qcat, kcat, v, ckv, kr64 = _mla_project(
                        x, m, wdq, gq, wq, wdkv, gkv, wuk, wuv, None, True)
                    ckv_out.append(ckv)
                    krope_out.append(kr64)
                    segments = [(kcat, v)]
                else:
                    qcat, kcat, v = _mla_project(
                        x, m, wdq, gq, wq, wdkv, gkv, wuk, wuv, (wq_rot, cos, sin), False)
                    kr_c = cache_mla_krope[:, j]
                    kr_c = jnp.concatenate([kr_c, kr_c], axis=-1).astype(BF16)
                    kcat_c, v_c = _kv_up(cache_mla_ckv[:, j], kr_c, wuk, wuv)
                    segments = [(kcat_c, v_c), (kcat, v)]
                x = _attn_layer(x, m, qcat, wo, segments)
            else:
                if p == 0:
                    h0 = jnp.zeros((x.shape[0], 2, D_RNN), F32)
                else:
                    h0 = state_lru[:, j]
                x, st = _lru_layer(x, m, w_in, lru_conv_w[j], row(lru_conv_b[j]),
                                   wg, bg, lru_lambda[j], h0, w_out)
                if p == 0:
                    lru_out.append(st)
            xs[p] = _ffn_layer(x, m, win, wout)
    return (xs[0], xs[1], jnp.stack(ckv_out, axis=1), jnp.stack(krope_out, axis=1),
            jnp.stack(lru_out, axis=1))
```

```python
import functools
import math
from typing import NamedTuple

import jax
import jax.numpy as jnp
from jax import lax
from jax.experimental import pallas as pl
from jax.experimental.pallas import tpu as pltpu

F32 = jnp.float32
BF16 = jnp.bfloat16

D_MODEL = 1024
DEPTH = 4
GRID_W = 64
N_MIXERS = 3
POOL_WINDOWS = (2, 4, 8, 16)
POOL_GROUPS = 4
POOL_GC = D_MODEL // POOL_GROUPS
MLA_HEADS = 8
Q_LORA = 384
KV_LORA = 256
QK_NOPE = 128
QK_ROPE = 64
V_HEAD = 128
ROPE_NF = QK_ROPE // 4
ROPE_THETA = 10000.0
MLA_SCALE = (QK_NOPE + QK_ROPE) ** -0.5
D_RNN = D_MODEL
LRU_BLOCKS = 8
LRU_BW = D_RNN // LRU_BLOCKS
CONV_W = 4
CONV_LEFT = 1
LRU_C = 8.0
D_FF = ((8 * D_MODEL // 3 + 255) // 256) * 256
ALPHA = (2.0 * DEPTH) ** 0.25
EPS = 1e-6

LANES = 128
SUBLANES = 8
VMEM_LIMIT_BYTES = 56 * 1024 * 1024
PAD_ROWS = SUBLANES
MOD_ROWS = 16
MOD_CHUNKS = 2
LRU_COLS = 512
SEG_PITCH_PAD = 4
SHORT_SEQ_ROWS = 512
PROJ_TILE = 512
FFN_TILE = 512
ATTN_TILE = 512
QK_CAT = 2 * LANES


def _params(*sem):
    return pltpu.CompilerParams(dimension_semantics=sem, vmem_limit_bytes=VMEM_LIMIT_BYTES)


def _dot(a, b):
    return jnp.dot(a, b, preferred_element_type=F32)


def _layer_norm(y, g, b):
    mu = jnp.mean(y, axis=-1, keepdims=True)
    d = y - mu
    var = jnp.mean(d * d, axis=-1, keepdims=True)
    return d * lax.rsqrt(var + EPS) * g + b


def _gelu_tanh(x):
    c = math.sqrt(2.0 / math.pi)
    return (0.5 * x) * (1.0 + jnp.tanh(x * (c + (c * 0.044715) * (x * x))))


def _rms_norm(y, g):
    return y * lax.rsqrt(jnp.mean(y * y, axis=-1, keepdims=True) + EPS) * g


def _modulate(x, mod_ref):
    return x * (1.0 + mod_ref[1, 0]) + mod_ref[0, 0]


def _finish(x, mixed, mod_ref, lng_ref, lnb_ref):
    return _layer_norm(ALPHA * x + mod_ref[2, 0] * mixed, lng_ref[0], lnb_ref[0])


class _Cond(NamedTuple):
    mod: jax.Array
    lng: jax.Array
    lnb: jax.Array
    layer: int
    latent: bool

    @property
    def arrays(self):
        return self.mod, self.lng, self.lnb


def _cond_specs(cond, sub, nargs):
    k = 2 * cond.layer + sub
    row = (lambda b: 1 + b) if cond.latent else (lambda b: 0)
    if nargs == 1:
        mod = pl.BlockSpec((3, 1, 1, D_MODEL), lambda b: (k, row(b), 0, 0))
        ln = pl.BlockSpec((1, 1, D_MODEL), lambda b: (k, 0, 0))
    else:
        mod = pl.BlockSpec((3, 1, 1, D_MODEL), lambda b, t: (k, row(b), 0, 0))
        ln = pl.BlockSpec((1, 1, D_MODEL), lambda b, t: (k, 0, 0))
    return [mod, ln, ln]


def _const_spec(shape, nargs):
    zeros = (0,) * len(shape)
    if nargs == 1:
        return pl.BlockSpec(shape, lambda b: zeros)
    return pl.BlockSpec(shape, lambda b, t: zeros)


def _mod_kernel(cond_ref, w_ref, b_ref, o_ref):
    s = jax.nn.silu(cond_ref[...]).astype(BF16)
    for k in range(o_ref.shape[1]):
        cols = slice(k * D_MODEL, (k + 1) * D_MODEL)
        res = _dot(s, w_ref[0, :, cols].astype(BF16)) + b_ref[0, k]
        for r in range(MOD_ROWS):
            o_ref[0, k, r] = res[r:r + 1]


def _modulation(cond, w_ada, b_ada):
    d = D_MODEL
    nk = MOD_CHUNKS
    return pl.pallas_call(
        _mod_kernel,
        grid=(DEPTH, 6 // nk),
        in_specs=[
            pl.BlockSpec((MOD_ROWS, d), lambda i, j: (0, 0)),
            pl.BlockSpec((1, d, nk * d), lambda i, j: (i, 0, j)),
            pl.BlockSpec((1, nk, 1, d), lambda i, j: (i, j, 0, 0)),
        ],
        out_specs=pl.BlockSpec((1, nk, MOD_ROWS, 1, d), lambda i, j: (i, j, 0, 0, 0)),
        out_shape=jax.ShapeDtypeStruct((DEPTH, 6, MOD_ROWS, 1, d), F32),
        compiler_params=_params("arbitrary", "arbitrary"),
        name="adaln_modulation",
    )(cond, w_ada, b_ada.reshape(DEPTH, 6, 1, d)).reshape(DEPTH * 6, MOD_ROWS, 1, d)


def _ffn_kernel(x_ref, mod_ref, lng_ref, lnb_ref, win_ref, wout_ref, o_ref):
    x = x_ref[0]
    h = _modulate(x, mod_ref).astype(BF16)
    ab = _dot(h, win_ref[0])
    u = (jax.nn.silu(ab[:, :D_FF]) * ab[:, D_FF:]).astype(BF16)
    o_ref[0] = _finish(x, _dot(u, wout_ref[0]), mod_ref, lng_ref, lnb_ref)


def _ffn_layer(x, cond, win, wout):
    layer = cond.layer
    shape = x.shape
    d = shape[-1]
    if not cond.latent and shape[1] < FFN_TILE:
        x = x.reshape(-1, FFN_TILE, d)
    b, t, _ = x.shape
    tm = min(FFN_TILE, t)
    slab = lambda r, c: pl.BlockSpec((1, r, c), lambda i, j: (layer, 0, 0), pipeline_mode=pl.Buffered(1))
    return pl.pallas_call(
        _ffn_kernel,
        grid=(b, t // tm),
        in_specs=[
            pl.BlockSpec((1, tm, d), lambda i, j: (i, j, 0)),
            *_cond_specs(cond, 1, 2),
            slab(d, 2 * D_FF),
            slab(D_FF, d),
        ],
        out_specs=pl.BlockSpec((1, tm, d), lambda i, j: (i, j, 0)),
        out_shape=jax.ShapeDtypeStruct(x.shape, x.dtype),
        compiler_params=_params("arbitrary", "arbitrary"),
        name="ffn_layer",
    )(x, *cond.arrays, win, wout).reshape(shape)


def _pool_kernel(x_ref, mod_ref, lng_ref, lnb_ref, wp_ref, ps_ref, o_ref):
    for q in range(x_ref.shape[0]):
        o_ref[q] = _pool_sequence(x_ref[q], mod_ref, lng_ref, lnb_ref, wp_ref, ps_ref)


def _pool_sequence(x, mod_ref, lng_ref, lnb_ref, wp_ref, ps_ref):
    t_len = x.shape[0]
    n_pad = t_len + 2 * PAD_ROWS
    h = _modulate(x, mod_ref)
    edge = jnp.zeros((PAD_ROWS, POOL_GC), F32)
    edge_row = lax.broadcasted_iota(jnp.int32, (PAD_ROWS, 1), 0)
    outs = []
    for gi, w in enumerate(POOL_WINDOWS):
        assert w & (w - 1) == 0 and w // 2 <= PAD_ROWS
        hg = h[:, gi * POOL_GC:(gi + 1) * POOL_GC]
        run = jnp.concatenate([edge, hg, edge], axis=0)
        span = 1
        while 2 * span < w:
            run = run + pltpu.roll(run, n_pad - span, axis=0)
            span *= 2
        s = (run + pltpu.roll(run, span, axis=0))[PAD_ROWS:PAD_ROWS + t_len]
        lo, hi = -(w // 2), w - w // 2
        parts = []
        for r0 in (0, t_len - PAD_ROWS):
            t = r0 + edge_row
            cnt = (jnp.minimum(t + hi, t_len) - jnp.maximum(t + lo, 0)).astype(F32)
            parts.append(s[r0:r0 + PAD_ROWS] / cnt - hg[r0:r0 + PAD_ROWS])
        mid = s[PAD_ROWS:t_len - PAD_ROWS] * (1.0 / w) - hg[PAD_ROWS:t_len - PAD_ROWS]
        pooled = jnp.concatenate([parts[0], mid, parts[1]], axis=0)
        outs.append(_dot(pooled.astype(BF16), wp_ref[gi]))
    mixed = jnp.concatenate(outs, axis=1) * ps_ref[...]
    return _finish(x, mixed, mod_ref, lng_ref, lnb_ref)


def _pool_layer(x, cond, wp, ps):
    b, t, d = x.shape
    nq = 1 if cond.latent else max(1, SHORT_SEQ_ROWS // t)
    return pl.pallas_call(
        _pool_kernel,
        grid=(b // nq,),
        in_specs=[
            pl.BlockSpec((nq, t, d), lambda i: (i, 0, 0)),
            *_cond_specs(cond, 0, 1),
            _const_spec((POOL_GROUPS, POOL_GC, POOL_GC), 1),
            _const_spec((1, d), 1),
        ],
        out_specs=pl.BlockSpec((nq, t, d), lambda i: (i, 0, 0)),
        out_shape=jax.ShapeDtypeStruct(x.shape, x.dtype),
        compiler_params=_params("arbitrary"),
        name="pool_layer",
    )(x, *cond.arrays, wp, ps)


def _store_k_cat(kcat_ref, k_nope, k_rope2):
    for hd in range(MLA_HEADS):
        kcat_ref[0, :, hd * QK_CAT:hd * QK_CAT + QK_NOPE] = k_nope[:, hd * QK_NOPE:(hd + 1) * QK_NOPE]
        kcat_ref[0, :, hd * QK_CAT + QK_NOPE:(hd + 1) * QK_CAT] = k_rope2


def _mla_proj_kernel(*refs, rope, emit_cache):
    x_ref, mod_ref, wdq_ref, gq_ref, wq_ref, wdkv_ref, gkv_ref, wuk_ref, wuv_ref = refs[:9]
    refs = refs[9:]
    if rope:
        wqrot_ref, cos_ref, sin_ref = refs[:3]
        refs = refs[3:]
    qcat_ref, kcat_ref, v_ref = refs[:3]
    h = _modulate(x_ref[0], mod_ref).astype(BF16)
    q_lat = _rms_norm(_dot(h, wdq_ref[...]), gq_ref[...]).astype(BF16)
    nope_w = MLA_HEADS * QK_NOPE
    q = _dot(q_lat, wq_ref[...])
    kv = _dot(h, wdkv_ref[...])
    k_rope2 = kv[:, KV_LORA:KV_LORA + 2 * QK_ROPE]
    if rope:
        cos, sin = cos_ref[...], sin_ref[...]
        q_rot = _dot(q_lat, wqrot_ref[...])
        k_rope2 = k_rope2 * cos + kv[:, KV_LORA + 2 * QK_ROPE:] * sin
    lane = lax.broadcasted_iota(jnp.int32, (1, LANES), 1)
    for pair in range(MLA_HEADS // 2):
        both = q[:, nope_w + pair * LANES:nope_w + (pair + 1) * LANES]
        if rope:
            both = both * cos + q_rot[:, pair * LANES:(pair + 1) * LANES] * sin
        for hd, keep in ((2 * pair, lane < QK_ROPE), (2 * pair + 1, lane >= QK_ROPE)):
            c0 = hd * QK_CAT
            qcat_ref[0, :, c0:c0 + QK_NOPE] = q[:, hd * QK_NOPE:(hd + 1) * QK_NOPE].astype(BF16)
            qcat_ref[0, :, c0 + QK_NOPE:c0 + QK_CAT] = jnp.where(keep, both, 0.0).astype(BF16)
    c_kv = _rms_norm(kv[:, :KV_LORA], gkv_ref[...])
    c_kv16 = c_kv.astype(BF16)
    _store_k_cat(kcat_ref, _dot(c_kv16, wuk_ref[...]).astype(BF16), k_rope2.astype(BF16))
    v_ref[0] = _dot(c_kv16, wuv_ref[...]).astype(BF16)
    if emit_cache:
        ckv_out_ref, kr_out_ref = refs[3:5]
        ckv_out_ref[0] = c_kv
        kr_out_ref[0] = kv[:, KV_LORA:KV_LORA + QK_ROPE]


def _mla_project(x, cond, wdq, gq, wq, wdkv, gkv, wuk, wuv, rope_args, emit_cache):
    b0, t0, d = x.shape
    rope = rope_args is not None
    if not cond.latent and not rope and t0 < PROJ_TILE:
        x = x.reshape(-1, PROJ_TILE, d)
    b, t, _ = x.shape
    tm = min(PROJ_TILE, t)
    tok = lambda w: pl.BlockSpec((1, tm, w), lambda i, j: (i, j, 0))
    args = [x, cond.mod, wdq, gq, wq, wdkv, gkv, wuk, wuv]
    in_specs = [tok(d), _cond_specs(cond, 0, 2)[0]] + [_const_spec(a.shape, 2) for a in args[2:]]
    if rope:
        wqrot, cos, sin = rope_args
        args += [wqrot, cos, sin]
        in_specs += [_const_spec(wqrot.shape, 2)] + [pl.BlockSpec((tm, LANES), lambda i, j: (j, 0))] * 2
    widths = [MLA_HEADS * QK_CAT, MLA_HEADS * QK_CAT, MLA_HEADS * V_HEAD]
    out_specs = [tok(w) for w in widths]
    out_shape = [jax.ShapeDtypeStruct((b, t, w), BF16) for w in widths]
    if emit_cache:
        out_specs += [tok(KV_LORA), tok(QK_ROPE)]
        out_shape += [jax.ShapeDtypeStruct((b, t, KV_LORA), F32),
                      jax.ShapeDtypeStruct((b, t, QK_ROPE), F32)]
    outs = pl.pallas_call(
        functools.partial(_mla_proj_kernel, rope=rope, emit_cache=emit_cache),
        grid=(b, t // tm),
        in_specs=in_specs,
        out_specs=out_specs,
        out_shape=out_shape,
        compiler_params=_params("arbitrary", "arbitrary"),
        name="mla_project",
    )(*args)
    return [o.reshape(b0, t0, o.shape[-1]) for o in outs]


def _kv_up_kernel(ckv_ref, kr2_ref, wuk_ref, wuv_ref, kcat_ref, v_ref):
    c = ckv_ref[0].astype(BF16)
    _store_k_cat(kcat_ref, _dot(c, wuk_ref[...]).astype(BF16), kr2_ref[0])
    v_ref[0] = _dot(c, wuv_ref[...]).astype(BF16)


def _kv_up(ckv, kr2, wuk, wuv):
    b, s, r = ckv.shape
    seq = lambda w: pl.BlockSpec((1, s, w), lambda i: (i, 0, 0))
    widths = [MLA_HEADS * QK_CAT, MLA_HEADS * V_HEAD]
    return pl.pallas_call(
        _kv_up_kernel,
        grid=(b,),
        in_specs=[seq(r), seq(kr2.shape[2]), _const_spec(wuk.shape, 1), _const_spec(wuv.shape, 1)],
        out_specs=[seq(w) for w in widths],
        out_shape=[jax.ShapeDtypeStruct((b, s, w), BF16) for w in widths],
        compiler_params=_params("arbitrary"),
        name="mla_cache_kv_up",
    )(ckv, kr2, wuk, wuv)


def _attn_kernel(*refs, n_seg):
    x_ref, mod_ref, lng_ref, lnb_ref, q_ref, wo_ref = refs[:6]
    seg_refs = [refs[6 + 2 * i:8 + 2 * i] for i in range(n_seg)]
    o_ref = refs[6 + 2 * n_seg]
    nt = (((1,), (1,)), ((), ()))
    exp2_scale = MLA_SCALE * math.log2(math.e)
    for sq in range(x_ref.shape[0]):
        heads = []
        for hd in range(MLA_HEADS):
            q = q_ref[sq, :, hd * QK_CAT:(hd + 1) * QK_CAT]
            scores = [lax.dot_general(q, k_ref[sq, :, hd * QK_CAT:(hd + 1) * QK_CAT], nt,
                                      preferred_element_type=F32) for k_ref, _ in seg_refs]
            m = functools.reduce(jnp.maximum, [jnp.max(s, axis=-1, keepdims=True) for s in scores])
            es = [jnp.exp2((s - m) * exp2_scale) for s in scores]
            total = functools.reduce(lambda a, b: a + b, [jnp.sum(e, axis=-1, keepdims=True) for e in es])
            o = None
            for e, (_, v_ref) in zip(es, seg_refs):
                part = _dot(e.astype(BF16), v_ref[sq, :, hd * V_HEAD:(hd + 1) * V_HEAD])
                o = part if o is None else o + part
            heads.append((o * (1.0 / total)).astype(BF16))
        mixed = _dot(jnp.concatenate(heads, axis=1), wo_ref[...])
        o_ref[sq] = _finish(x_ref[sq], mixed, mod_ref, lng_ref, lnb_ref)


def _attn_layer(x, cond, qcat, wo, segments):
    b, t, d = x.shape
    tq = min(ATTN_TILE, t)
    nq = 1 if cond.latent else max(1, SHORT_SEQ_ROWS // t)
    tok = lambda w: pl.BlockSpec((nq, tq, w), lambda i, j: (i, j, 0))
    in_specs = [tok(d), *_cond_specs(cond, 0, 2),
                tok(qcat.shape[2]), _const_spec(wo.shape, 2)]
    args = [x, *cond.arrays, qcat, wo]
    for seg in segments:
        for a in seg:
            in_specs.append(pl.BlockSpec((nq,) + a.shape[1:], lambda i, j: (i, 0, 0)))
            args.append(a)
    return pl.pallas_call(
        functools.partial(_attn_kernel, n_seg=len(segments)),
        grid=(b // nq, t // tq),
        in_specs=in_specs,
        out_specs=tok(d),
        out_shape=jax.ShapeDtypeStruct(x.shape, x.dtype),
        compiler_params=_params("arbitrary", "arbitrary"),
        name="mla_attention",
    )(*args)


def _lru_kernel(x_ref, mod_ref, lng_ref, lnb_ref, wu_ref, wy_ref, cw_ref, cb_ref, wg_ref, bg_ref,
                lam_ref, h0_ref, wout_ref, o_ref, st_ref,
                perm_scr, h_scr, acc_scr, pad_scr, af_scr, bf_scr, ab_scr, bb_scr):
    t_len = x_ref.shape[1]
    n_blk = t_len // SUBLANES
    pitch = n_blk + SEG_PITCH_PAD
    n_slab = D_MODEL // LANES
    cols = wu_ref.shape[1]
    j = pl.program_id(1)
    slab = lambda c: slice(c * LANES, (c + 1) * LANES)

    @pl.when(j == 0)
    def _():
        for c in range(n_slab):
            for s in range(SUBLANES):
                perm_scr[c, s * pitch:s * pitch + n_blk, :] = x_ref[0, s * n_blk:(s + 1) * n_blk, slab(c)]

        scale, shift = 1.0 + mod_ref[1, 0], mod_ref[0, 0]
        pair = 2 * SUBLANES

        def gather(kp, carry):
            r = pl.multiple_of(kp * pair, pair)
            for c in range(n_slab):
                rows = jnp.concatenate(
                    [perm_scr[c, pl.ds(2 * kp + i, SUBLANES, stride=pitch), :] for i in range(2)], axis=0)
                h_scr[pl.ds(r, pair), slab(c)] = (rows * scale[:, slab(c)] + shift[:, slab(c)]).astype(BF16)
            return carry

        lax.fori_loop(0, n_blk // 2, gather, 0, unroll=2)
        acc_scr[...] = jnp.zeros_like(acc_scr)

    h = h_scr[...]
    u = _dot(h, wu_ref[...])
    y = _gelu_tanh(_dot(h, wy_ref[...]))

    sub = lax.broadcasted_iota(jnp.int32, (SUBLANES, cols), 0)
    lead = CONV_LEFT * SUBLANES
    for i in range(1, CONV_LEFT + 1):
        blk = u[(n_blk - i) * SUBLANES:(n_blk - i + 1) * SUBLANES]
        pad_scr[lead - i * SUBLANES:lead - (i - 1) * SUBLANES, :] = jnp.where(
            sub >= 1, pltpu.roll(blk, 1, axis=0), 0.0)
    pad_scr[lead:lead + t_len, :] = u
    for i in range(CONV_W - 1 - CONV_LEFT):
        blk = u[i * SUBLANES:(i + 1) * SUBLANES]
        pad_scr[lead + t_len + i * SUBLANES:lead + t_len + (i + 1) * SUBLANES, :] = jnp.where(
            sub < SUBLANES - 1, pltpu.roll(blk, SUBLANES - 1, axis=0), 0.0)
    uc = None
    for k in range(CONV_W):
        term = pad_scr[k * SUBLANES:k * SUBLANES + t_len, :] * cw_ref[k:k + 1, :]
        uc = term if uc is None else uc + term
    uc = uc + cb_ref[...]

    a_scr = (af_scr, ab_scr)
    b_scr = (bf_scr, bb_scr)
    for bl in range(cols // LRU_BW):
        c0 = bl * LRU_BW
        ub = uc[:, c0:c0 + LRU_BW]
        tg = jnp.tanh(_dot(ub.astype(BF16), wg_ref[bl]) + bg_ref[bl])
        for dr in range(2):
            tr = tg[:, dr * LRU_BW:(dr + 1) * LRU_BW]
            ti = tg[:, (2 + dr) * LRU_BW:(3 + dr) * LRU_BW]
            half_sp = (-0.25 * LRU_C) * jax.nn.softplus(-lam_ref[dr:dr + 1, c0:c0 + LRU_BW])
            t = jnp.tanh(half_sp * (1.0 + tr))
            inv = 1.0 / (1.0 - t)
            nt = -t
            root = jnp.where(nt > 0.0, nt * lax.rsqrt(nt), 0.0)
            a_scr[dr][:, c0:c0 + LRU_BW] = (1.0 + t) * inv
            b_scr[dr][:, c0:c0 + LRU_BW] = (root * inv) * ((1.0 + ti) * ub)

    def step(k, carry):
        a_f, b_f, a_b, b_b = carry
        rf = pl.multiple_of(k * SUBLANES, SUBLANES)
        rb = pl.multiple_of((n_blk - 1 - k) * SUBLANES, SUBLANES)
        a = af_scr[pl.ds(rf, SUBLANES), :]
        b_f = a * b_f + bf_scr[pl.ds(rf, SUBLANES), :]
        a_f = a * a_f
        af_scr[pl.ds(rf, SUBLANES), :] = a_f
        bf_scr[pl.ds(rf, SUBLANES), :] = b_f
        a = ab_scr[pl.ds(rb, SUBLANES), :]
        b_b = a * b_b + bb_scr[pl.ds(rb, SUBLANES), :]
        a_b = a * a_b
        ab_scr[pl.ds(rb, SUBLANES), :] = a_b
        bb_scr[pl.ds(rb, SUBLANES), :] = b_b
        return a_f, b_f, a_b, b_b

    one = jnp.ones((SUBLANES, cols), F32)
    zero = jnp.zeros((SUBLANES, cols), F32)
    a_f, b_f, a_b, b_b = lax.fori_loop(0, n_blk, step, (one, zero, one, zero), unroll=4)

    hf = h0_ref[0, 0:1, :]
    hf_in = zero
    for s in range(SUBLANES):
        hf_in = jnp.where(sub == s, hf, hf_in)
        hf = a_f[s:s + 1] * hf + b_f[s:s + 1]
    hb = h0_ref[0, 1:2, :]
    hb_in = zero
    for s in reversed(range(SUBLANES)):
        hb_in = jnp.where(sub == s, hb, hb_in)
        hb = a_b[s:s + 1] * hb + b_b[s:s + 1]
    st_ref[0, 0:1, :] = hf
    st_ref[0, 1:2, :] = hb

    blocks = lambda ref: ref[...].reshape(n_blk, SUBLANES, cols)
    h_sum = (blocks(af_scr) * hf_in + blocks(bf_scr)) + (blocks(ab_scr) * hb_in + blocks(bb_scr))
    mixed = (h_sum.reshape(t_len, cols) * y).astype(BF16)
    acc_scr[...] += _dot(mixed, wout_ref[...])

    @pl.when(j == pl.num_programs(1) - 1)
    def _():
        def scatter(k, carry):
            r = pl.multiple_of(k * SUBLANES, SUBLANES)
            for c in range(n_slab):
                perm_scr[c, pl.ds(k, SUBLANES, stride=pitch), :] = acc_scr[pl.ds(r, SUBLANES), slab(c)]
            return carry

        lax.fori_loop(0, n_blk, scatter, 0, unroll=2)
        for s in range(SUBLANES):
            rows = slice(s * n_blk, (s + 1) * n_blk)
            mixed_s = jnp.concatenate(
                [perm_scr[c, s * pitch:s * pitch + n_blk, :] for c in range(n_slab)], axis=1)
            o_ref[0, rows, :] = _finish(x_ref[0, rows, :], mixed_s, mod_ref, lng_ref, lnb_ref)


def _lru_layer(x, cond, w_in, conv_w, conv_b, wg, bg, lam, h0, w_out):
    b, t, d = x.shape
    cw = LRU_COLS if t > SHORT_SEQ_ROWS else D_RNN
    ncb = D_RNN // cw
    bpc = cw // LRU_BW
    pitch = t // SUBLANES + SEG_PITCH_PAD
    seq = lambda w: pl.BlockSpec((1, t, w), lambda i, j: (i, 0, 0))
    col = lambda rows: pl.BlockSpec((rows, cw), lambda i, j: (0, j))
    out, state = pl.pallas_call(
        _lru_kernel,
        grid=(b, ncb),
        in_specs=[
            seq(d),
            *_cond_specs(cond, 0, 2),
            pl.BlockSpec((d, cw), lambda i, j: (0, j)),
            pl.BlockSpec((d, cw), lambda i, j: (0, ncb + j)),
            col(CONV_W),
            col(1),
            pl.BlockSpec((bpc, LRU_BW, 4 * LRU_BW), lambda i, j: (j, 0, 0)),
            pl.BlockSpec((bpc, 1, 4 * LRU_BW), lambda i, j: (j, 0, 0)),
            col(2),
            pl.BlockSpec((1, 2, cw), lambda i, j: (i, 0, j)),
            pl.BlockSpec((cw, d), lambda i, j: (j, 0)),
        ],
        out_specs=[seq(d), pl.BlockSpec((1, 2, cw), lambda i, j: (i, 0, j))],
        out_shape=[jax.ShapeDtypeStruct(x.shape, x.dtype),
                   jax.ShapeDtypeStruct((b, 2, D_RNN), x.dtype)],
        scratch_shapes=[pltpu.VMEM((d // LANES, SUBLANES * pitch, LANES), F32),
                        pltpu.VMEM((t, d), BF16), pltpu.VMEM((t, d), F32),
                        pltpu.VMEM((t + (CONV_W - 1) * SUBLANES, cw), F32)]
                       + [pltpu.VMEM((t, cw), F32)] * 4,
        compiler_params=_params("arbitrary", "arbitrary"),
        name="lru_layer",
    )(x, *cond.arrays, w_in, w_in, conv_w, conv_b, wg, bg, lam, h0, w_out)
    return out, state


def _rope_rotate_cols(w):
    nf = ROPE_NF
    return jnp.concatenate([-w[..., nf:2 * nf], w[..., :nf], -w[..., 3 * nf:], w[..., 2 * nf:3 * nf]], axis=-1)


def _rope_tables(t_len):
    t = jnp.arange(t_len)
    rows = (t // GRID_W).astype(F32)
    cols = (t % GRID_W).astype(F32)
    inv = ROPE_THETA ** (-jnp.arange(ROPE_NF, dtype=F32) / ROPE_NF)
    ar, ac = rows[:, None] * inv, cols[:, None] * inv
    cos = jnp.concatenate([jnp.cos(ar), jnp.cos(ar), jnp.cos(ac), jnp.cos(ac)], axis=1)
    sin = jnp.concatenate([jnp.sin(ar), jnp.sin(ar), jnp.sin(ac), jnp.sin(ac)], axis=1)
    return jnp.tile(cos, (1, 2)), jnp.tile(sin, (1, 2))


def _mla_weights(w_dq, w_uq, w_dkv, w_uk, w_uv, w_o, rope):
    r = w_uq.shape[0]
    w_nope, w_rope = w_uq[:, :, :QK_NOPE], w_uq[:, :, QK_NOPE:]
    wq = jnp.concatenate([w_nope.reshape(r, -1), w_rope.reshape(r, -1)], axis=1).astype(BF16)
    wk = w_dkv[:, KV_LORA:]
    wdkv = [w_dkv[:, :KV_LORA], wk, wk]
    wq_rot = None
    if rope:
        wq_rot = _rope_rotate_cols(w_rope).reshape(r, -1).astype(BF16)
        wk_rot = _rope_rotate_cols(wk)
        wdkv += [wk_rot, wk_rot]
    return (w_dq.astype(BF16), wq, jnp.concatenate(wdkv, axis=1).astype(BF16),
            w_uk.reshape(KV_LORA, -1).astype(BF16), w_uv.reshape(KV_LORA, -1).astype(BF16),
            w_o.astype(BF16), wq_rot)


def _lru_gate_weights(w_a, b_a, w_i, b_i):
    wg = (0.5 * jnp.concatenate([w_a[0], w_a[1], w_i[0], w_i[1]], axis=-1)).astype(BF16)
    blk = lambda v: v.reshape(LRU_BLOCKS, 1, LRU_BW)
    bg = 0.5 * jnp.concatenate([blk(b_a[0]), blk(b_a[1]), blk(b_i[0]), blk(b_i[1])], axis=-1)
    return wg, bg


def kernel(x_prompt, x_sample, cache_mla_ckv, cache_mla_krope, state_lru, c, c_ctx, w_ada, b_ada, ln_g, ln_b, w_ffn_in, w_ffn_out, w_pool, pool_scale, w_dq, g_q, w_uq, w_dkv, g_kv, w_uk, w_uv, w_mla_o, w_lru_in, lru_conv_w, lru_conv_b, w_lru_a, b_lru_a, w_lru_i, b_lru_i, lru_lambda, w_lru_out):
    d = D_MODEL
    n_lat = c.shape[0]
    cond = jnp.concatenate([c_ctx[None], c, jnp.zeros((MOD_ROWS - 1 - n_lat, d), F32)], axis=0)
    mod = _modulation(cond, w_ada, b_ada)
    lng, lnb = ln_g.reshape(DEPTH * 2, 1, d), ln_b.reshape(DEPTH * 2, 1, d)

    xs = [x_prompt, x_sample]
    ckv_out, krope_out, lru_out = [], [], []
    cos, sin = _rope_tables(x_sample.shape[1])
    win, wout = w_ffn_in.astype(BF16), w_ffn_out.astype(BF16)
    for i in range(DEPTH):
        kind, j = i % N_MIXERS, i // N_MIXERS
        row = lambda v: v.reshape(1, -1)
        if kind == 0:
            wp, ps = w_pool[j].astype(BF16), row(pool_scale[j])
        elif kind == 1:
            mla_w = [_mla_weights(w_dq[j], w_uq[j], w_dkv[j], w_uk[j], w_uv[j], w_mla_o[j], rope)
                     for rope in (False, True)]
            gq, gkv = row(g_q[j]), row(g_kv[j])
        else:
            w_in = w_lru_in[j].astype(BF16)
            wg, bg = _lru_gate_weights(w_lru_a[j], b_lru_a[j], w_lru_i[j], b_lru_i[j])
            w_out = w_lru_out[j].astype(BF16)
        for p in range(2):
            x, m = xs[p], _Cond(mod, lng, lnb, i, p == 1)
            if kind == 0:
                x = _pool_layer(x, m, wp, ps)
            elif kind == 1:
                wdq, wq, wdkv, wuk, wuv, wo, wq_rot = mla_w[p]
                if p == 0:
                    qcat, kcat, v, ckv, kr64 = _mla_project(
                        x, m, wdq, gq, wq, wdkv, gkv, wuk, wuv, None, True)
                    ckv_out.append(ckv)
                    krope_out.append(kr64)
                    segments = [(kcat, v)]
                else:
                    qcat, kcat, v = _mla_project(
                        x, m, wdq, gq, wq, wdkv, gkv, wuk, wuv, (wq_rot, cos, sin), False)
                    kr_c = cache_mla_krope[:, j]
                    kr_c = jnp.concatenate([kr_c, kr_c], axis=-1).astype(BF16)
                    kcat_c, v_c = _kv_up(cache_mla_ckv[:, j], kr_c, wuk, wuv)
                    segments = [(kcat_c, v_c), (kcat, v)]
                x = _attn_layer(x, m, qcat, wo, segments)
            else:
                if p == 0:
                    h0 = jnp.zeros((x.shape[0], 2, D_RNN), F32)
                else:
                    h0 = state_lru[:, j]
                x, st = _lru_layer(x, m, w_in, lru_conv_w[j], row(lru_conv_b[j]),
                                   wg, bg, lru_lambda[j], h0, w_out)
                if p == 0:
                    lru_out.append(st)
            xs[p] = _ffn_layer(x, m, win, wout)
    return (xs[0], xs[1], jnp.stack(ckv_out, axis=1), jnp.stack(krope_out, axis=1),
            jnp.stack(lru_out, axis=1))
```

```python
import functools
import math
from typing import NamedTuple

import jax
import jax.numpy as jnp
from jax import lax
from jax.experimental import pallas as pl
from jax.experimental.pallas import tpu as pltpu

F32 = jnp.float32
BF16 = jnp.bfloat16

D_MODEL = 1024
DEPTH = 4
GRID_W = 64
N_MIXERS = 3
POOL_WINDOWS = (2, 4, 8, 16)
POOL_GROUPS = 4
POOL_GC = D_MODEL // POOL_GROUPS
MLA_HEADS = 8
Q_LORA = 384
KV_LORA = 256
QK_NOPE = 128
QK_ROPE = 64
V_HEAD = 128
ROPE_NF = QK_ROPE // 4
ROPE_THETA = 10000.0
MLA_SCALE = (QK_NOPE + QK_ROPE) ** -0.5
D_RNN = D_MODEL
LRU_BLOCKS = 8
LRU_BW = D_RNN // LRU_BLOCKS
CONV_W = 4
CONV_LEFT = 1
LRU_C = 8.0
D_FF = ((8 * D_MODEL // 3 + 255) // 256) * 256
ALPHA = (2.0 * DEPTH) ** 0.25
EPS = 1e-6

LANES = 128
SUBLANES = 8
VMEM_LIMIT_BYTES = 56 * 1024 * 1024
PAD_ROWS = SUBLANES
MOD_ROWS = 16
MOD_CHUNKS = 2
LRU_COLS = 512
SEG_PITCH_PAD = 4
SHORT_SEQ_ROWS = 512
PROJ_TILE = 512
FFN_TILE = 512
ATTN_TILE = 512
QK_CAT = 2 * LANES


def _params(*sem):
    return pltpu.CompilerParams(dimension_semantics=sem, vmem_limit_bytes=VMEM_LIMIT_BYTES)


def _dot(a, b):
    return jnp.dot(a, b, preferred_element_type=F32)


def _layer_norm(y, g, b):
    mu = jnp.mean(y, axis=-1, keepdims=True)
    d = y - mu
    var = jnp.mean(d * d, axis=-1, keepdims=True)
    return d * lax.rsqrt(var + EPS) * g + b


def _gelu_tanh(x):
    c = math.sqrt(2.0 / math.pi)
    return (0.5 * x) * (1.0 + jnp.tanh(x * (c + (c * 0.044715) * (x * x))))


def _rms_norm(y, g):
    return y * lax.rsqrt(jnp.mean(y * y, axis=-1, keepdims=True) + EPS) * g


def _modulate(x, mod_ref):
    return x * (1.0 + mod_ref[1, 0]) + mod_ref[0, 0]


def _finish(x, mixed, mod_ref, lng_ref, lnb_ref):
    return _layer_norm(ALPHA * x + mod_ref[2, 0] * mixed, lng_ref[0], lnb_ref[0])


class _Cond(NamedTuple):
    mod: jax.Array
    lng: jax.Array
    lnb: jax.Array
    layer: int
    latent: bool

    @property
    def arrays(self):
        return self.mod, self.lng, self.lnb


def _cond_specs(cond, sub, nargs):
    k = 2 * cond.layer + sub
    row = (lambda b: 1 + b) if cond.latent else (lambda b: 0)
    if nargs == 1:
        mod = pl.BlockSpec((3, 1, 1, D_MODEL), lambda b: (k, row(b), 0, 0))
        ln = pl.BlockSpec((1, 1, D_MODEL), lambda b: (k, 0, 0))
    else:
        mod = pl.BlockSpec((3, 1, 1, D_MODEL), lambda b, t: (k, row(b), 0, 0))
        ln = pl.BlockSpec((1, 1, D_MODEL), lambda b, t: (k, 0, 0))
    return [mod, ln, ln]


def _const_spec(shape, nargs):
    zeros = (0,) * len(shape)
    if nargs == 1:
        return pl.BlockSpec(shape, lambda b: zeros)
    return pl.BlockSpec(shape, lambda b, t: zeros)


def _mod_kernel(cond_ref, w_ref, b_ref, o_ref):
    s = jax.nn.silu(cond_ref[...]).astype(BF16)
    for k in range(o_ref.shape[1]):
        cols = slice(k * D_MODEL, (k + 1) * D_MODEL)
        res = _dot(s, w_ref[0, :, cols].astype(BF16)) + b_ref[0, k]
        for r in range(MOD_ROWS):
            o_ref[0, k, r] = res[r:r + 1]


def _modulation(cond, w_ada, b_ada):
    d = D_MODEL
    nk = MOD_CHUNKS
    return pl.pallas_call(
        _mod_kernel,
        grid=(DEPTH, 6 // nk),
        in_specs=[
            pl.BlockSpec((MOD_ROWS, d), lambda i, j: (0, 0)),
            pl.BlockSpec((1, d, nk * d), lambda i, j: (i, 0, j)),
            pl.BlockSpec((1, nk, 1, d), lambda i, j: (i, j, 0, 0)),
        ],
        out_specs=pl.BlockSpec((1, nk, MOD_ROWS, 1, d), lambda i, j: (i, j, 0, 0, 0)),
        out_shape=jax.ShapeDtypeStruct((DEPTH, 6, MOD_ROWS, 1, d), F32),
        compiler_params=_params("arbitrary", "arbitrary"),
        name="adaln_modulation",
    )(cond, w_ada, b_ada.reshape(DEPTH, 6, 1, d)).reshape(DEPTH * 6, MOD_ROWS, 1, d)


def _ffn_kernel(x_ref, mod_ref, lng_ref, lnb_ref, win_ref, wout_ref, o_ref):
    x = x_ref[0]
    h = _modulate(x, mod_ref).astype(BF16)
    ab = _dot(h, win_ref[0].astype(BF16))
    u = (jax.nn.silu(ab[:, :D_FF]) * ab[:, D_FF:]).astype(BF16)
    o_ref[0] = _finish(x, _dot(u, wout_ref[0].astype(BF16)), mod_ref, lng_ref, lnb_ref)


def _ffn_layer(x, cond, win, wout):
    layer = cond.layer
    shape = x.shape
    d = shape[-1]
    if not cond.latent and shape[1] < FFN_TILE:
        x = x.reshape(-1, FFN_TILE, d)
    b, t, _ = x.shape
    tm = min(FFN_TILE, t)
    slab = lambda r, c: pl.BlockSpec((1, r, c), lambda i, j: (layer, 0, 0), pipeline_mode=pl.Buffered(1))
    return pl.pallas_call(
        _ffn_kernel,
        grid=(b, t // tm),
        in_specs=[
            pl.BlockSpec((1, tm, d), lambda i, j: (i, j, 0)),
            *_cond_specs(cond, 1, 2),
            slab(d, 2 * D_FF),
            slab(D_FF, d),
        ],
        out_specs=pl.BlockSpec((1, tm, d), lambda i, j: (i, j, 0)),
        out_shape=jax.ShapeDtypeStruct(x.shape, x.dtype),
        compiler_params=_params("arbitrary", "arbitrary"),
        name="ffn_layer",
    )(x, *cond.arrays, win, wout).reshape(shape)


def _pool_kernel(x_ref, mod_ref, lng_ref, lnb_ref, wp_ref, ps_ref, o_ref):
    for q in range(x_ref.shape[0]):
        o_ref[q] = _pool_sequence(x_ref[q], mod_ref, lng_ref, lnb_ref, wp_ref, ps_ref)


def _pool_sequence(x, mod_ref, lng_ref, lnb_ref, wp_ref, ps_ref):
    t_len = x.shape[0]
    n_pad = t_len + 2 * PAD_ROWS
    h = _modulate(x, mod_ref)
    edge = jnp.zeros((PAD_ROWS, POOL_GC), F32)
    edge_row = lax.broadcasted_iota(jnp.int32, (PAD_ROWS, 1), 0)
    outs = []
    for gi, w in enumerate(POOL_WINDOWS):
        assert w & (w - 1) == 0 and w // 2 <= PAD_ROWS
        hg = h[:, gi * POOL_GC:(gi + 1) * POOL_GC]
        run = jnp.concatenate([edge, hg, edge], axis=0)
        span = 1
        while 2 * span < w:
            run = run + pltpu.roll(run, n_pad - span, axis=0)
            span *= 2
        s = (run + pltpu.roll(run, span, axis=0))[PAD_ROWS:PAD_ROWS + t_len]
        lo, hi = -(w // 2), w - w // 2
        parts = []
        for r0 in (0, t_len - PAD_ROWS):
            t = r0 + edge_row
            cnt = (jnp.minimum(t + hi, t_len) - jnp.maximum(t + lo, 0)).astype(F32)
            parts.append(s[r0:r0 + PAD_ROWS] / cnt - hg[r0:r0 + PAD_ROWS])
        mid = s[PAD_ROWS:t_len - PAD_ROWS] * (1.0 / w) - hg[PAD_ROWS:t_len - PAD_ROWS]
        pooled = jnp.concatenate([parts[0], mid, parts[1]], axis=0)
        outs.append(_dot(pooled.astype(BF16), wp_ref[gi]))
    mixed = jnp.concatenate(outs, axis=1) * ps_ref[...]
    return _finish(x, mixed, mod_ref, lng_ref, lnb_ref)


def _pool_layer(x, cond, wp, ps):
    b, t, d = x.shape
    nq = 1 if cond.latent else max(1, SHORT_SEQ_ROWS // t)
    return pl.pallas_call(
        _pool_kernel,
        grid=(b // nq,),
        in_specs=[
            pl.BlockSpec((nq, t, d), lambda i: (i, 0, 0)),
            *_cond_specs(cond, 0, 1),
            _const_spec((POOL_GROUPS, POOL_GC, POOL_GC), 1),
            _const_spec((1, d), 1),
        ],
        out_specs=pl.BlockSpec((nq, t, d), lambda i: (i, 0, 0)),
        out_shape=jax.ShapeDtypeStruct(x.shape, x.dtype),
        compiler_params=_params("arbitrary"),
        name="pool_layer",
    )(x, *cond.arrays, wp, ps)


def _store_k_cat(kcat_ref, k_nope, k_rope2):
    for hd in range(MLA_HEADS):
        kcat_ref[0, :, hd * QK_CAT:hd * QK_CAT + QK_NOPE] = k_nope[:, hd * QK_NOPE:(hd + 1) * QK_NOPE]
        kcat_ref[0, :, hd * QK_CAT + QK_NOPE:(hd + 1) * QK_CAT] = k_rope2


def _mla_proj_kernel(*refs, rope, emit_cache):
    x_ref, mod_ref, wdq_ref, gq_ref, wq_ref, wdkv_ref, gkv_ref, wuk_ref, wuv_ref = refs[:9]
    refs = refs[9:]
    if rope:
        wqrot_ref, cos_ref, sin_ref = refs[:3]
        refs = refs[3:]
    qcat_ref, kcat_ref, v_ref = refs[:3]
    h = _modulate(x_ref[0], mod_ref).astype(BF16)
    q_lat = _rms_norm(_dot(h, wdq_ref[...]), gq_ref[...]).astype(BF16)
    nope_w = MLA_HEADS * QK_NOPE
    q = _dot(q_lat, wq_ref[...])
    kv = _dot(h, wdkv_ref[...])
    k_rope2 = kv[:, KV_LORA:KV_LORA + 2 * QK_ROPE]
    if rope:
        cos, sin = cos_ref[...], sin_ref[...]
        q_rot = _dot(q_lat, wqrot_ref[...])
        k_rope2 = k_rope2 * cos + kv[:, KV_LORA + 2 * QK_ROPE:] * sin
    lane = lax.broadcasted_iota(jnp.int32, (1, LANES), 1)
    for pair in range(MLA_HEADS // 2):
        both = q[:, nope_w + pair * LANES:nope_w + (pair + 1) * LANES]
        if rope:
            both = both * cos + q_rot[:, pair * LANES:(pair + 1) * LANES] * sin
        for hd, keep in ((2 * pair, lane < QK_ROPE), (2 * pair + 1, lane >= QK_ROPE)):
            c0 = hd * QK_CAT
            qcat_ref[0, :, c0:c0 + QK_NOPE] = q[:, hd * QK_NOPE:(hd + 1) * QK_NOPE].astype(BF16)
            qcat_ref[0, :, c0 + QK_NOPE:c0 + QK_CAT] = jnp.where(keep, both, 0.0).astype(BF16)
    c_kv = _rms_norm(kv[:, :KV_LORA], gkv_ref[...])
    c_kv16 = c_kv.astype(BF16)
    _store_k_cat(kcat_ref, _dot(c_kv16, wuk_ref[...]).astype(BF16), k_rope2.astype(BF16))
    v_ref[0] = _dot(c_kv16, wuv_ref[...]).astype(BF16)
    if emit_cache:
        ckv_out_ref, kr_out_ref = refs[3:5]
        ckv_out_ref[0] = c_kv
        kr_out_ref[0] = kv[:, KV_LORA:KV_LORA + QK_ROPE]


def _mla_project(x, cond, wdq, gq, wq, wdkv, gkv, wuk, wuv, rope_args, emit_cache):
    b0, t0, d = x.shape
    rope = rope_args is not None
    if not cond.latent and not rope and t0 < PROJ_TILE:
        x = x.reshape(-1, PROJ_TILE, d)
    b, t, _ = x.shape
    tm = min(PROJ_TILE, t)
    tok = lambda w: pl.BlockSpec((1, tm, w), lambda i, j: (i, j, 0))
    args = [x, cond.mod, wdq, gq, wq, wdkv, gkv, wuk, wuv]
    in_specs = [tok(d), _cond_specs(cond, 0, 2)[0]] + [_const_spec(a.shape, 2) for a in args[2:]]
    if rope:
        wqrot, cos, sin = rope_args
        args += [wqrot, cos, sin]
        in_specs += [_const_spec(wqrot.shape, 2)] + [pl.BlockSpec((tm, LANES), lambda i, j: (j, 0))] * 2
    widths = [MLA_HEADS * QK_CAT, MLA_HEADS * QK_CAT, MLA_HEADS * V_HEAD]
    out_specs = [tok(w) for w in widths]
    out_shape = [jax.ShapeDtypeStruct((b, t, w), BF16) for w in widths]
    if emit_cache:
        out_specs += [tok(KV_LORA), tok(QK_ROPE)]
        out_shape += [jax.ShapeDtypeStruct((b, t, KV_LORA), F32),
                      jax.ShapeDtypeStruct((b, t, QK_ROPE), F32)]
    outs = pl.pallas_call(
        functools.partial(_mla_proj_kernel, rope=rope, emit_cache=emit_cache),
        grid=(b, t // tm),
        in_specs=in_specs,
        out_specs=out_specs,
        out_shape=out_shape,
        compiler_params=_params("arbitrary", "arbitrary"),
        name="mla_project",
    )(*args)
    return [o.reshape(b0, t0, o.shape[-1]) for o in outs]


def _kv_up_kernel(ckv_ref, kr2_ref, wuk_ref, wuv_ref, kcat_ref, v_ref):
    c = ckv_ref[0].astype(BF16)
    _store_k_cat(kcat_ref, _dot(c, wuk_ref[...]).astype(BF16), kr2_ref[0])
    v_ref[0] = _dot(c, wuv_ref[...]).astype(BF16)


def _kv_up(ckv, kr2, wuk, wuv):
    b, s, r = ckv.shape
    seq = lambda w: pl.BlockSpec((1, s, w), lambda i: (i, 0, 0))
    widths = [MLA_HEADS * QK_CAT, MLA_HEADS * V_HEAD]
    return pl.pallas_call(
        _kv_up_kernel,
        grid=(b,),
        in_specs=[seq(r), seq(kr2.shape[2]), _const_spec(wuk.shape, 1), _const_spec(wuv.shape, 1)],
        out_specs=[seq(w) for w in widths],
        out_shape=[jax.ShapeDtypeStruct((b, s, w), BF16) for w in widths],
        compiler_params=_params("arbitrary"),
        name="mla_cache_kv_up",
    )(ckv, kr2, wuk, wuv)


def _attn_kernel(*refs, n_seg):
    x_ref, mod_ref, lng_ref, lnb_ref, q_ref, wo_ref = refs[:6]
    seg_refs = [refs[6 + 2 * i:8 + 2 * i] for i in range(n_seg)]
    o_ref = refs[6 + 2 * n_seg]
    nt = (((1,), (1,)), ((), ()))
    exp2_scale = MLA_SCALE * math.log2(math.e)
    for sq in range(x_ref.shape[0]):
        heads = []
        for hd in range(MLA_HEADS):
            q = q_ref[sq, :, hd * QK_CAT:(hd + 1) * QK_CAT]
            scores = [lax.dot_general(q, k_ref[sq, :, hd * QK_CAT:(hd + 1) * QK_CAT], nt,
                                      preferred_element_type=F32) for k_ref, _ in seg_refs]
            m = functools.reduce(jnp.maximum, [jnp.max(s, axis=-1, keepdims=True) for s in scores])
            es = [jnp.exp2((s - m) * exp2_scale) for s in scores]
            total = functools.reduce(lambda a, b: a + b, [jnp.sum(e, axis=-1, keepdims=True) for e in es])
            o = None
            for e, (_, v_ref) in zip(es, seg_refs):
                part = _dot(e.astype(BF16), v_ref[sq, :, hd * V_HEAD:(hd + 1) * V_HEAD])
                o = part if o is None else o + part
            heads.append((o * (1.0 / total)).astype(BF16))
        mixed = _dot(jnp.concatenate(heads, axis=1), wo_ref[...])
        o_ref[sq] = _finish(x_ref[sq], mixed, mod_ref, lng_ref, lnb_ref)


def _attn_layer(x, cond, qcat, wo, segments):
    b, t, d = x.shape
    tq = min(ATTN_TILE, t)
    nq = 1 if cond.latent else max(1, SHORT_SEQ_ROWS // t)
    tok = lambda w: pl.BlockSpec((nq, tq, w), lambda i, j: (i, j, 0))
    in_specs = [tok(d), *_cond_specs(cond, 0, 2),
                tok(qcat.shape[2]), _const_spec(wo.shape, 2)]
    args = [x, *cond.arrays, qcat, wo]
    for seg in segments:
        for a in seg:
            in_specs.append(pl.BlockSpec((nq,) + a.shape[1:], lambda i, j: (i, 0, 0)))
            args.append(a)
    return pl.pallas_call(
        functools.partial(_attn_kernel, n_seg=len(segments)),
        grid=(b // nq, t // tq),
        in_specs=in_specs,
        out_specs=tok(d),
        out_shape=jax.ShapeDtypeStruct(x.shape, x.dtype),
        compiler_params=_params("arbitrary", "arbitrary"),
        name="mla_attention",
    )(*args)


def _lru_kernel(x_ref, mod_ref, lng_ref, lnb_ref, wu_ref, wy_ref, cw_ref, cb_ref, wg_ref, bg_ref,
                lam_ref, h0_ref, wout_ref, o_ref, st_ref,
                perm_scr, h_scr, acc_scr, pad_scr, af_scr, bf_scr, ab_scr, bb_scr):
    t_len = x_ref.shape[1]
    n_blk = t_len // SUBLANES
    pitch = n_blk + SEG_PITCH_PAD
    n_slab = D_MODEL // LANES
    cols = wu_ref.shape[1]
    j = pl.program_id(1)
    slab = lambda c: slice(c * LANES, (c + 1) * LANES)

    @pl.when(j == 0)
    def _():
        for c in range(n_slab):
            for s in range(SUBLANES):
                perm_scr[c, s * pitch:s * pitch + n_blk, :] = x_ref[0, s * n_blk:(s + 1) * n_blk, slab(c)]

        def gather(k, carry):
            r = pl.multiple_of(k * SUBLANES, SUBLANES)
            for c in range(n_slab):
                acc_scr[pl.ds(r, SUBLANES), slab(c)] = perm_scr[c, pl.ds(k, SUBLANES, stride=pitch), :]
            return carry

        lax.fori_loop(0, n_blk, gather, 0, unroll=2)
        h_scr[...] = _modulate(acc_scr[...], mod_ref).astype(BF16)
        acc_scr[...] = jnp.zeros_like(acc_scr)

    h = h_scr[...]
    u = _dot(h, wu_ref[...])
    y = _gelu_tanh(_dot(h, wy_ref[...]))

    sub = lax.broadcasted_iota(jnp.int32, (SUBLANES, cols), 0)
    lead = CONV_LEFT * SUBLANES
    for i in range(1, CONV_LEFT + 1):
        blk = u[(n_blk - i) * SUBLANES:(n_blk - i + 1) * SUBLANES]
        pad_scr[lead - i * SUBLANES:lead - (i - 1) * SUBLANES, :] = jnp.where(
            sub >= 1, pltpu.roll(blk, 1, axis=0), 0.0)
    pad_scr[lead:lead + t_len, :] = u
    for i in range(CONV_W - 1 - CONV_LEFT):
        blk = u[i * SUBLANES:(i + 1) * SUBLANES]
        pad_scr[lead + t_len + i * SUBLANES:lead + t_len + (i + 1) * SUBLANES, :] = jnp.where(
            sub < SUBLANES - 1, pltpu.roll(blk, SUBLANES - 1, axis=0), 0.0)
    uc = None
    for k in range(CONV_W):
        term = pad_scr[k * SUBLANES:k * SUBLANES + t_len, :] * cw_ref[k:k + 1, :]
        uc = term if uc is None else uc + term
    uc = uc + cb_ref[...]

    a_scr = (af_scr, ab_scr)
    b_scr = (bf_scr, bb_scr)
    for bl in range(cols // LRU_BW):
        c0 = bl * LRU_BW
        ub = uc[:, c0:c0 + LRU_BW]
        tg = jnp.tanh(_dot(ub.astype(BF16), wg_ref[bl]) + bg_ref[bl])
        for dr in range(2):
            tr = tg[:, dr * LRU_BW:(dr + 1) * LRU_BW]
            ti = tg[:, (2 + dr) * LRU_BW:(3 + dr) * LRU_BW]
            half_sp = (-0.25 * LRU_C) * jax.nn.softplus(-lam_ref[dr:dr + 1, c0:c0 + LRU_BW])
            t = jnp.tanh(half_sp * (1.0 + tr))
            inv = 1.0 / (1.0 - t)
            nt = -t
            root = jnp.where(nt > 0.0, nt * lax.rsqrt(nt), 0.0)
            a_scr[dr][:, c0:c0 + LRU_BW] = (1.0 + t) * inv
            b_scr[dr][:, c0:c0 + LRU_BW] = (root * inv) * ((1.0 + ti) * ub)

    def step(k, carry):
        a_f, b_f, a_b, b_b = carry
        rf = pl.multiple_of(k * SUBLANES, SUBLANES)
        rb = pl.multiple_of((n_blk - 1 - k) * SUBLANES, SUBLANES)
        a = af_scr[pl.ds(rf, SUBLANES), :]
        b_f = a * b_f + bf_scr[pl.ds(rf, SUBLANES), :]
        a_f = a * a_f
        af_scr[pl.ds(rf, SUBLANES), :] = a_f
        bf_scr[pl.ds(rf, SUBLANES), :] = b_f
        a = ab_scr[pl.ds(rb, SUBLANES), :]
        b_b = a * b_b + bb_scr[pl.ds(rb, SUBLANES), :]
        a_b = a * a_b
        ab_scr[pl.ds(rb, SUBLANES), :] = a_b
        bb_scr[pl.ds(rb, SUBLANES), :] = b_b
        return a_f, b_f, a_b, b_b

    one = jnp.ones((SUBLANES, cols), F32)
    zero = jnp.zeros((SUBLANES, cols), F32)
    a_f, b_f, a_b, b_b = lax.fori_loop(0, n_blk, step, (one, zero, one, zero), unroll=4)

    hf = h0_ref[0, 0:1, :]
    hf_in = zero
    for s in range(SUBLANES):
        hf_in = jnp.where(sub == s, hf, hf_in)
        hf = a_f[s:s + 1] * hf + b_f[s:s + 1]
    hb = h0_ref[0, 1:2, :]
    hb_in = zero
    for s in reversed(range(SUBLANES)):
        hb_in = jnp.where(sub == s, hb, hb_in)
        hb = a_b[s:s + 1] * hb + b_b[s:s + 1]
    st_ref[0, 0:1, :] = hf
    st_ref[0, 1:2, :] = hb

    blocks = lambda ref: ref[...].reshape(n_blk, SUBLANES, cols)
    h_sum = (blocks(af_scr) * hf_in + blocks(bf_scr)) + (blocks(ab_scr) * hb_in + blocks(bb_scr))
    mixed = (h_sum.reshape(t_len, cols) * y).astype(BF16)
    acc_scr[...] += _dot(mixed, wout_ref[...])

    @pl.when(j == pl.num_programs(1) - 1)
    def _():
        def scatter(k, carry):
            r = pl.multiple_of(k * SUBLANES, SUBLANES)
            for c in range(n_slab):
                perm_scr[c, pl.ds(k, SUBLANES, stride=pitch), :] = acc_scr[pl.ds(r, SUBLANES), slab(c)]
            return carry

        lax.fori_loop(0, n_blk, scatter, 0, unroll=2)
        for s in range(SUBLANES):
            rows = slice(s * n_blk, (s + 1) * n_blk)
            mixed_s = jnp.concatenate(
                [perm_scr[c, s * pitch:s * pitch + n_blk, :] for c in range(n_slab)], axis=1)
            o_ref[0, rows, :] = _finish(x_ref[0, rows, :], mixed_s, mod_ref, lng_ref, lnb_ref)


def _lru_layer(x, cond, w_in, conv_w, conv_b, wg, bg, lam, h0, w_out):
    b, t, d = x.shape
    cw = LRU_COLS if t > SHORT_SEQ_ROWS else D_RNN
    ncb = D_RNN // cw
    bpc = cw // LRU_BW
    pitch = t // SUBLANES + SEG_PITCH_PAD
    seq = lambda w: pl.BlockSpec((1, t, w), lambda i, j: (i, 0, 0))
    col = lambda rows: pl.BlockSpec((rows, cw), lambda i, j: (0, j))
    out, state = pl.pallas_call(
        _lru_kernel,
        grid=(b, ncb),
        in_specs=[
            seq(d),
            *_cond_specs(cond, 0, 2),
            pl.BlockSpec((d, cw), lambda i, j: (0, j)),
            pl.BlockSpec((d, cw), lambda i, j: (0, ncb + j)),
            col(CONV_W),
            col(1),
            pl.BlockSpec((bpc, LRU_BW, 4 * LRU_BW), lambda i, j: (j, 0, 0)),
            pl.BlockSpec((bpc, 1, 4 * LRU_BW), lambda i, j: (j, 0, 0)),
            col(2),
            pl.BlockSpec((1, 2, cw), lambda i, j: (i, 0, j)),
            pl.BlockSpec((cw, d), lambda i, j: (j, 0)),
        ],
        out_specs=[seq(d), pl.BlockSpec((1, 2, cw), lambda i, j: (i, 0, j))],
        out_shape=[jax.ShapeDtypeStruct(x.shape, x.dtype),
                   jax.ShapeDtypeStruct((b, 2, D_RNN), x.dtype)],
        scratch_shapes=[pltpu.VMEM((d // LANES, SUBLANES * pitch, LANES), F32),
                        pltpu.VMEM((t, d), BF16), pltpu.VMEM((t, d), F32),
                        pltpu.VMEM((t + (CONV_W - 1) * SUBLANES, cw), F32)]
                       + [pltpu.VMEM((t, cw), F32)] * 4,
        compiler_params=_params("arbitrary", "arbitrary"),
        name="lru_layer",
    )(x, *cond.arrays, w_in, w_in, conv_w, conv_b, wg, bg, lam, h0, w_out)
    return out, state


def _rope_rotate_cols(w):
    nf = ROPE_NF
    return jnp.concatenate([-w[..., nf:2 * nf], w[..., :nf], -w[..., 3 * nf:], w[..., 2 * nf:3 * nf]], axis=-1)


def _rope_tables(t_len):
    t = jnp.arange(t_len)
    rows = (t // GRID_W).astype(F32)
    cols = (t % GRID_W).astype(F32)
    inv = ROPE_THETA ** (-jnp.arange(ROPE_NF, dtype=F32) / ROPE_NF)
    ar, ac = rows[:, None] * inv, cols[:, None] * inv
    cos = jnp.concatenate([jnp.cos(ar), jnp.cos(ar), jnp.cos(ac), jnp.cos(ac)], axis=1)
    sin = jnp.concatenate([jnp.sin(ar), jnp.sin(ar), jnp.sin(ac), jnp.sin(ac)], axis=1)
    return jnp.tile(cos, (1, 2)), jnp.tile(sin, (1, 2))


def _mla_weights(w_dq, w_uq, w_dkv, w_uk, w_uv, w_o, rope):
    r = w_uq.shape[0]
    w_nope, w_rope = w_uq[:, :, :QK_NOPE], w_uq[:, :, QK_NOPE:]
    wq = jnp.concatenate([w_nope.reshape(r, -1), w_rope.reshape(r, -1)], axis=1).astype(BF16)
    wk = w_dkv[:, KV_LORA:]
    wdkv = [w_dkv[:, :KV_LORA], wk, wk]
    wq_rot = None
    if rope:
        wq_rot = _rope_rotate_cols(w_rope).reshape(r, -1).astype(BF16)
        wk_rot = _rope_rotate_cols(wk)
        wdkv += [wk_rot, wk_rot]
    return (w_dq.astype(BF16), wq, jnp.concatenate(wdkv, axis=1).astype(BF16),
            w_uk.reshape(KV_LORA, -1).astype(BF16), w_uv.reshape(KV_LORA, -1).astype(BF16),
            w_o.astype(BF16), wq_rot)


def _lru_gate_weights(w_a, b_a, w_i, b_i):
    wg = (0.5 * jnp.concatenate([w_a[0], w_a[1], w_i[0], w_i[1]], axis=-1)).astype(BF16)
    blk = lambda v: v.reshape(LRU_BLOCKS, 1, LRU_BW)
    bg = 0.5 * jnp.concatenate([blk(b_a[0]), blk(b_a[1]), blk(b_i[0]), blk(b_i[1])], axis=-1)
    return wg, bg


def kernel(x_prompt, x_sample, cache_mla_ckv, cache_mla_krope, state_lru, c, c_ctx, w_ada, b_ada, ln_g, ln_b, w_ffn_in, w_ffn_out, w_pool, pool_scale, w_dq, g_q, w_uq, w_dkv, g_kv, w_uk, w_uv, w_mla_o, w_lru_in, lru_conv_w, lru_conv_b, w_lru_a, b_lru_a, w_lru_i, b_lru_i, lru_lambda, w_lru_out):
    d = D_MODEL
    n_lat = c.shape[0]
    cond = jnp.concatenate([c_ctx[None], c, jnp.zeros((MOD_ROWS - 1 - n_lat, d), F32)], axis=0)
    mod = _modulation(cond, w_ada, b_ada)
    lng, lnb = ln_g.reshape(DEPTH * 2, 1, d), ln_b.reshape(DEPTH * 2, 1, d)

    xs = [x_prompt, x_sample]
    ckv_out, krope_out, lru_out = [], [], []
    cos, sin = _rope_tables(x_sample.shape[1])
    win, wout = w_ffn_in, w_ffn_out
    for i in range(DEPTH):
        kind, j = i % N_MIXERS, i // N_MIXERS
        row = lambda v: v.reshape(1, -1)
        if kind == 0:
            wp, ps = w_pool[j].astype(BF16), row(pool_scale[j])
        elif kind == 1:
            mla_w = [_mla_weights(w_dq[j], w_uq[j], w_dkv[j], w_uk[j], w_uv[j], w_mla_o[j], rope)
                     for rope in (False, True)]
            gq, gkv = row(g_q[j]), row(g_kv[j])
        else:
            w_in = w_lru_in[j].astype(BF16)
            wg, bg = _lru_gate_weights(w_lru_a[j], b_lru_a[j], w_lru_i[j], b_lru_i[j])
            w_out = w_lru_out[j].astype(BF16)
        for p in range(2):
            x, m = xs[p], _Cond(mod, lng, lnb, i, p == 1)
            if kind == 0:
                x = _pool_layer(x, m, wp, ps)
            elif kind == 1:
                wdq, wq, wdkv, wuk, wuv, wo, wq_rot = mla_w[p]
                if p == 0:
                    qcat, kcat, v, ckv, kr64 = _mla_project(
                        x, m, wdq, gq, wq, wdkv, gkv, wuk, wuv, None, True)
                    ckv_out.append(ckv)
                    krope_out.append(kr64)
                    segments = [(kcat, v)]
                else:
                    qcat, kcat, v = _mla_project(
                        x, m, wdq, gq, wq, wdkv, gkv, wuk, wuv, (wq_rot, cos, sin), False)
                    kr_c = cache_mla_krope[:, j]
                    kr_c = jnp.concatenate([kr_c, kr_c], axis=-1).astype(BF16)
                    kcat_c, v_c = _kv_up(cache_mla_ckv[:, j], kr_c, wuk, wuv)
                    segments = [(kcat_c, v_c), (kcat, v)]
                x = _attn_layer(x, m, qcat, wo, segments)
            else:
                if p == 0:
                    h0 = jnp.zeros((x.shape[0], 2, D_RNN), F32)
                else:
                    h0 = state_lru[:, j]
                x, st = _lru_layer(x, m, w_in, lru_conv_w[j], row(lru_conv_b[j]),
                                   wg, bg, lru_lambda[j], h0, w_out)
                if p == 0:
                    lru_out.append(st)
            xs[p] = _ffn_layer(x, m, win, wout)
    return (xs[0], xs[1], jnp.stack(ckv_out, axis=1), jnp.stack(krope_out, axis=1),
            jnp.stack(lru_out, axis=1))
```

```python
import functools
import math
from typing import NamedTuple

import jax
import jax.numpy as jnp
from jax import lax
from jax.experimental import pallas as pl
from jax.experimental.pallas import tpu as pltpu

F32 = jnp.float32
BF16 = jnp.bfloat16

D_MODEL = 1024
DEPTH = 4
GRID_W = 64
N_MIXERS = 3
POOL_WINDOWS = (2, 4, 8, 16)
POOL_GROUPS = 4
POOL_GC = D_MODEL // POOL_GROUPS
MLA_HEADS = 8
Q_LORA = 384
KV_LORA = 256
QK_NOPE = 128
QK_ROPE = 64
V_HEAD = 128
ROPE_NF = QK_ROPE // 4
ROPE_THETA = 10000.0
MLA_SCALE = (QK_NOPE + QK_ROPE) ** -0.5
D_RNN = D_MODEL
LRU_BLOCKS = 8
LRU_BW = D_RNN // LRU_BLOCKS
CONV_W = 4
CONV_LEFT = 1
LRU_C = 8.0
D_FF = ((8 * D_MODEL // 3 + 255) // 256) * 256
ALPHA = (2.0 * DEPTH) ** 0.25
EPS = 1e-6

LANES = 128
SUBLANES = 8
VMEM_LIMIT_BYTES = 56 * 1024 * 1024
PAD_ROWS = SUBLANES
MOD_ROWS = 16
MOD_CHUNKS = 2
LRU_COLS = 512
SEG_PITCH_PAD = 4
SHORT_SEQ_ROWS = 512
PROJ_TILE = 512
FFN_TILE = 512
ATTN_TILE = 512
QK_CAT = 2 * LANES


def _params(*sem):
    return pltpu.CompilerParams(dimension_semantics=sem, vmem_limit_bytes=VMEM_LIMIT_BYTES)


def _dot(a, b):
    return jnp.dot(a, b, preferred_element_type=F32)


def _layer_norm(y, g, b):
    mu = jnp.mean(y, axis=-1, keepdims=True)
    d = y - mu
    var = jnp.mean(d * d, axis=-1, keepdims=True)
    return d * lax.rsqrt(var + EPS) * g + b


def _gelu_tanh(x):
    c = math.sqrt(2.0 / math.pi)
    return (0.5 * x) * (1.0 + jnp.tanh(x * (c + (c * 0.044715) * (x * x))))


def _rms_norm(y, g):
    return y * lax.rsqrt(jnp.mean(y * y, axis=-1, keepdims=True) + EPS) * g


def _modulate(x, mod_ref):
    return x * (1.0 + mod_ref[1, 0]) + mod_ref[0, 0]


def _finish(x, mixed, mod_ref, lng_ref, lnb_ref):
    return _layer_norm(ALPHA * x + mod_ref[2, 0] * mixed, lng_ref[0], lnb_ref[0])


class _Cond(NamedTuple):
    mod: jax.Array
    lng: jax.Array
    lnb: jax.Array
    layer: int
    latent: bool

    @property
    def arrays(self):
        return self.mod, self.lng, self.lnb


def _cond_specs(cond, sub, nargs):
    k = 2 * cond.layer + sub
    row = (lambda b: 1 + b) if cond.latent else (lambda b: 0)
    if nargs == 1:
        mod = pl.BlockSpec((3, 1, 1, D_MODEL), lambda b: (k, row(b), 0, 0))
        ln = pl.BlockSpec((1, 1, D_MODEL), lambda b: (k, 0, 0))
    else:
        mod = pl.BlockSpec((3, 1, 1, D_MODEL), lambda b, t: (k, row(b), 0, 0))
        ln = pl.BlockSpec((1, 1, D_MODEL), lambda b, t: (k, 0, 0))
    return [mod, ln, ln]


def _const_spec(shape, nargs):
    zeros = (0,) * len(shape)
    if nargs == 1:
        return pl.BlockSpec(shape, lambda b: zeros)
    return pl.BlockSpec(shape, lambda b, t: zeros)


def _mod_kernel(cond_ref, w_ref, b_ref, o_ref):
    s = jax.nn.silu(cond_ref[...]).astype(BF16)
    for k in range(o_ref.shape[1]):
        cols = slice(k * D_MODEL, (k + 1) * D_MODEL)
        res = _dot(s, w_ref[0, :, cols].astype(BF16)) + b_ref[0, k]
        for r in range(MOD_ROWS):
            o_ref[0, k, r] = res[r:r + 1]


def _modulation(cond, w_ada, b_ada):
    d = D_MODEL
    nk = MOD_CHUNKS
    return pl.pallas_call(
        _mod_kernel,
        grid=(DEPTH, 6 // nk),
        in_specs=[
            pl.BlockSpec((MOD_ROWS, d), lambda i, j: (0, 0)),
            pl.BlockSpec((1, d, nk * d), lambda i, j: (i, 0, j)),
            pl.BlockSpec((1, nk, 1, d), lambda i, j: (i, j, 0, 0)),
        ],
        out_specs=pl.BlockSpec((1, nk, MOD_ROWS, 1, d), lambda i, j: (i, j, 0, 0, 0)),
        out_shape=jax.ShapeDtypeStruct((DEPTH, 6, MOD_ROWS, 1, d), F32),
        compiler_params=_params("arbitrary", "arbitrary"),
        name="adaln_modulation",
    )(cond, w_ada, b_ada.reshape(DEPTH, 6, 1, d)).reshape(DEPTH * 6, MOD_ROWS, 1, d)


def _ffn_kernel(x_ref, mod_ref, lng_ref, lnb_ref, win_ref, wout_hbm, o_ref, wout_vmem, sem, *, layer):
    fetch = pltpu.make_async_copy(wout_hbm.at[layer], wout_vmem, sem)
    fetch.start()
    x = x_ref[0]
    h = _modulate(x, mod_ref).astype(BF16)
    ab = _dot(h, win_ref[0].astype(BF16))
    u = (jax.nn.silu(ab[:, :D_FF]) * ab[:, D_FF:]).astype(BF16)
    fetch.wait()
    o_ref[0] = _finish(x, _dot(u, wout_vmem[...].astype(BF16)), mod_ref, lng_ref, lnb_ref)


def _ffn_layer(x, cond, win, wout):
    layer = cond.layer
    shape = x.shape
    d = shape[-1]
    if not cond.latent and shape[1] < FFN_TILE:
        x = x.reshape(-1, FFN_TILE, d)
    b, t, _ = x.shape
    tm = min(FFN_TILE, t)
    slab = lambda r, c: pl.BlockSpec((1, r, c), lambda i, j: (layer, 0, 0), pipeline_mode=pl.Buffered(1))
    return pl.pallas_call(
        functools.partial(_ffn_kernel, layer=layer),
        grid=(b, t // tm),
        in_specs=[
            pl.BlockSpec((1, tm, d), lambda i, j: (i, j, 0)),
            *_cond_specs(cond, 1, 2),
            slab(d, 2 * D_FF),
            pl.BlockSpec(memory_space=pl.ANY),
        ],
        out_specs=pl.BlockSpec((1, tm, d), lambda i, j: (i, j, 0)),
        out_shape=jax.ShapeDtypeStruct(x.shape, x.dtype),
        scratch_shapes=[pltpu.VMEM((D_FF, d), F32), pltpu.SemaphoreType.DMA(())],
        compiler_params=_params("arbitrary", "arbitrary"),
        name="ffn_layer",
    )(x, *cond.arrays, win, wout).reshape(shape)


def _pool_kernel(x_ref, mod_ref, lng_ref, lnb_ref, wp_ref, ps_ref, o_ref):
    for q in range(x_ref.shape[0]):
        o_ref[q] = _pool_sequence(x_ref[q], mod_ref, lng_ref, lnb_ref, wp_ref, ps_ref)


def _pool_sequence(x, mod_ref, lng_ref, lnb_ref, wp_ref, ps_ref):
    t_len = x.shape[0]
    n_pad = t_len + 2 * PAD_ROWS
    h = _modulate(x, mod_ref)
    edge = jnp.zeros((PAD_ROWS, POOL_GC), F32)
    edge_row = lax.broadcasted_iota(jnp.int32, (PAD_ROWS, 1), 0)
    outs = []
    for gi, w in enumerate(POOL_WINDOWS):
        assert w & (w - 1) == 0 and w // 2 <= PAD_ROWS
        hg = h[:, gi * POOL_GC:(gi + 1) * POOL_GC]
        run = jnp.concatenate([edge, hg, edge], axis=0)
        span = 1
        while 2 * span < w:
            run = run + pltpu.roll(run, n_pad - span, axis=0)
            span *= 2
        s = (run + pltpu.roll(run, span, axis=0))[PAD_ROWS:PAD_ROWS + t_len]
        lo, hi = -(w // 2), w - w // 2
        parts = []
        for r0 in (0, t_len - PAD_ROWS):
            t = r0 + edge_row
            cnt = (jnp.minimum(t + hi, t_len) - jnp.maximum(t + lo, 0)).astype(F32)
            parts.append(s[r0:r0 + PAD_ROWS] / cnt - hg[r0:r0 + PAD_ROWS])
        mid = s[PAD_ROWS:t_len - PAD_ROWS] * (1.0 / w) - hg[PAD_ROWS:t_len - PAD_ROWS]
        pooled = jnp.concatenate([parts[0], mid, parts[1]], axis=0)
        outs.append(_dot(pooled.astype(BF16), wp_ref[gi]))
    mixed = jnp.concatenate(outs, axis=1) * ps_ref[...]
    return _finish(x, mixed, mod_ref, lng_ref, lnb_ref)


def _pool_layer(x, cond, wp, ps):
    b, t, d = x.shape
    nq = 1 if cond.latent else max(1, SHORT_SEQ_ROWS // t)
    return pl.pallas_call(
        _pool_kernel,
        grid=(b // nq,),
        in_specs=[
            pl.BlockSpec((nq, t, d), lambda i: (i, 0, 0)),
            *_cond_specs(cond, 0, 1),
            _const_spec((POOL_GROUPS, POOL_GC, POOL_GC), 1),
            _const_spec((1, d), 1),
        ],
        out_specs=pl.BlockSpec((nq, t, d), lambda i: (i, 0, 0)),
        out_shape=jax.ShapeDtypeStruct(x.shape, x.dtype),
        compiler_params=_params("arbitrary"),
        name="pool_layer",
    )(x, *cond.arrays, wp, ps)


def _store_k_cat(kcat_ref, k_nope, k_rope2):
    for hd in range(MLA_HEADS):
        kcat_ref[0, :, hd * QK_CAT:hd * QK_CAT + QK_NOPE] = k_nope[:, hd * QK_NOPE:(hd + 1) * QK_NOPE]
        kcat_ref[0, :, hd * QK_CAT + QK_NOPE:(hd + 1) * QK_CAT] = k_rope2


def _mla_proj_kernel(*refs, rope, emit_cache):
    x_ref, mod_ref, wdq_ref, gq_ref, wq_ref, wdkv_ref, gkv_ref, wuk_ref, wuv_ref = refs[:9]
    refs = refs[9:]
    if rope:
        wqrot_ref, cos_ref, sin_ref = refs[:3]
        refs = refs[3:]
    qcat_ref, kcat_ref, v_ref = refs[:3]
    h = _modulate(x_ref[0], mod_ref).astype(BF16)
    q_lat = _rms_norm(_dot(h, wdq_ref[...]), gq_ref[...]).astype(BF16)
    nope_w = MLA_HEADS * QK_NOPE
    q = _dot(q_lat, wq_ref[...])
    kv = _dot(h, wdkv_ref[...])
    k_rope2 = kv[:, KV_LORA:KV_LORA + 2 * QK_ROPE]
    if rope:
        cos, sin = cos_ref[...], sin_ref[...]
        q_rot = _dot(q_lat, wqrot_ref[...])
        k_rope2 = k_rope2 * cos + kv[:, KV_LORA + 2 * QK_ROPE:] * sin
    lane = lax.broadcasted_iota(jnp.int32, (1, LANES), 1)
    for pair in range(MLA_HEADS // 2):
        both = q[:, nope_w + pair * LANES:nope_w + (pair + 1) * LANES]
        if rope:
            both = both * cos + q_rot[:, pair * LANES:(pair + 1) * LANES] * sin
        for hd, keep in ((2 * pair, lane < QK_ROPE), (2 * pair + 1, lane >= QK_ROPE)):
            c0 = hd * QK_CAT
            qcat_ref[0, :, c0:c0 + QK_NOPE] = q[:, hd * QK_NOPE:(hd + 1) * QK_NOPE].astype(BF16)
            qcat_ref[0, :, c0 + QK_NOPE:c0 + QK_CAT] = jnp.where(keep, both, 0.0).astype(BF16)
    c_kv = _rms_norm(kv[:, :KV_LORA], gkv_ref[...])
    c_kv16 = c_kv.astype(BF16)
    _store_k_cat(kcat_ref, _dot(c_kv16, wuk_ref[...]).astype(BF16), k_rope2.astype(BF16))
    v_ref[0] = _dot(c_kv16, wuv_ref[...]).astype(BF16)
    if emit_cache:
        ckv_out_ref, kr_out_ref = refs[3:5]
        ckv_out_ref[0] = c_kv
        kr_out_ref[0] = kv[:, KV_LORA:KV_LORA + QK_ROPE]


def _mla_project(x, cond, wdq, gq, wq, wdkv, gkv, wuk, wuv, rope_args, emit_cache):
    b0, t0, d = x.shape
    rope = rope_args is not None
    if not cond.latent and not rope and t0 < PROJ_TILE:
        x = x.reshape(-1, PROJ_TILE, d)
    b, t, _ = x.shape
    tm = min(PROJ_TILE, t)
    tok = lambda w: pl.BlockSpec((1, tm, w), lambda i, j: (i, j, 0))
    args = [x, cond.mod, wdq, gq, wq, wdkv, gkv, wuk, wuv]
    in_specs = [tok(d), _cond_specs(cond, 0, 2)[0]] + [_const_spec(a.shape, 2) for a in args[2:]]
    if rope:
        wqrot, cos, sin = rope_args
        args += [wqrot, cos, sin]
        in_specs += [_const_spec(wqrot.shape, 2)] + [pl.BlockSpec((tm, LANES), lambda i, j: (j, 0))] * 2
    widths = [MLA_HEADS * QK_CAT, MLA_HEADS * QK_CAT, MLA_HEADS * V_HEAD]
    out_specs = [tok(w) for w in widths]
    out_shape = [jax.ShapeDtypeStruct((b, t, w), BF16) for w in widths]
    if emit_cache:
        out_specs += [tok(KV_LORA), tok(QK_ROPE)]
        out_shape += [jax.ShapeDtypeStruct((b, t, KV_LORA), F32),
                      jax.ShapeDtypeStruct((b, t, QK_ROPE), F32)]
    outs = pl.pallas_call(
        functools.partial(_mla_proj_kernel, rope=rope, emit_cache=emit_cache),
        grid=(b, t // tm),
        in_specs=in_specs,
        out_specs=out_specs,
        out_shape=out_shape,
        compiler_params=_params("arbitrary", "arbitrary"),
        name="mla_project",
    )(*args)
    return [o.reshape(b0, t0, o.shape[-1]) for o in outs]


def _kv_up_kernel(ckv_ref, kr2_ref, wuk_ref, wuv_ref, kcat_ref, v_ref):
    c = ckv_ref[0].astype(BF16)
    _store_k_cat(kcat_ref, _dot(c, wuk_ref[...]).astype(BF16), kr2_ref[0])
    v_ref[0] = _dot(c, wuv_ref[...]).astype(BF16)


def _kv_up(ckv, kr2, wuk, wuv):
    b, s, r = ckv.shape
    seq = lambda w: pl.BlockSpec((1, s, w), lambda i: (i, 0, 0))
    widths = [MLA_HEADS * QK_CAT, MLA_HEADS * V_HEAD]
    return pl.pallas_call(
        _kv_up_kernel,
        grid=(b,),
        in_specs=[seq(r), seq(kr2.shape[2]), _const_spec(wuk.shape, 1), _const_spec(wuv.shape, 1)],
        out_specs=[seq(w) for w in widths],
        out_shape=[jax.ShapeDtypeStruct((b, s, w), BF16) for w in widths],
        compiler_params=_params("arbitrary"),
        name="mla_cache_kv_up",
    )(ckv, kr2, wuk, wuv)


def _attn_kernel(*refs, n_seg):
    x_ref, mod_ref, lng_ref, lnb_ref, q_ref, wo_ref = refs[:6]
    seg_refs = [refs[6 + 2 * i:8 + 2 * i] for i in range(n_seg)]
    o_ref = refs[6 + 2 * n_seg]
    nt = (((1,), (1,)), ((), ()))
    exp2_scale = MLA_SCALE * math.log2(math.e)
    for sq in range(x_ref.shape[0]):
        heads = []
        for hd in range(MLA_HEADS):
            q = q_ref[sq, :, hd * QK_CAT:(hd + 1) * QK_CAT]
            scores = [lax.dot_general(q, k_ref[sq, :, hd * QK_CAT:(hd + 1) * QK_CAT], nt,
                                      preferred_element_type=F32) for k_ref, _ in seg_refs]
            m = functools.reduce(jnp.maximum, [jnp.max(s, axis=-1, keepdims=True) for s in scores])
            es = [jnp.exp2((s - m) * exp2_scale) for s in scores]
            total = functools.reduce(lambda a, b: a + b, [jnp.sum(e, axis=-1, keepdims=True) for e in es])
            o = None
            for e, (_, v_ref) in zip(es, seg_refs):
                part = _dot(e.astype(BF16), v_ref[sq, :, hd * V_HEAD:(hd + 1) * V_HEAD])
                o = part if o is None else o + part
            heads.append((o * (1.0 / total)).astype(BF16))
        mixed = _dot(jnp.concatenate(heads, axis=1), wo_ref[...])
        o_ref[sq] = _finish(x_ref[sq], mixed, mod_ref, lng_ref, lnb_ref)


def _attn_layer(x, cond, qcat, wo, segments):
    b, t, d = x.shape
    tq = min(ATTN_TILE, t)
    nq = 1 if cond.latent else max(1, SHORT_SEQ_ROWS // t)
    tok = lambda w: pl.BlockSpec((nq, tq, w), lambda i, j: (i, j, 0))
    in_specs = [tok(d), *_cond_specs(cond, 0, 2),
                tok(qcat.shape[2]), _const_spec(wo.shape, 2)]
    args = [x, *cond.arrays, qcat, wo]
    for seg in segments:
        for a in seg:
            in_specs.append(pl.BlockSpec((nq,) + a.shape[1:], lambda i, j: (i, 0, 0)))
            args.append(a)
    return pl.pallas_call(
        functools.partial(_attn_kernel, n_seg=len(segments)),
        grid=(b // nq, t // tq),
        in_specs=in_specs,
        out_specs=tok(d),
        out_shape=jax.ShapeDtypeStruct(x.shape, x.dtype),
        compiler_params=_params("arbitrary", "arbitrary"),
        name="mla_attention",
    )(*args)


def _lru_kernel(x_ref, mod_ref, lng_ref, lnb_ref, wu_ref, wy_ref, cw_ref, cb_ref, wg_ref, bg_ref,
                lam_ref, h0_ref, wout_ref, o_ref, st_ref,
                perm_scr, h_scr, acc_scr, pad_scr, af_scr, bf_scr, ab_scr, bb_scr):
    t_len = x_ref.shape[1]
    n_blk = t_len // SUBLANES
    pitch = n_blk + SEG_PITCH_PAD
    n_slab = D_MODEL // LANES
    cols = wu_ref.shape[1]
    j = pl.program_id(1)
    slab = lambda c: slice(c * LANES, (c + 1) * LANES)

    @pl.when(j == 0)
    def _():
        for c in range(n_slab):
            for s in range(SUBLANES):
                perm_scr[c, s * pitch:s * pitch + n_blk, :] = x_ref[0, s * n_blk:(s + 1) * n_blk, slab(c)]

        def gather(k, carry):
            r = pl.multiple_of(k * SUBLANES, SUBLANES)
            for c in range(n_slab):
                acc_scr[pl.ds(r, SUBLANES), slab(c)] = perm_scr[c, pl.ds(k, SUBLANES, stride=pitch), :]
            return carry

        lax.fori_loop(0, n_blk, gather, 0, unroll=2)
        h_scr[...] = _modulate(acc_scr[...], mod_ref).astype(BF16)
        acc_scr[...] = jnp.zeros_like(acc_scr)

    h = h_scr[...]
    u = _dot(h, wu_ref[...])
    y = _gelu_tanh(_dot(h, wy_ref[...]))

    sub = lax.broadcasted_iota(jnp.int32, (SUBLANES, cols), 0)
    lead = CONV_LEFT * SUBLANES
    for i in range(1, CONV_LEFT + 1):
        blk = u[(n_blk - i) * SUBLANES:(n_blk - i + 1) * SUBLANES]
        pad_scr[lead - i * SUBLANES:lead - (i - 1) * SUBLANES, :] = jnp.where(
            sub >= 1, pltpu.roll(blk, 1, axis=0), 0.0)
    pad_scr[lead:lead + t_len, :] = u
    for i in range(CONV_W - 1 - CONV_LEFT):
        blk = u[i * SUBLANES:(i + 1) * SUBLANES]
        pad_scr[lead + t_len + i * SUBLANES:lead + t_len + (i + 1) * SUBLANES, :] = jnp.where(
            sub < SUBLANES - 1, pltpu.roll(blk, SUBLANES - 1, axis=0), 0.0)
    uc = None
    for k in range(CONV_W):
        term = pad_scr[k * SUBLANES:k * SUBLANES + t_len, :] * cw_ref[k:k + 1, :]
        uc = term if uc is None else uc + term
    uc = uc + cb_ref[...]

    a_scr = (af_scr, ab_scr)
    b_scr = (bf_scr, bb_scr)
    for bl in range(cols // LRU_BW):
        c0 = bl * LRU_BW
        ub = uc[:, c0:c0 + LRU_BW]
        tg = jnp.tanh(_dot(ub.astype(BF16), wg_ref[bl]) + bg_ref[bl])
        for dr in range(2):
            tr = tg[:, dr * LRU_BW:(dr + 1) * LRU_BW]
            ti = tg[:, (2 + dr) * LRU_BW:(3 + dr) * LRU_BW]
            half_sp = (-0.25 * LRU_C) * jax.nn.softplus(-lam_ref[dr:dr + 1, c0:c0 + LRU_BW])
            t = jnp.tanh(half_sp * (1.0 + tr))
            inv = 1.0 / (1.0 - t)
            nt = -t
            root = jnp.where(nt > 0.0, nt * lax.rsqrt(nt), 0.0)
            a_scr[dr][:, c0:c0 + LRU_BW] = (1.0 + t) * inv
            b_scr[dr][:, c0:c0 + LRU_BW] = (root * inv) * ((1.0 + ti) * ub)

    def step(k, carry):
        a_f, b_f, a_b, b_b = carry
        rf = pl.multiple_of(k * SUBLANES, SUBLANES)
        rb = pl.multiple_of((n_blk - 1 - k) * SUBLANES, SUBLANES)
        a = af_scr[pl.ds(rf, SUBLANES), :]
        b_f = a * b_f + bf_scr[pl.ds(rf, SUBLANES), :]
        a_f = a * a_f
        af_scr[pl.ds(rf, SUBLANES), :] = a_f
        bf_scr[pl.ds(rf, SUBLANES), :] = b_f
        a = ab_scr[pl.ds(rb, SUBLANES), :]
        b_b = a * b_b + bb_scr[pl.ds(rb, SUBLANES), :]
        a_b = a * a_b
        ab_scr[pl.ds(rb, SUBLANES), :] = a_b
        bb_scr[pl.ds(rb, SUBLANES), :] = b_b
        return a_f, b_f, a_b, b_b

    one = jnp.ones((SUBLANES, cols), F32)
    zero = jnp.zeros((SUBLANES, cols), F32)
    a_f, b_f, a_b, b_b = lax.fori_loop(0, n_blk, step, (one, zero, one, zero), unroll=4)

    hf = h0_ref[0, 0:1, :]
    hf_in = zero
    for s in range(SUBLANES):
        hf_in = jnp.where(sub == s, hf, hf_in)
        hf = a_f[s:s + 1] * hf + b_f[s:s + 1]
    hb = h0_ref[0, 1:2, :]
    hb_in = zero
    for s in reversed(range(SUBLANES)):
        hb_in = jnp.where(sub == s, hb, hb_in)
        hb = a_b[s:s + 1] * hb + b_b[s:s + 1]
    st_ref[0, 0:1, :] = hf
    st_ref[0, 1:2, :] = hb

    blocks = lambda ref: ref[...].reshape(n_blk, SUBLANES, cols)
    h_sum = (blocks(af_scr) * hf_in + blocks(bf_scr)) + (blocks(ab_scr) * hb_in + blocks(bb_scr))
    mixed = (h_sum.reshape(t_len, cols) * y).astype(BF16)
    acc_scr[...] += _dot(mixed, wout_ref[...])

    @pl.when(j == pl.num_programs(1) - 1)
    def _():
        def scatter(k, carry):
            r = pl.multiple_of(k * SUBLANES, SUBLANES)
            for c in range(n_slab):
                perm_scr[c, pl.ds(k, SUBLANES, stride=pitch), :] = acc_scr[pl.ds(r, SUBLANES), slab(c)]
            return carry

        lax.fori_loop(0, n_blk, scatter, 0, unroll=2)
        for s in range(SUBLANES):
            rows = slice(s * n_blk, (s + 1) * n_blk)
            mixed_s = jnp.concatenate(
                [perm_scr[c, s * pitch:s * pitch + n_blk, :] for c in range(n_slab)], axis=1)
            o_ref[0, rows, :] = _finish(x_ref[0, rows, :], mixed_s, mod_ref, lng_ref, lnb_ref)


def _lru_layer(x, cond, w_in, conv_w, conv_b, wg, bg, lam, h0, w_out):
    b, t, d = x.shape
    cw = LRU_COLS if t > SHORT_SEQ_ROWS else D_RNN
    ncb = D_RNN // cw
    bpc = cw // LRU_BW
    pitch = t // SUBLANES + SEG_PITCH_PAD
    seq = lambda w: pl.BlockSpec((1, t, w), lambda i, j: (i, 0, 0))
    col = lambda rows: pl.BlockSpec((rows, cw), lambda i, j: (0, j))
    out, state = pl.pallas_call(
        _lru_kernel,
        grid=(b, ncb),
        in_specs=[
            seq(d),
            *_cond_specs(cond, 0, 2),
            pl.BlockSpec((d, cw), lambda i, j: (0, j)),
            pl.BlockSpec((d, cw), lambda i, j: (0, ncb + j)),
            col(CONV_W),
            col(1),
            pl.BlockSpec((bpc, LRU_BW, 4 * LRU_BW), lambda i, j: (j, 0, 0)),
            pl.BlockSpec((bpc, 1, 4 * LRU_BW), lambda i, j: (j, 0, 0)),
            col(2),
            pl.BlockSpec((1, 2, cw), lambda i, j: (i, 0, j)),
            pl.BlockSpec((cw, d), lambda i, j: (j, 0)),
        ],
        out_specs=[seq(d), pl.BlockSpec((1, 2, cw), lambda i, j: (i, 0, j))],
        out_shape=[jax.ShapeDtypeStruct(x.shape, x.dtype),
                   jax.ShapeDtypeStruct((b, 2, D_RNN), x.dtype)],
        scratch_shapes=[pltpu.VMEM((d // LANES, SUBLANES * pitch, LANES), F32),
                        pltpu.VMEM((t, d), BF16), pltpu.VMEM((t, d), F32),
                        pltpu.VMEM((t + (CONV_W - 1) * SUBLANES, cw), F32)]
                       + [pltpu.VMEM((t, cw), F32)] * 4,
        compiler_params=_params("arbitrary", "arbitrary"),
        name="lru_layer",
    )(x, *cond.arrays, w_in, w_in, conv_w, conv_b, wg, bg, lam, h0, w_out)
    return out, state


def _rope_rotate_cols(w):
    nf = ROPE_NF
    return jnp.concatenate([-w[..., nf:2 * nf], w[..., :nf], -w[..., 3 * nf:], w[..., 2 * nf:3 * nf]], axis=-1)


def _rope_tables(t_len):
    t = jnp.arange(t_len)
    rows = (t // GRID_W).astype(F32)
    cols = (t % GRID_W).astype(F32)
    inv = ROPE_THETA ** (-jnp.arange(ROPE_NF, dtype=F32) / ROPE_NF)
    ar, ac = rows[:, None] * inv, cols[:, None] * inv
    cos = jnp.concatenate([jnp.cos(ar), jnp.cos(ar), jnp.cos(ac), jnp.cos(ac)], axis=1)
    sin = jnp.concatenate([jnp.sin(ar), jnp.sin(ar), jnp.sin(ac), jnp.sin(ac)], axis=1)
    return jnp.tile(cos, (1, 2)), jnp.tile(sin, (1, 2))


def _mla_weights(w_dq, w_uq, w_dkv, w_uk, w_uv, w_o, rope):
    r = w_uq.shape[0]
    w_nope, w_rope = w_uq[:, :, :QK_NOPE], w_uq[:, :, QK_NOPE:]
    wq = jnp.concatenate([w_nope.reshape(r, -1), w_rope.reshape(r, -1)], axis=1).astype(BF16)
    wk = w_dkv[:, KV_LORA:]
    wdkv = [w_dkv[:, :KV_LORA], wk, wk]
    wq_rot = None
    if rope:
        wq_rot = _rope_rotate_cols(w_rope).reshape(r, -1).astype(BF16)
        wk_rot = _rope_rotate_cols(wk)
        wdkv += [wk_rot, wk_rot]
    return (w_dq.astype(BF16), wq, jnp.concatenate(wdkv, axis=1).astype(BF16),
            w_uk.reshape(KV_LORA, -1).astype(BF16), w_uv.reshape(KV_LORA, -1).astype(BF16),
            w_o.astype(BF16), wq_rot)


def _lru_gate_weights(w_a, b_a, w_i, b_i):
    wg = (0.5 * jnp.concatenate([w_a[0], w_a[1], w_i[0], w_i[1]], axis=-1)).astype(BF16)
    blk = lambda v: v.reshape(LRU_BLOCKS, 1, LRU_BW)
    bg = 0.5 * jnp.concatenate([blk(b_a[0]), blk(b_a[1]), blk(b_i[0]), blk(b_i[1])], axis=-1)
    return wg, bg


def kernel(x_prompt, x_sample, cache_mla_ckv, cache_mla_krope, state_lru, c, c_ctx, w_ada, b_ada, ln_g, ln_b, w_ffn_in, w_ffn_out, w_pool, pool_scale, w_dq, g_q, w_uq, w_dkv, g_kv, w_uk, w_uv, w_mla_o, w_lru_in, lru_conv_w, lru_conv_b, w_lru_a, b_lru_a, w_lru_i, b_lru_i, lru_lambda, w_lru_out):
    d = D_MODEL
    n_lat = c.shape[0]
    cond = jnp.concatenate([c_ctx[None], c, jnp.zeros((MOD_ROWS - 1 - n_lat, d), F32)], axis=0)
    mod = _modulation(cond, w_ada, b_ada)
    lng, lnb = ln_g.reshape(DEPTH * 2, 1, d), ln_b.reshape(DEPTH * 2, 1, d)

    xs = [x_prompt, x_sample]
    ckv_out, krope_out, lru_out = [], [], []
    cos, sin = _rope_tables(x_sample.shape[1])
    win, wout = w_ffn_in, w_ffn_out
    for i in range(DEPTH):
        kind, j = i % N_MIXERS, i // N_MIXERS
        row = lambda v: v.reshape(1, -1)
        if kind == 0:
            wp, ps = w_pool[j].astype(BF16), row(pool_scale[j])
        elif kind == 1:
            mla_w = [_mla_weights(w_dq[j], w_uq[j], w_dkv[j], w_uk[j], w_uv[j], w_mla_o[j], rope)
                     for rope in (False, True)]
            gq, gkv = row(g_q[j]), row(g_kv[j])
        else:
            w_in = w_lru_in[j].astype(BF16)
            wg, bg = _lru_gate_weights(w_lru_a[j], b_lru_a[j], w_lru_i[j], b_lru_i[j])
            w_out = w_lru_out[j].astype(BF16)
        for p in range(2):
            x, m = xs[p], _Cond(mod, lng, lnb, i, p == 1)
            if kind == 0:
                x = _pool_layer(x, m, wp, ps)
            elif kind == 1:
                wdq, wq, wdkv, wuk, wuv, wo, wq_rot = mla_w[p]
                if p == 0:
                    qcat, kcat, v, ckv, kr64 = _mla_project(
                        x, m, wdq, gq, wq, wdkv, gkv, wuk, wuv, None, True)
                    ckv_out.append(ckv)
                    krope_out.append(kr64)
                    segments = [(kcat, v)]
                else:
                    qcat, kcat, v = _mla_project(
                        x, m, wdq, gq, wq, wdkv, gkv, wuk, wuv, (wq_rot, cos, sin), False)
                    kr_c = cache_mla_krope[:, j]
                    kr_c = jnp.concatenate([kr_c, kr_c], axis=-1).astype(BF16)
                    kcat_c, v_c = _kv_up(cache_mla_ckv[:, j], kr_c, wuk, wuv)
                    segments = [(kcat_c, v_c), (kcat, v)]
                x = _attn_layer(x, m, qcat, wo, segments)
            else:
                if p == 0:
                    h0 = jnp.zeros((x.shape[0], 2, D_RNN), F32)
                else:
                    h0 = state_lru[:, j]
                x, st = _lru_layer(x, m, w_in, lru_conv_w[j], row(lru_conv_b[j]),
                                   wg, bg, lru_lambda[j], h0, w_out)
                if p == 0:
                    lru_out.append(st)
            xs[p] = _ffn_layer(x, m, win, wout)
    return (xs[0], xs[1], jnp.stack(ckv_out, axis=1), jnp.stack(krope_out, axis=1),
            jnp.stack(lru_out, axis=1))
```

```python
import functools
import math
from typing import NamedTuple

import jax
import jax.numpy as jnp
from jax import lax
from jax.experimental import pallas as pl
from jax.experimental.pallas import tpu as pltpu

F32 = jnp.float32
BF16 = jnp.bfloat16

D_MODEL = 1024
DEPTH = 4
GRID_W = 64
N_MIXERS = 3
POOL_WINDOWS = (2, 4, 8, 16)
POOL_GROUPS = 4
POOL_GC = D_MODEL // POOL_GROUPS
MLA_HEADS = 8
Q_LORA = 384
KV_LORA = 256
QK_NOPE = 128
QK_ROPE = 64
V_HEAD = 128
ROPE_NF = QK_ROPE // 4
ROPE_THETA = 10000.0
MLA_SCALE = (QK_NOPE + QK_ROPE) ** -0.5
D_RNN = D_MODEL
LRU_BLOCKS = 8
LRU_BW = D_RNN // LRU_BLOCKS
CONV_W = 4
CONV_LEFT = 1
LRU_C = 8.0
D_FF = ((8 * D_MODEL // 3 + 255) // 256) * 256
ALPHA = (2.0 * DEPTH) ** 0.25
EPS = 1e-6

LANES = 128
SUBLANES = 8
VMEM_LIMIT_BYTES = 56 * 1024 * 1024
PAD_ROWS = SUBLANES
MOD_ROWS = 16
MOD_CHUNKS = 2
MOD_RING = 3
LRU_COLS = 512
SEG_PITCH_PAD = 4
SHORT_SEQ_ROWS = 512
PROJ_TILE = 512
FFN_TILE = 512
ATTN_TILE = 512
QK_CAT = 2 * LANES


def _params(*sem):
    return pltpu.CompilerParams(dimension_semantics=sem, vmem_limit_bytes=VMEM_LIMIT_BYTES)


def _dot(a, b):
    return jnp.dot(a, b, preferred_element_type=F32)


def _layer_norm(y, g, b):
    mu = jnp.mean(y, axis=-1, keepdims=True)
    d = y - mu
    var = jnp.mean(d * d, axis=-1, keepdims=True)
    return d * lax.rsqrt(var + EPS) * g + b


def _gelu_tanh(x):
    c = math.sqrt(2.0 / math.pi)
    return (0.5 * x) * (1.0 + jnp.tanh(x * (c + (c * 0.044715) * (x * x))))


def _rms_norm(y, g):
    return y * lax.rsqrt(jnp.mean(y * y, axis=-1, keepdims=True) + EPS) * g


def _modulate(x, mod_ref):
    return x * (1.0 + mod_ref[1, 0]) + mod_ref[0, 0]


def _finish(x, mixed, mod_ref, lng_ref, lnb_ref):
    return _layer_norm(ALPHA * x + mod_ref[2, 0] * mixed, lng_ref[0], lnb_ref[0])


class _Cond(NamedTuple):
    mod: jax.Array
    lng: jax.Array
    lnb: jax.Array
    layer: int
    latent: bool

    @property
    def arrays(self):
        return self.mod, self.lng, self.lnb


def _cond_specs(cond, sub, nargs):
    k = 2 * cond.layer + sub
    row = (lambda b: 1 + b) if cond.latent else (lambda b: 0)
    if nargs == 1:
        mod = pl.BlockSpec((3, 1, 1, D_MODEL), lambda b: (k, row(b), 0, 0))
        ln = pl.BlockSpec((1, 1, D_MODEL), lambda b: (k, 0, 0))
    else:
        mod = pl.BlockSpec((3, 1, 1, D_MODEL), lambda b, t: (k, row(b), 0, 0))
        ln = pl.BlockSpec((1, 1, D_MODEL), lambda b, t: (k, 0, 0))
    return [mod, ln, ln]


def _const_spec(shape, nargs):
    zeros = (0,) * len(shape)
    if nargs == 1:
        return pl.BlockSpec(shape, lambda b: zeros)
    return pl.BlockSpec(shape, lambda b, t: zeros)


def _mod_kernel(cond_ref, w_hbm, b_ref, o_ref, w_ring, sems):
    nk = o_ref.shape[1]
    nj = 6 // nk
    step = pl.program_id(0) * nj + pl.program_id(1)
    n_steps = DEPTH * nj

    def block_copy(t):
        slot = t % MOD_RING
        cols = pl.ds(pl.multiple_of((t % nj) * (nk * D_MODEL), LANES), nk * D_MODEL)
        return pltpu.make_async_copy(w_hbm.at[t // nj, :, cols], w_ring.at[slot], sems.at[slot])

    @pl.when(step == 0)
    def _():
        for t in range(MOD_RING - 1):
            block_copy(t).start()

    @pl.when(step + (MOD_RING - 1) < n_steps)
    def _():
        block_copy(step + (MOD_RING - 1)).start()

    block_copy(step).wait()
    w_ref = w_ring.at[step % MOD_RING]
    s = jax.nn.silu(cond_ref[...]).astype(BF16)
    for k in range(nk):
        cols = slice(k * D_MODEL, (k + 1) * D_MODEL)
        res = _dot(s, w_ref[:, cols].astype(BF16)) + b_ref[0, k]
        for r in range(MOD_ROWS):
            o_ref[0, k, r] = res[r:r + 1]


def _modulation(cond, w_ada, b_ada):
    d = D_MODEL
    nk = MOD_CHUNKS
    return pl.pallas_call(
        _mod_kernel,
        grid=(DEPTH, 6 // nk),
        in_specs=[
            pl.BlockSpec((MOD_ROWS, d), lambda i, j: (0, 0)),
            pl.BlockSpec(memory_space=pl.ANY),
            pl.BlockSpec((1, nk, 1, d), lambda i, j: (i, j, 0, 0)),
        ],
        out_specs=pl.BlockSpec((1, nk, MOD_ROWS, 1, d), lambda i, j: (i, j, 0, 0, 0)),
        out_shape=jax.ShapeDtypeStruct((DEPTH, 6, MOD_ROWS, 1, d), F32),
        scratch_shapes=[pltpu.VMEM((MOD_RING, d, nk * d), F32), pltpu.SemaphoreType.DMA((MOD_RING,))],
        compiler_params=_params("arbitrary", "arbitrary"),
        name="adaln_modulation",
    )(cond, w_ada, b_ada.reshape(DEPTH, 6, 1, d)).reshape(DEPTH * 6, MOD_ROWS, 1, d)


def _ffn_kernel(x_ref, mod_ref, lng_ref, lnb_ref, win_ref, wout_hbm, o_ref, wout_vmem, sem, *, layer):
    fetch = pltpu.make_async_copy(wout_hbm.at[layer], wout_vmem, sem)
    fetch.start()
    x = x_ref[0]
    h = _modulate(x, mod_ref).astype(BF16)
    ab = _dot(h, win_ref[0].astype(BF16))
    u = (jax.nn.silu(ab[:, :D_FF]) * ab[:, D_FF:]).astype(BF16)
    fetch.wait()
    o_ref[0] = _finish(x, _dot(u, wout_vmem[...].astype(BF16)), mod_ref, lng_ref, lnb_ref)


def _ffn_layer(x, cond, win, wout):
    layer = cond.layer
    shape = x.shape
    d = shape[-1]
    if not cond.latent and shape[1] < FFN_TILE:
        x = x.reshape(-1, FFN_TILE, d)
    b, t, _ = x.shape
    tm = min(FFN_TILE, t)
    slab = lambda r, c: pl.BlockSpec((1, r, c), lambda i, j: (layer, 0, 0), pipeline_mode=pl.Buffered(1))
    return pl.pallas_call(
        functools.partial(_ffn_kernel, layer=layer),
        grid=(b, t // tm),
        in_specs=[
            pl.BlockSpec((1, tm, d), lambda i, j: (i, j, 0)),
            *_cond_specs(cond, 1, 2),
            slab(d, 2 * D_FF),
            pl.BlockSpec(memory_space=pl.ANY),
        ],
        out_specs=pl.BlockSpec((1, tm, d), lambda i, j: (i, j, 0)),
        out_shape=jax.ShapeDtypeStruct(x.shape, x.dtype),
        scratch_shapes=[pltpu.VMEM((D_FF, d), F32), pltpu.SemaphoreType.DMA(())],
        compiler_params=_params("arbitrary", "arbitrary"),
        name="ffn_layer",
    )(x, *cond.arrays, win, wout).reshape(shape)


def _pool_kernel(x_ref, mod_ref, lng_ref, lnb_ref, wp_ref, ps_ref, o_ref):
    for q in range(x_ref.shape[0]):
        o_ref[q] = _pool_sequence(x_ref[q], mod_ref, lng_ref, lnb_ref, wp_ref, ps_ref)


def _pool_sequence(x, mod_ref, lng_ref, lnb_ref, wp_ref, ps_ref):
    t_len = x.shape[0]
    n_pad = t_len + 2 * PAD_ROWS
    h = _modulate(x, mod_ref)
    edge = jnp.zeros((PAD_ROWS, POOL_GC), F32)
    edge_row = lax.broadcasted_iota(jnp.int32, (PAD_ROWS, 1), 0)
    outs = []
    for gi, w in enumerate(POOL_WINDOWS):
        assert w & (w - 1) == 0 and w // 2 <= PAD_ROWS
        hg = h[:, gi * POOL_GC:(gi + 1) * POOL_GC]
        run = jnp.concatenate([edge, hg, edge], axis=0)
        span = 1
        while 2 * span < w:
            run = run + pltpu.roll(run, n_pad - span, axis=0)
            span *= 2
        s = (run + pltpu.roll(run, span, axis=0))[PAD_ROWS:PAD_ROWS + t_len]
        lo, hi = -(w // 2), w - w // 2
        parts = []
        for r0 in (0, t_len - PAD_ROWS):
            t = r0 + edge_row
            cnt = (jnp.minimum(t + hi, t_len) - jnp.maximum(t + lo, 0)).astype(F32)
            parts.append(s[r0:r0 + PAD_ROWS] / cnt - hg[r0:r0 + PAD_ROWS])
        mid = s[PAD_ROWS:t_len - PAD_ROWS] * (1.0 / w) - hg[PAD_ROWS:t_len - PAD_ROWS]
        pooled = jnp.concatenate([parts[0], mid, parts[1]], axis=0)
        outs.append(_dot(pooled.astype(BF16), wp_ref[gi]))
    mixed = jnp.concatenate(outs, axis=1) * ps_ref[...]
    return _finish(x, mixed, mod_ref, lng_ref, lnb_ref)


def _pool_layer(x, cond, wp, ps):
    b, t, d = x.shape
    nq = 1 if cond.latent else max(1, SHORT_SEQ_ROWS // t)
    return pl.pallas_call(
        _pool_kernel,
        grid=(b // nq,),
        in_specs=[
            pl.BlockSpec((nq, t, d), lambda i: (i, 0, 0)),
            *_cond_specs(cond, 0, 1),
            _const_spec((POOL_GROUPS, POOL_GC, POOL_GC), 1),
            _const_spec((1, d), 1),
        ],
        out_specs=pl.BlockSpec((nq, t, d), lambda i: (i, 0, 0)),
        out_shape=jax.ShapeDtypeStruct(x.shape, x.dtype),
        compiler_params=_params("arbitrary"),
        name="pool_layer",
    )(x, *cond.arrays, wp, ps)


def _store_k_cat(kcat_ref, k_nope, k_rope2):
    for hd in range(MLA_HEADS):
        kcat_ref[0, :, hd * QK_CAT:hd * QK_CAT + QK_NOPE] = k_nope[:, hd * QK_NOPE:(hd + 1) * QK_NOPE]
        kcat_ref[0, :, hd * QK_CAT + QK_NOPE:(hd + 1) * QK_CAT] = k_rope2


def _mla_proj_kernel(*refs, rope, emit_cache):
    x_ref, mod_ref, wdq_ref, gq_ref, wq_ref, wdkv_ref, gkv_ref, wuk_ref, wuv_ref = refs[:9]
    refs = refs[9:]
    if rope:
        wqrot_ref, cos_ref, sin_ref = refs[:3]
        refs = refs[3:]
    qcat_ref, kcat_ref, v_ref = refs[:3]
    h = _modulate(x_ref[0], mod_ref).astype(BF16)
    q_lat = _rms_norm(_dot(h, wdq_ref[...]), gq_ref[...]).astype(BF16)
    nope_w = MLA_HEADS * QK_NOPE
    q = _dot(q_lat, wq_ref[...])
    kv = _dot(h, wdkv_ref[...])
    k_rope2 = kv[:, KV_LORA:KV_LORA + 2 * QK_ROPE]
    if rope:
        cos, sin = cos_ref[...], sin_ref[...]
        q_rot = _dot(q_lat, wqrot_ref[...])
        k_rope2 = k_rope2 * cos + kv[:, KV_LORA + 2 * QK_ROPE:] * sin
    lane = lax.broadcasted_iota(jnp.int32, (1, LANES), 1)
    for pair in range(MLA_HEADS // 2):
        both = q[:, nope_w + pair * LANES:nope_w + (pair + 1) * LANES]
        if rope:
            both = both * cos + q_rot[:, pair * LANES:(pair + 1) * LANES] * sin
        for hd, keep in ((2 * pair, lane < QK_ROPE), (2 * pair + 1, lane >= QK_ROPE)):
            c0 = hd * QK_CAT
            qcat_ref[0, :, c0:c0 + QK_NOPE] = q[:, hd * QK_NOPE:(hd + 1) * QK_NOPE].astype(BF16)
            qcat_ref[0, :, c0 + QK_NOPE:c0 + QK_CAT] = jnp.where(keep, both, 0.0).astype(BF16)
    c_kv = _rms_norm(kv[:, :KV_LORA], gkv_ref[...])
    c_kv16 = c_kv.astype(BF16)
    _store_k_cat(kcat_ref, _dot(c_kv16, wuk_ref[...]).astype(BF16), k_rope2.astype(BF16))
    v_ref[0] = _dot(c_kv16, wuv_ref[...]).astype(BF16)
    if emit_cache:
        ckv_out_ref, kr_out_ref = refs[3:5]
        ckv_out_ref[0] = c_kv
        kr_out_ref[0] = kv[:, KV_LORA:KV_LORA + QK_ROPE]


def _mla_project(x, cond, wdq, gq, wq, wdkv, gkv, wuk, wuv, rope_args, emit_cache):
    b0, t0, d = x.shape
    rope = rope_args is not None
    if not cond.latent and not rope and t0 < PROJ_TILE:
        x = x.reshape(-1, PROJ_TILE, d)
    b, t, _ = x.shape
    tm = min(PROJ_TILE, t)
    tok = lambda w: pl.BlockSpec((1, tm, w), lambda i, j: (i, j, 0))
    args = [x, cond.mod, wdq, gq, wq, wdkv, gkv, wuk, wuv]
    in_specs = [tok(d), _cond_specs(cond, 0, 2)[0]] + [_const_spec(a.shape, 2) for a in args[2:]]
    if rope:
        wqrot, cos, sin = rope_args
        args += [wqrot, cos, sin]
        in_specs += [_const_spec(wqrot.shape, 2)] + [pl.BlockSpec((tm, LANES), lambda i, j: (j, 0))] * 2
    widths = [MLA_HEADS * QK_CAT, MLA_HEADS * QK_CAT, MLA_HEADS * V_HEAD]
    out_specs = [tok(w) for w in widths]
    out_shape = [jax.ShapeDtypeStruct((b, t, w), BF16) for w in widths]
    if emit_cache:
        out_specs += [tok(KV_LORA), tok(QK_ROPE)]
        out_shape += [jax.ShapeDtypeStruct((b, t, KV_LORA), F32),
                      jax.ShapeDtypeStruct((b, t, QK_ROPE), F32)]
    outs = pl.pallas_call(
        functools.partial(_mla_proj_kernel, rope=rope, emit_cache=emit_cache),
        grid=(b, t // tm),
        in_specs=in_specs,
        out_specs=out_specs,
        out_shape=out_shape,
        compiler_params=_params("arbitrary", "arbitrary"),
        name="mla_project",
    )(*args)
    return [o.reshape(b0, t0, o.shape[-1]) for o in outs]


def _kv_up_kernel(ckv_ref, kr2_ref, wuk_ref, wuv_ref, kcat_ref, v_ref):
    c = ckv_ref[0].astype(BF16)
    _store_k_cat(kcat_ref, _dot(c, wuk_ref[...]).astype(BF16), kr2_ref[0])
    v_ref[0] = _dot(c, wuv_ref[...]).astype(BF16)


def _kv_up(ckv, kr2, wuk, wuv):
    b, s, r = ckv.shape
    seq = lambda w: pl.BlockSpec((1, s, w), lambda i: (i, 0, 0))
    widths = [MLA_HEADS * QK_CAT, MLA_HEADS * V_HEAD]
    return pl.pallas_call(
        _kv_up_kernel,
        grid=(b,),
        in_specs=[seq(r), seq(kr2.shape[2]), _const_spec(wuk.shape, 1), _const_spec(wuv.shape, 1)],
        out_specs=[seq(w) for w in widths],
        out_shape=[jax.ShapeDtypeStruct((b, s, w), BF16) for w in widths],
        compiler_params=_params("arbitrary"),
        name="mla_cache_kv_up",
    )(ckv, kr2, wuk, wuv)


def _attn_kernel(*refs, n_seg):
    x_ref, mod_ref, lng_ref, lnb_ref, q_ref, wo_ref = refs[:6]
    seg_refs = [refs[6 + 2 * i:8 + 2 * i] for i in range(n_seg)]
    o_ref = refs[6 + 2 * n_seg]
    nt = (((1,), (1,)), ((), ()))
    exp2_scale = MLA_SCALE * math.log2(math.e)
    for sq in range(x_ref.shape[0]):
        heads = []
        for hd in range(MLA_HEADS):
            q = q_ref[sq, :, hd * QK_CAT:(hd + 1) * QK_CAT]
            scores = [lax.dot_general(q, k_ref[sq, :, hd * QK_CAT:(hd + 1) * QK_CAT], nt,
                                      preferred_element_type=F32) for k_ref, _ in seg_refs]
            m = functools.reduce(jnp.maximum, [jnp.max(s, axis=-1, keepdims=True) for s in scores])
            es = [jnp.exp2((s - m) * exp2_scale) for s in scores]
            total = functools.reduce(lambda a, b: a + b, [jnp.sum(e, axis=-1, keepdims=True) for e in es])
            o = None
            for e, (_, v_ref) in zip(es, seg_refs):
                part = _dot(e.astype(BF16), v_ref[sq, :, hd * V_HEAD:(hd + 1) * V_HEAD])
                o = part if o is None else o + part
            heads.append((o * (1.0 / total)).astype(BF16))
        mixed = _dot(jnp.concatenate(heads, axis=1), wo_ref[...])
        o_ref[sq] = _finish(x_ref[sq], mixed, mod_ref, lng_ref, lnb_ref)


def _attn_layer(x, cond, qcat, wo, segments):
    b, t, d = x.shape
    tq = min(ATTN_TILE, t)
    nq = 1 if cond.latent else max(1, SHORT_SEQ_ROWS // t)
    tok = lambda w: pl.BlockSpec((nq, tq, w), lambda i, j: (i, j, 0))
    in_specs = [tok(d), *_cond_specs(cond, 0, 2),
                tok(qcat.shape[2]), _const_spec(wo.shape, 2)]
    args = [x, *cond.arrays, qcat, wo]
    for seg in segments:
        for a in seg:
            in_specs.append(pl.BlockSpec((nq,) + a.shape[1:], lambda i, j: (i, 0, 0)))
            args.append(a)
    return pl.pallas_call(
        functools.partial(_attn_kernel, n_seg=len(segments)),
        grid=(b // nq, t // tq),
        in_specs=in_specs,
        out_specs=tok(d),
        out_shape=jax.ShapeDtypeStruct(x.shape, x.dtype),
        compiler_params=_params("arbitrary", "arbitrary"),
        name="mla_attention",
    )(*args)


def _lru_kernel(x_ref, mod_ref, lng_ref, lnb_ref, wu_ref, wy_ref, cw_ref, cb_ref, wg_ref, bg_ref,
                lam_ref, h0_ref, wout_ref, o_ref, st_ref,
                perm_scr, h_scr, acc_scr, pad_scr, af_scr, bf_scr, ab_scr, bb_scr):
    t_len = x_ref.shape[1]
    n_blk = t_len // SUBLANES
    pitch = n_blk + SEG_PITCH_PAD
    n_slab = D_MODEL // LANES
    cols = wu_ref.shape[1]
    j = pl.program_id(1)
    slab = lambda c: slice(c * LANES, (c + 1) * LANES)

    @pl.when(j == 0)
    def _():
        for c in range(n_slab):
            for s in range(SUBLANES):
                perm_scr[c, s * pitch:s * pitch + n_blk, :] = x_ref[0, s * n_blk:(s + 1) * n_blk, slab(c)]

        def gather(k, carry):
            r = pl.multiple_of(k * SUBLANES, SUBLANES)
            for c in range(n_slab):
                acc_scr[pl.ds(r, SUBLANES), slab(c)] = perm_scr[c, pl.ds(k, SUBLANES, stride=pitch), :]
            return carry

        lax.fori_loop(0, n_blk, gather, 0, unroll=2)
        h_scr[...] = _modulate(acc_scr[...], mod_ref).astype(BF16)
        acc_scr[...] = jnp.zeros_like(acc_scr)

    h = h_scr[...]
    u = _dot(h, wu_ref[...])
    y = _gelu_tanh(_dot(h, wy_ref[...]))

    sub = lax.broadcasted_iota(jnp.int32, (SUBLANES, cols), 0)
    lead = CONV_LEFT * SUBLANES
    for i in range(1, CONV_LEFT + 1):
        blk = u[(n_blk - i) * SUBLANES:(n_blk - i + 1) * SUBLANES]
        pad_scr[lead - i * SUBLANES:lead - (i - 1) * SUBLANES, :] = jnp.where(
            sub >= 1, pltpu.roll(blk, 1, axis=0), 0.0)
    pad_scr[lead:lead + t_len, :] = u
    for i in range(CONV_W - 1 - CONV_LEFT):
        blk = u[i * SUBLANES:(i + 1) * SUBLANES]
        pad_scr[lead + t_len + i * SUBLANES:lead + t_len + (i + 1) * SUBLANES, :] = jnp.where(
            sub < SUBLANES - 1, pltpu.roll(blk, SUBLANES - 1, axis=0), 0.0)
    uc = None
    for k in range(CONV_W):
        term = pad_scr[k * SUBLANES:k * SUBLANES + t_len, :] * cw_ref[k:k + 1, :]
        uc = term if uc is None else uc + term
    uc = uc + cb_ref[...]

    a_scr = (af_scr, ab_scr)
    b_scr = (bf_scr, bb_scr)
    for bl in range(cols // LRU_BW):
        c0 = bl * LRU_BW
        ub = uc[:, c0:c0 + LRU_BW]
        tg = jnp.tanh(_dot(ub.astype(BF16), wg_ref[bl]) + bg_ref[bl])
        for dr in range(2):
            tr = tg[:, dr * LRU_BW:(dr + 1) * LRU_BW]
            ti = tg[:, (2 + dr) * LRU_BW:(3 + dr) * LRU_BW]
            half_sp = (-0.25 * LRU_C) * jax.nn.softplus(-lam_ref[dr:dr + 1, c0:c0 + LRU_BW])
            t = jnp.tanh(half_sp * (1.0 + tr))
            inv = 1.0 / (1.0 - t)
            nt = -t
            root = jnp.where(nt > 0.0, nt * lax.rsqrt(nt), 0.0)
            a_scr[dr][:, c0:c0 + LRU_BW] = (1.0 + t) * inv
            b_scr[dr][:, c0:c0 + LRU_BW] = (root * inv) * ((1.0 + ti) * ub)

    def step(k, carry):
        a_f, b_f, a_b, b_b = carry
        rf = pl.multiple_of(k * SUBLANES, SUBLANES)
        rb = pl.multiple_of((n_blk - 1 - k) * SUBLANES, SUBLANES)
        a = af_scr[pl.ds(rf, SUBLANES), :]
        b_f = a * b_f + bf_scr[pl.ds(rf, SUBLANES), :]
        a_f = a * a_f
        af_scr[pl.ds(rf, SUBLANES), :] = a_f
        bf_scr[pl.ds(rf, SUBLANES), :] = b_f
        a = ab_scr[pl.ds(rb, SUBLANES), :]
        b_b = a * b_b + bb_scr[pl.ds(rb, SUBLANES), :]
        a_b = a * a_b
        ab_scr[pl.ds(rb, SUBLANES), :] = a_b
        bb_scr[pl.ds(rb, SUBLANES), :] = b_b
        return a_f, b_f, a_b, b_b

    one = jnp.ones((SUBLANES, cols), F32)
    zero = jnp.zeros((SUBLANES, cols), F32)
    a_f, b_f, a_b, b_b = lax.fori_loop(0, n_blk, step, (one, zero, one, zero), unroll=4)

    hf = h0_ref[0, 0:1, :]
    hf_in = zero
    for s in range(SUBLANES):
        hf_in = jnp.where(sub == s, hf, hf_in)
        hf = a_f[s:s + 1] * hf + b_f[s:s + 1]
    hb = h0_ref[0, 1:2, :]
    hb_in = zero
    for s in reversed(range(SUBLANES)):
        hb_in = jnp.where(sub == s, hb, hb_in)
        hb = a_b[s:s + 1] * hb + b_b[s:s + 1]
    st_ref[0, 0:1, :] = hf
    st_ref[0, 1:2, :] = hb

    blocks = lambda ref: ref[...].reshape(n_blk, SUBLANES, cols)
    h_sum = (blocks(af_scr) * hf_in + blocks(bf_scr)) + (blocks(ab_scr) * hb_in + blocks(bb_scr))
    mixed = (h_sum.reshape(t_len, cols) * y).astype(BF16)
    acc_scr[...] += _dot(mixed, wout_ref[...])

    @pl.when(j == pl.num_programs(1) - 1)
    def _():
        def scatter(k, carry):
            r = pl.multiple_of(k * SUBLANES, SUBLANES)
            for c in range(n_slab):
                perm_scr[c, pl.ds(k, SUBLANES, stride=pitch), :] = acc_scr[pl.ds(r, SUBLANES), slab(c)]
            return carry

        lax.fori_loop(0, n_blk, scatter, 0, unroll=2)
        for s in range(SUBLANES):
            rows = slice(s * n_blk, (s + 1) * n_blk)
            mixed_s = jnp.concatenate(
                [perm_scr[c, s * pitch:s * pitch + n_blk, :] for c in range(n_slab)], axis=1)
            o_ref[0, rows, :] = _finish(x_ref[0, rows, :], mixed_s, mod_ref, lng_ref, lnb_ref)


def _lru_layer(x, cond, w_in, conv_w, conv_b, wg, bg, lam, h0, w_out):
    b, t, d = x.shape
    cw = LRU_COLS if t > SHORT_SEQ_ROWS else D_RNN
    ncb = D_RNN // cw
    bpc = cw // LRU_BW
    pitch = t // SUBLANES + SEG_PITCH_PAD
    seq = lambda w: pl.BlockSpec((1, t, w), lambda i, j: (i, 0, 0))
    col = lambda rows: pl.BlockSpec((rows, cw), lambda i, j: (0, j))
    out, state = pl.pallas_call(
        _lru_kernel,
        grid=(b, ncb),
        in_specs=[
            seq(d),
            *_cond_specs(cond, 0, 2),
            pl.BlockSpec((d, cw), lambda i, j: (0, j)),
            pl.BlockSpec((d, cw), lambda i, j: (0, ncb + j)),
            col(CONV_W),
            col(1),
            pl.BlockSpec((bpc, LRU_BW, 4 * LRU_BW), lambda i, j: (j, 0, 0)),
            pl.BlockSpec((bpc, 1, 4 * LRU_BW), lambda i, j: (j, 0, 0)),
            col(2),
            pl.BlockSpec((1, 2, cw), lambda i, j: (i, 0, j)),
            pl.BlockSpec((cw, d), lambda i, j: (j, 0)),
        ],
        out_specs=[seq(d), pl.BlockSpec((1, 2, cw), lambda i, j: (i, 0, j))],
        out_shape=[jax.ShapeDtypeStruct(x.shape, x.dtype),
                   jax.ShapeDtypeStruct((b, 2, D_RNN), x.dtype)],
        scratch_shapes=[pltpu.VMEM((d // LANES, SUBLANES * pitch, LANES), F32),
                        pltpu.VMEM((t, d), BF16), pltpu.VMEM((t, d), F32),
                        pltpu.VMEM((t + (CONV_W - 1) * SUBLANES, cw), F32)]
                       + [pltpu.VMEM((t, cw), F32)] * 4,
        compiler_params=_params("arbitrary", "arbitrary"),
        name="lru_layer",
    )(x, *cond.arrays, w_in, w_in, conv_w, conv_b, wg, bg, lam, h0, w_out)
    return out, state


def _rope_rotate_cols(w):
    nf = ROPE_NF
    return jnp.concatenate([-w[..., nf:2 * nf], w[..., :nf], -w[..., 3 * nf:], w[..., 2 * nf:3 * nf]], axis=-1)


def _rope_tables(t_len):
    t = jnp.arange(t_len)
    rows = (t // GRID_W).astype(F32)
    cols = (t % GRID_W).astype(F32)
    inv = ROPE_THETA ** (-jnp.arange(ROPE_NF, dtype=F32) / ROPE_NF)
    ar, ac = rows[:, None] * inv, cols[:, None] * inv
    cos = jnp.concatenate([jnp.cos(ar), jnp.cos(ar), jnp.cos(ac), jnp.cos(ac)], axis=1)
    sin = jnp.concatenate([jnp.sin(ar), jnp.sin(ar), jnp.sin(ac), jnp.sin(ac)], axis=1)
    return jnp.tile(cos, (1, 2)), jnp.tile(sin, (1, 2))


def _mla_weights(w_dq, w_uq, w_dkv, w_uk, w_uv, w_o, rope):
    r = w_uq.shape[0]
    w_nope, w_rope = w_uq[:, :, :QK_NOPE], w_uq[:, :, QK_NOPE:]
    wq = jnp.concatenate([w_nope.reshape(r, -1), w_rope.reshape(r, -1)], axis=1).astype(BF16)
    wk = w_dkv[:, KV_LORA:]
    wdkv = [w_dkv[:, :KV_LORA], wk, wk]
    wq_rot = None
    if rope:
        wq_rot = _rope_rotate_cols(w_rope).reshape(r, -1).astype(BF16)
        wk_rot = _rope_rotate_cols(wk)
        wdkv += [wk_rot, wk_rot]
    return (w_dq.astype(BF16), wq, jnp.concatenate(wdkv, axis=1).astype(BF16),
            w_uk.reshape(KV_LORA, -1).astype(BF16), w_uv.reshape(KV_LORA, -1).astype(BF16),
            w_o.astype(BF16), wq_rot)


def _lru_gate_weights(w_a, b_a, w_i, b_i):
    wg = (0.5 * jnp.concatenate([w_a[0], w_a[1], w_i[0], w_i[1]], axis=-1)).astype(BF16)
    blk = lambda v: v.reshape(LRU_BLOCKS, 1, LRU_BW)
    bg = 0.5 * jnp.concatenate([blk(b_a[0]), blk(b_a[1]), blk(b_i[0]), blk(b_i[1])], axis=-1)
    return wg, bg


def kernel(x_prompt, x_sample, cache_mla_ckv, cache_mla_krope, state_lru, c, c_ctx, w_ada, b_ada, ln_g, ln_b, w_ffn_in, w_ffn_out, w_pool, pool_scale, w_dq, g_q, w_uq, w_dkv, g_kv, w_uk, w_uv, w_mla_o, w_lru_in, lru_conv_w, lru_conv_b, w_lru_a, b_lru_a, w_lru_i, b_lru_i, lru_lambda, w_lru_out):
    d = D_MODEL
    n_lat = c.shape[0]
    cond = jnp.concatenate([c_ctx[None], c, jnp.zeros((MOD_ROWS - 1 - n_lat, d), F32)], axis=0)
    mod = _modulation(cond, w_ada, b_ada)
    lng, lnb = ln_g.reshape(DEPTH * 2, 1, d), ln_b.reshape(DEPTH * 2, 1, d)

    xs = [x_prompt, x_sample]
    ckv_out, krope_out, lru_out = [], [], []
    cos, sin = _rope_tables(x_sample.shape[1])
    win, wout = w_ffn_in, w_ffn_out
    for i in range(DEPTH):
        kind, j = i % N_MIXERS, i // N_MIXERS
        row = lambda v: v.reshape(1, -1)
        if kind == 0:
            wp, ps = w_pool[j].astype(BF16), row(pool_scale[j])
        elif kind == 1:
            mla_w = [_mla_weights(w_dq[j], w_uq[j], w_dkv[j], w_uk[j], w_uv[j], w_mla_o[j], rope)
                     for rope in (False, True)]
            gq, gkv = row(g_q[j]), row(g_kv[j])
        else:
            w_in = w_lru_in[j].astype(BF16)
            wg, bg = _lru_gate_weights(w_lru_a[j], b_lru_a[j], w_lru_i[j], b_lru_i[j])
            w_out = w_lru_out[j].astype(BF16)
        for p in range(2):
            x, m = xs[p], _Cond(mod, lng, lnb, i, p == 1)
            if kind == 0:
                x = _pool_layer(x, m, wp, ps)
            elif kind == 1:
                wdq, wq, wdkv, wuk, wuv, wo, wq_rot = mla_w[p]
                if p == 0:
                    qcat, kcat, v, ckv, kr64 = _mla_project(
                        x, m, wdq, gq, wq, wdkv, gkv, wuk, wuv, None, True)
                    ckv_out.append(ckv)
                    krope_out.append(kr64)
                    segments = [(kcat, v)]
                else:
                    qcat, kcat, v = _mla_project(
                        x, m, wdq, gq, wq, wdkv, gkv, wuk, wuv, (wq_rot, cos, sin), False)
                    kr_c = cache_mla_krope[:, j]
                    kr_c = jnp.concatenate([kr_c, kr_c], axis=-1).astype(BF16)
                    kcat_c, v_c = _kv_up(cache_mla_ckv[:, j], kr_c, wuk, wuv)
                    segments = [(kcat_c, v_c), (kcat, v)]
                x = _attn_layer(x, m, qcat, wo, segments)
            else:
                if p == 0:
                    h0 = jnp.zeros((x.shape[0], 2, D_RNN), F32)
                else:
                    h0 = state_lru[:, j]
                x, st = _lru_layer(x, m, w_in, lru_conv_w[j], row(lru_conv_b[j]),
                                   wg, bg, lru_lambda[j], h0, w_out)
                if p == 0:
                    lru_out.append(st)
            xs[p] = _ffn_layer(x, m, win, wout)
    return (xs[0], xs[1], jnp.stack(ckv_out, axis=1), jnp.stack(krope_out, axis=1),
            jnp.stack(lru_out, axis=1))
```

```python
import functools
import math
from typing import NamedTuple

import jax
import jax.numpy as jnp
from jax import lax
from jax.experimental import pallas as pl
from jax.experimental.pallas import tpu as pltpu

F32 = jnp.float32
BF16 = jnp.bfloat16

D_MODEL = 1024
DEPTH = 4
GRID_W = 64
N_MIXERS = 3
POOL_WINDOWS = (2, 4, 8, 16)
POOL_GROUPS = 4
POOL_GC = D_MODEL // POOL_GROUPS
MLA_HEADS = 8
Q_LORA = 384
KV_LORA = 256
QK_NOPE = 128
QK_ROPE = 64
V_HEAD = 128
ROPE_NF = QK_ROPE // 4
ROPE_THETA = 10000.0
MLA_SCALE = (QK_NOPE + QK_ROPE) ** -0.5
D_RNN = D_MODEL
LRU_BLOCKS = 8
LRU_BW = D_RNN // LRU_BLOCKS
CONV_W = 4
CONV_LEFT = 1
LRU_C = 8.0
D_FF = ((8 * D_MODEL // 3 + 255) // 256) * 256
ALPHA = (2.0 * DEPTH) ** 0.25
EPS = 1e-6

LANES = 128
SUBLANES = 8
VMEM_LIMIT_BYTES = 56 * 1024 * 1024
PAD_ROWS = SUBLANES
MOD_ROWS = 16
MOD_CHUNKS = 2
MOD_RING = 3
LRU_COLS = 512
SEG_PITCH_PAD = 4
SHORT_SEQ_ROWS = 512
PROJ_TILE = 512
FFN_TILE = 512
ATTN_TILE = 512
QK_CAT = 2 * LANES


def _params(*sem):
    return pltpu.CompilerParams(dimension_semantics=sem, vmem_limit_bytes=VMEM_LIMIT_BYTES)


def _dot(a, b):
    return jnp.dot(a, b, preferred_element_type=F32)


def _layer_norm(y, g, b):
    mu = jnp.mean(y, axis=-1, keepdims=True)
    d = y - mu
    var = jnp.mean(d * d, axis=-1, keepdims=True)
    return d * lax.rsqrt(var + EPS) * g + b


def _gelu_tanh(x):
    c = math.sqrt(2.0 / math.pi)
    return (0.5 * x) * (1.0 + jnp.tanh(x * (c + (c * 0.044715) * (x * x))))


def _rms_norm(y, g):
    return y * lax.rsqrt(jnp.mean(y * y, axis=-1, keepdims=True) + EPS) * g


def _modulate(x, mod_ref):
    return x * (1.0 + mod_ref[1, 0]) + mod_ref[0, 0]


def _finish(x, mixed, mod_ref, lng_ref, lnb_ref):
    return _layer_norm(ALPHA * x + mod_ref[2, 0] * mixed, lng_ref[0], lnb_ref[0])


class _Cond(NamedTuple):
    mod: jax.Array
    lng: jax.Array
    lnb: jax.Array
    layer: int
    latent: bool

    @property
    def arrays(self):
        return self.mod, self.lng, self.lnb


def _cond_specs(cond, sub, nargs):
    k = 2 * cond.layer + sub
    row = (lambda b: 1 + b) if cond.latent else (lambda b: 0)
    if nargs == 1:
        mod = pl.BlockSpec((3, 1, 1, D_MODEL), lambda b: (k, row(b), 0, 0))
        ln = pl.BlockSpec((1, 1, D_MODEL), lambda b: (k, 0, 0))
    else:
        mod = pl.BlockSpec((3, 1, 1, D_MODEL), lambda b, t: (k, row(b), 0, 0))
        ln = pl.BlockSpec((1, 1, D_MODEL), lambda b, t: (k, 0, 0))
    return [mod, ln, ln]


def _const_spec(shape, nargs):
    zeros = (0,) * len(shape)
    if nargs == 1:
        return pl.BlockSpec(shape, lambda b: zeros)
    return pl.BlockSpec(shape, lambda b, t: zeros)


def _mod_kernel(cond_ref, w_hbm, b_ref, o_ref, w_ring, sems):
    nk = o_ref.shape[1]
    nj = 6 // nk
    step = pl.program_id(0) * nj + pl.program_id(1)
    n_steps = DEPTH * nj

    def block_copy(t):
        slot = t % MOD_RING
        cols = pl.ds(pl.multiple_of((t % nj) * (nk * D_MODEL), LANES), nk * D_MODEL)
        return pltpu.make_async_copy(w_hbm.at[t // nj, :, cols], w_ring.at[slot], sems.at[slot])

    @pl.when(step == 0)
    def _():
        for t in range(MOD_RING - 1):
            block_copy(t).start()

    @pl.when(step + (MOD_RING - 1) < n_steps)
    def _():
        block_copy(step + (MOD_RING - 1)).start()

    block_copy(step).wait()
    w_ref = w_ring.at[step % MOD_RING]
    s = jax.nn.silu(cond_ref[...]).astype(BF16)
    for k in range(nk):
        cols = slice(k * D_MODEL, (k + 1) * D_MODEL)
        res = _dot(s, w_ref[:, cols].astype(BF16)) + b_ref[0, k]
        for r in range(MOD_ROWS):
            o_ref[0, k, r] = res[r:r + 1]


def _modulation(cond, w_ada, b_ada):
    d = D_MODEL
    nk = MOD_CHUNKS
    return pl.pallas_call(
        _mod_kernel,
        grid=(DEPTH, 6 // nk),
        in_specs=[
            pl.BlockSpec((MOD_ROWS, d), lambda i, j: (0, 0)),
            pl.BlockSpec(memory_space=pl.ANY),
            pl.BlockSpec((1, nk, 1, d), lambda i, j: (i, j, 0, 0)),
        ],
        out_specs=pl.BlockSpec((1, nk, MOD_ROWS, 1, d), lambda i, j: (i, j, 0, 0, 0)),
        out_shape=jax.ShapeDtypeStruct((DEPTH, 6, MOD_ROWS, 1, d), F32),
        scratch_shapes=[pltpu.VMEM((MOD_RING, d, nk * d), F32), pltpu.SemaphoreType.DMA((MOD_RING,))],
        compiler_params=_params("arbitrary", "arbitrary"),
        name="adaln_modulation",
    )(cond, w_ada, b_ada.reshape(DEPTH, 6, 1, d)).reshape(DEPTH * 6, MOD_ROWS, 1, d)


def _ffn_kernel(x_ref, mod_ref, lng_ref, lnb_ref, win_ref, wout_hbm, o_ref, wout_vmem, sem, *, layer):
    fetch = pltpu.make_async_copy(wout_hbm.at[layer], wout_vmem, sem)
    fetch.start()
    x = x_ref[0]
    h = _modulate(x, mod_ref).astype(BF16)
    ab = _dot(h, win_ref[0].astype(BF16))
    fetch.wait()
    u = (jax.nn.silu(ab[:, :D_FF]) * ab[:, D_FF:]).astype(BF16)
    o_ref[0] = _finish(x, _dot(u, wout_vmem[...].astype(BF16)), mod_ref, lng_ref, lnb_ref)


def _ffn_layer(x, cond, win, wout):
    layer = cond.layer
    shape = x.shape
    d = shape[-1]
    if not cond.latent and shape[1] < FFN_TILE:
        x = x.reshape(-1, FFN_TILE, d)
    b, t, _ = x.shape
    tm = min(FFN_TILE, t)
    slab = lambda r, c: pl.BlockSpec((1, r, c), lambda i, j: (layer, 0, 0), pipeline_mode=pl.Buffered(1))
    return pl.pallas_call(
        functools.partial(_ffn_kernel, layer=layer),
        grid=(b, t // tm),
        in_specs=[
            pl.BlockSpec((1, tm, d), lambda i, j: (i, j, 0)),
            *_cond_specs(cond, 1, 2),
            slab(d, 2 * D_FF),
            pl.BlockSpec(memory_space=pl.ANY),
        ],
        out_specs=pl.BlockSpec((1, tm, d), lambda i, j: (i, j, 0)),
        out_shape=jax.ShapeDtypeStruct(x.shape, x.dtype),
        scratch_shapes=[pltpu.VMEM((D_FF, d), F32), pltpu.SemaphoreType.DMA(())],
        compiler_params=_params("arbitrary", "arbitrary"),
        name="ffn_layer",
    )(x, *cond.arrays, win, wout).reshape(shape)


def _pool_kernel(x_ref, mod_ref, lng_ref, lnb_ref, wp_ref, ps_ref, o_ref):
    for q in range(x_ref.shape[0]):
        o_ref[q] = _pool_sequence(x_ref[q], mod_ref, lng_ref, lnb_ref, wp_ref, ps_ref)


def _pool_sequence(x, mod_ref, lng_ref, lnb_ref, wp_ref, ps_ref):
    t_len = x.shape[0]
    n_pad = t_len + 2 * PAD_ROWS
    h = _modulate(x, mod_ref)
    edge = jnp.zeros((PAD_ROWS, POOL_GC), F32)
    edge_row = lax.broadcasted_iota(jnp.int32, (PAD_ROWS, 1), 0)
    outs = []
    for gi, w in enumerate(POOL_WINDOWS):
        assert w & (w - 1) == 0 and w // 2 <= PAD_ROWS
        hg = h[:, gi * POOL_GC:(gi + 1) * POOL_GC]
        run = jnp.concatenate([edge, hg, edge], axis=0)
        span = 1
        while 2 * span < w:
            run = run + pltpu.roll(run, n_pad - span, axis=0)
            span *= 2
        s = (run + pltpu.roll(run, span, axis=0))[PAD_ROWS:PAD_ROWS + t_len]
        lo, hi = -(w // 2), w - w // 2
        parts = []
        for r0 in (0, t_len - PAD_ROWS):
            t = r0 + edge_row
            cnt = (jnp.minimum(t + hi, t_len) - jnp.maximum(t + lo, 0)).astype(F32)
            parts.append(s[r0:r0 + PAD_ROWS] / cnt - hg[r0:r0 + PAD_ROWS])
        mid = s[PAD_ROWS:t_len - PAD_ROWS] * (1.0 / w) - hg[PAD_ROWS:t_len - PAD_ROWS]
        pooled = jnp.concatenate([parts[0], mid, parts[1]], axis=0)
        outs.append(_dot(pooled.astype(BF16), wp_ref[gi]))
    mixed = jnp.concatenate(outs, axis=1) * ps_ref[...]
    return _finish(x, mixed, mod_ref, lng_ref, lnb_ref)


def _pool_layer(x, cond, wp, ps):
    b, t, d = x.shape
    nq = 1 if cond.latent else max(1, SHORT_SEQ_ROWS // t)
    return pl.pallas_call(
        _pool_kernel,
        grid=(b // nq,),
        in_specs=[
            pl.BlockSpec((nq, t, d), lambda i: (i, 0, 0)),
            *_cond_specs(cond, 0, 1),
            _const_spec((POOL_GROUPS, POOL_GC, POOL_GC), 1),
            _const_spec((1, d), 1),
        ],
        out_specs=pl.BlockSpec((nq, t, d), lambda i: (i, 0, 0)),
        out_shape=jax.ShapeDtypeStruct(x.shape, x.dtype),
        compiler_params=_params("arbitrary"),
        name="pool_layer",
    )(x, *cond.arrays, wp, ps)


def _store_k_cat(kcat_ref, k_nope, k_rope2):
    for hd in range(MLA_HEADS):
        kcat_ref[0, :, hd * QK_CAT:hd * QK_CAT + QK_NOPE] = k_nope[:, hd * QK_NOPE:(hd + 1) * QK_NOPE]
        kcat_ref[0, :, hd * QK_CAT + QK_NOPE:(hd + 1) * QK_CAT] = k_rope2


def _mla_proj_kernel(*refs, rope, emit_cache):
    x_ref, mod_ref, wdq_ref, gq_ref, wq_ref, wdkv_ref, gkv_ref, wuk_ref, wuv_ref = refs[:9]
    refs = refs[9:]
    if rope:
        wqrot_ref, cos_ref, sin_ref = refs[:3]
        refs = refs[3:]
    qcat_ref, kcat_ref, v_ref = refs[:3]
    h = _modulate(x_ref[0], mod_ref).astype(BF16)
    q_lat = _rms_norm(_dot(h, wdq_ref[...]), gq_ref[...]).astype(BF16)
    nope_w = MLA_HEADS * QK_NOPE
    q = _dot(q_lat, wq_ref[...])
    kv = _dot(h, wdkv_ref[...])
    k_rope2 = kv[:, KV_LORA:KV_LORA + 2 * QK_ROPE]
    if rope:
        cos, sin = cos_ref[...], sin_ref[...]
        q_rot = _dot(q_lat, wqrot_ref[...])
        k_rope2 = k_rope2 * cos + kv[:, KV_LORA + 2 * QK_ROPE:] * sin
    lane = lax.broadcasted_iota(jnp.int32, (1, LANES), 1)
    for pair in range(MLA_HEADS // 2):
        both = q[:, nope_w + pair * LANES:nope_w + (pair + 1) * LANES]
        if rope:
            both = both * cos + q_rot[:, pair * LANES:(pair + 1) * LANES] * sin
        for hd, keep in ((2 * pair, lane < QK_ROPE), (2 * pair + 1, lane >= QK_ROPE)):
            c0 = hd * QK_CAT
            qcat_ref[0, :, c0:c0 + QK_NOPE] = q[:, hd * QK_NOPE:(hd + 1) * QK_NOPE].astype(BF16)
            qcat_ref[0, :, c0 + QK_NOPE:c0 + QK_CAT] = jnp.where(keep, both, 0.0).astype(BF16)
    c_kv = _rms_norm(kv[:, :KV_LORA], gkv_ref[...])
    c_kv16 = c_kv.astype(BF16)
    _store_k_cat(kcat_ref, _dot(c_kv16, wuk_ref[...]).astype(BF16), k_rope2.astype(BF16))
    v_ref[0] = _dot(c_kv16, wuv_ref[...]).astype(BF16)
    if emit_cache:
        ckv_out_ref, kr_out_ref = refs[3:5]
        ckv_out_ref[0] = c_kv
        kr_out_ref[0] = kv[:, KV_LORA:KV_LORA + QK_ROPE]


def _mla_project(x, cond, wdq, gq, wq, wdkv, gkv, wuk, wuv, rope_args, emit_cache):
    b0, t0, d = x.shape
    rope = rope_args is not None
    if not cond.latent and not rope and t0 < PROJ_TILE:
        x = x.reshape(-1, PROJ_TILE, d)
    b, t, _ = x.shape
    tm = min(PROJ_TILE, t)
    tok = lambda w: pl.BlockSpec((1, tm, w), lambda i, j: (i, j, 0))
    args = [x, cond.mod, wdq, gq, wq, wdkv, gkv, wuk, wuv]
    in_specs = [tok(d), _cond_specs(cond, 0, 2)[0]] + [_const_spec(a.shape, 2) for a in args[2:]]
    if rope:
        wqrot, cos, sin = rope_args
        args += [wqrot, cos, sin]
        in_specs += [_const_spec(wqrot.shape, 2)] + [pl.BlockSpec((tm, LANES), lambda i, j: (j, 0))] * 2
    widths = [MLA_HEADS * QK_CAT, MLA_HEADS * QK_CAT, MLA_HEADS * V_HEAD]
    out_specs = [tok(w) for w in widths]
    out_shape = [jax.ShapeDtypeStruct((b, t, w), BF16) for w in widths]
    if emit_cache:
        out_specs += [tok(KV_LORA), tok(QK_ROPE)]
        out_shape += [jax.ShapeDtypeStruct((b, t, KV_LORA), F32),
                      jax.ShapeDtypeStruct((b, t, QK_ROPE), F32)]
    outs = pl.pallas_call(
        functools.partial(_mla_proj_kernel, rope=rope, emit_cache=emit_cache),
        grid=(b, t // tm),
        in_specs=in_specs,
        out_specs=out_specs,
        out_shape=out_shape,
        compiler_params=_params("arbitrary", "arbitrary"),
        name="mla_project",
    )(*args)
    return [o.reshape(b0, t0, o.shape[-1]) for o in outs]


def _kv_up_kernel(ckv_ref, kr2_ref, wuk_ref, wuv_ref, kcat_ref, v_ref):
    c = ckv_ref[0].astype(BF16)
    _store_k_cat(kcat_ref, _dot(c, wuk_ref[...]).astype(BF16), kr2_ref[0])
    v_ref[0] = _dot(c, wuv_ref[...]).astype(BF16)


def _kv_up(ckv, kr2, wuk, wuv):
    b, s, r = ckv.shape
    seq = lambda w: pl.BlockSpec((1, s, w), lambda i: (i, 0, 0))
    widths = [MLA_HEADS * QK_CAT, MLA_HEADS * V_HEAD]
    return pl.pallas_call(
        _kv_up_kernel,
        grid=(b,),
        in_specs=[seq(r), seq(kr2.shape[2]), _const_spec(wuk.shape, 1), _const_spec(wuv.shape, 1)],
        out_specs=[seq(w) for w in widths],
        out_shape=[jax.ShapeDtypeStruct((b, s, w), BF16) for w in widths],
        compiler_params=_params("arbitrary"),
        name="mla_cache_kv_up",
    )(ckv, kr2, wuk, wuv)


def _attn_kernel(*refs, n_seg):
    x_ref, mod_ref, lng_ref, lnb_ref, q_ref, wo_ref = refs[:6]
    seg_refs = [refs[6 + 2 * i:8 + 2 * i] for i in range(n_seg)]
    o_ref = refs[6 + 2 * n_seg]
    nt = (((1,), (1,)), ((), ()))
    exp2_scale = MLA_SCALE * math.log2(math.e)
    for sq in range(x_ref.shape[0]):
        heads = []
        for hd in range(MLA_HEADS):
            q = q_ref[sq, :, hd * QK_CAT:(hd + 1) * QK_CAT]
            scores = [lax.dot_general(q, k_ref[sq, :, hd * QK_CAT:(hd + 1) * QK_CAT], nt,
                                      preferred_element_type=F32) for k_ref, _ in seg_refs]
            m = functools.reduce(jnp.maximum, [jnp.max(s, axis=-1, keepdims=True) for s in scores])
            es = [jnp.exp2((s - m) * exp2_scale) for s in scores]
            total = functools.reduce(lambda a, b: a + b, [jnp.sum(e, axis=-1, keepdims=True) for e in es])
            o = None
            for e, (_, v_ref) in zip(es, seg_refs):
                part = _dot(e.astype(BF16), v_ref[sq, :, hd * V_HEAD:(hd + 1) * V_HEAD])
                o = part if o is None else o + part
            heads.append((o * (1.0 / total)).astype(BF16))
        mixed = _dot(jnp.concatenate(heads, axis=1), wo_ref[...])
        o_ref[sq] = _finish(x_ref[sq], mixed, mod_ref, lng_ref, lnb_ref)


def _attn_layer(x, cond, qcat, wo, segments):
    b, t, d = x.shape
    tq = min(ATTN_TILE, t)
    nq = 1 if cond.latent else max(1, SHORT_SEQ_ROWS // t)
    tok = lambda w: pl.BlockSpec((nq, tq, w), lambda i, j: (i, j, 0))
    in_specs = [tok(d), *_cond_specs(cond, 0, 2),
                tok(qcat.shape[2]), _const_spec(wo.shape, 2)]
    args = [x, *cond.arrays, qcat, wo]
    for seg in segments:
        for a in seg:
            in_specs.append(pl.BlockSpec((nq,) + a.shape[1:], lambda i, j: (i, 0, 0)))
            args.append(a)
    return pl.pallas_call(
        functools.partial(_attn_kernel, n_seg=len(segments)),
        grid=(b // nq, t // tq),
        in_specs=in_specs,
        out_specs=tok(d),
        out_shape=jax.ShapeDtypeStruct(x.shape, x.dtype),
        compiler_params=_params("arbitrary", "arbitrary"),
        name="mla_attention",
    )(*args)


def _lru_kernel(x_ref, mod_ref, lng_ref, lnb_ref, wu_ref, wy_ref, cw_ref, cb_ref, wg_ref, bg_ref,
                lam_ref, h0_ref, wout_ref, o_ref, st_ref,
                perm_scr, h_scr, acc_scr, pad_scr, af_scr, bf_scr, ab_scr, bb_scr):
    t_len = x_ref.shape[1]
    n_blk = t_len // SUBLANES
    pitch = n_blk + SEG_PITCH_PAD
    n_slab = D_MODEL // LANES
    cols = wu_ref.shape[1]
    j = pl.program_id(1)
    slab = lambda c: slice(c * LANES, (c + 1) * LANES)

    @pl.when(j == 0)
    def _():
        for c in range(n_slab):
            for s in range(SUBLANES):
                perm_scr[c, s * pitch:s * pitch + n_blk, :] = x_ref[0, s * n_blk:(s + 1) * n_blk, slab(c)]

        def gather(k, carry):
            r = pl.multiple_of(k * SUBLANES, SUBLANES)
            for c in range(n_slab):
                acc_scr[pl.ds(r, SUBLANES), slab(c)] = perm_scr[c, pl.ds(k, SUBLANES, stride=pitch), :]
            return carry

        lax.fori_loop(0, n_blk, gather, 0, unroll=2)
        h_scr[...] = _modulate(acc_scr[...], mod_ref).astype(BF16)
        acc_scr[...] = jnp.zeros_like(acc_scr)

    h = h_scr[...]
    u = _dot(h, wu_ref[...])
    y = _gelu_tanh(_dot(h, wy_ref[...]))

    sub = lax.broadcasted_iota(jnp.int32, (SUBLANES, cols), 0)
    lead = CONV_LEFT * SUBLANES
    for i in range(1, CONV_LEFT + 1):
        blk = u[(n_blk - i) * SUBLANES:(n_blk - i + 1) * SUBLANES]
        pad_scr[lead - i * SUBLANES:lead - (i - 1) * SUBLANES, :] = jnp.where(
            sub >= 1, pltpu.roll(blk, 1, axis=0), 0.0)
    pad_scr[lead:lead + t_len, :] = u
    for i in range(CONV_W - 1 - CONV_LEFT):
        blk = u[i * SUBLANES:(i + 1) * SUBLANES]
        pad_scr[lead + t_len + i * SUBLANES:lead + t_len + (i + 1) * SUBLANES, :] = jnp.where(
            sub < SUBLANES - 1, pltpu.roll(blk, SUBLANES - 1, axis=0), 0.0)
    uc = None
    for k in range(CONV_W):
        term = pad_scr[k * SUBLANES:k * SUBLANES + t_len, :] * cw_ref[k:k + 1, :]
        uc = term if uc is None else uc + term
    uc = uc + cb_ref[...]

    a_scr = (af_scr, ab_scr)
    b_scr = (bf_scr, bb_scr)
    for bl in range(cols // LRU_BW):
        c0 = bl * LRU_BW
        ub = uc[:, c0:c0 + LRU_BW]
        tg = jnp.tanh(_dot(ub.astype(BF16), wg_ref[bl]) + bg_ref[bl])
        for dr in range(2):
            tr = tg[:, dr * LRU_BW:(dr + 1) * LRU_BW]
            ti = tg[:, (2 + dr) * LRU_BW:(3 + dr) * LRU_BW]
            half_sp = (-0.25 * LRU_C) * jax.nn.softplus(-lam_ref[dr:dr + 1, c0:c0 + LRU_BW])
            t = jnp.tanh(half_sp * (1.0 + tr))
            inv = 1.0 / (1.0 - t)
            nt = -t
            root = jnp.where(nt > 0.0, nt * lax.rsqrt(nt), 0.0)
            a_scr[dr][:, c0:c0 + LRU_BW] = (1.0 + t) * inv
            b_scr[dr][:, c0:c0 + LRU_BW] = (root * inv) * ((1.0 + ti) * ub)

    def step(k, carry):
        a_f, b_f, a_b, b_b = carry
        rf = pl.multiple_of(k * SUBLANES, SUBLANES)
        rb = pl.multiple_of((n_blk - 1 - k) * SUBLANES, SUBLANES)
        a = af_scr[pl.ds(rf, SUBLANES), :]
        b_f = a * b_f + bf_scr[pl.ds(rf, SUBLANES), :]
        a_f = a * a_f
        af_scr[pl.ds(rf, SUBLANES), :] = a_f
        bf_scr[pl.ds(rf, SUBLANES), :] = b_f
        a = ab_scr[pl.ds(rb, SUBLANES), :]
        b_b = a * b_b + bb_scr[pl.ds(rb, SUBLANES), :]
        a_b = a * a_b
        ab_scr[pl.ds(rb, SUBLANES), :] = a_b
        bb_scr[pl.ds(rb, SUBLANES), :] = b_b
        return a_f, b_f, a_b, b_b

    one = jnp.ones((SUBLANES, cols), F32)
    zero = jnp.zeros((SUBLANES, cols), F32)
    a_f, b_f, a_b, b_b = lax.fori_loop(0, n_blk, step, (one, zero, one, zero), unroll=4)

    hf = h0_ref[0, 0:1, :]
    hf_in = zero
    for s in range(SUBLANES):
        hf_in = jnp.where(sub == s, hf, hf_in)
        hf = a_f[s:s + 1] * hf + b_f[s:s + 1]
    hb = h0_ref[0, 1:2, :]
    hb_in = zero
    for s in reversed(range(SUBLANES)):
        hb_in = jnp.where(sub == s, hb, hb_in)
        hb = a_b[s:s + 1] * hb + b_b[s:s + 1]
    st_ref[0, 0:1, :] = hf
    st_ref[0, 1:2, :] = hb

    blocks = lambda ref: ref[...].reshape(n_blk, SUBLANES, cols)
    h_sum = (blocks(af_scr) * hf_in + blocks(bf_scr)) + (blocks(ab_scr) * hb_in + blocks(bb_scr))
    mixed = (h_sum.reshape(t_len, cols) * y).astype(BF16)
    acc_scr[...] += _dot(mixed, wout_ref[...])

    @pl.when(j == pl.num_programs(1) - 1)
    def _():
        def scatter(k, carry):
            r = pl.multiple_of(k * SUBLANES, SUBLANES)
            for c in range(n_slab):
                perm_scr[c, pl.ds(k, SUBLANES, stride=pitch), :] = acc_scr[pl.ds(r, SUBLANES), slab(c)]
            return carry

        lax.fori_loop(0, n_blk, scatter, 0, unroll=2)
        for s in range(SUBLANES):
            rows = slice(s * n_blk, (s + 1) * n_blk)
            mixed_s = jnp.concatenate(
                [perm_scr[c, s * pitch:s * pitch + n_blk, :] for c in range(n_slab)], axis=1)
            o_ref[0, rows, :] = _finish(x_ref[0, rows, :], mixed_s, mod_ref, lng_ref, lnb_ref)


def _lru_layer(x, cond, w_in, conv_w, conv_b, wg, bg, lam, h0, w_out):
    b, t, d = x.shape
    cw = LRU_COLS if t > SHORT_SEQ_ROWS else D_RNN
    ncb = D_RNN // cw
    bpc = cw // LRU_BW
    pitch = t // SUBLANES + SEG_PITCH_PAD
    seq = lambda w: pl.BlockSpec((1, t, w), lambda i, j: (i, 0, 0))
    col = lambda rows: pl.BlockSpec((rows, cw), lambda i, j: (0, j))
    out, state = pl.pallas_call(
        _lru_kernel,
        grid=(b, ncb),
        in_specs=[
            seq(d),
            *_cond_specs(cond, 0, 2),
            pl.BlockSpec((d, cw), lambda i, j: (0, j)),
            pl.BlockSpec((d, cw), lambda i, j: (0, ncb + j)),
            col(CONV_W),
            col(1),
            pl.BlockSpec((bpc, LRU_BW, 4 * LRU_BW), lambda i, j: (j, 0, 0)),
            pl.BlockSpec((bpc, 1, 4 * LRU_BW), lambda i, j: (j, 0, 0)),
            col(2),
            pl.BlockSpec((1, 2, cw), lambda i, j: (i, 0, j)),
            pl.BlockSpec((cw, d), lambda i, j: (j, 0)),
        ],
        out_specs=[seq(d), pl.BlockSpec((1, 2, cw), lambda i, j: (i, 0, j))],
        out_shape=[jax.ShapeDtypeStruct(x.shape, x.dtype),
                   jax.ShapeDtypeStruct((b, 2, D_RNN), x.dtype)],
        scratch_shapes=[pltpu.VMEM((d // LANES, SUBLANES * pitch, LANES), F32),
                        pltpu.VMEM((t, d), BF16), pltpu.VMEM((t, d), F32),
                        pltpu.VMEM((t + (CONV_W - 1) * SUBLANES, cw), F32)]
                       + [pltpu.VMEM((t, cw), F32)] * 4,
        compiler_params=_params("arbitrary", "arbitrary"),
        name="lru_layer",
    )(x, *cond.arrays, w_in, w_in, conv_w, conv_b, wg, bg, lam, h0, w_out)
    return out, state


def _rope_rotate_cols(w):
    nf = ROPE_NF
    return jnp.concatenate([-w[..., nf:2 * nf], w[..., :nf], -w[..., 3 * nf:], w[..., 2 * nf:3 * nf]], axis=-1)


def _rope_tables(t_len):
    t = jnp.arange(t_len)
    rows = (t // GRID_W).astype(F32)
    cols = (t % GRID_W).astype(F32)
    inv = ROPE_THETA ** (-jnp.arange(ROPE_NF, dtype=F32) / ROPE_NF)
    ar, ac = rows[:, None] * inv, cols[:, None] * inv
    cos = jnp.concatenate([jnp.cos(ar), jnp.cos(ar), jnp.cos(ac), jnp.cos(ac)], axis=1)
    sin = jnp.concatenate([jnp.sin(ar), jnp.sin(ar), jnp.sin(ac), jnp.sin(ac)], axis=1)
    return jnp.tile(cos, (1, 2)), jnp.tile(sin, (1, 2))


def _mla_weights(w_dq, w_uq, w_dkv, w_uk, w_uv, w_o, rope):
    r = w_uq.shape[0]
    w_nope, w_rope = w_uq[:, :, :QK_NOPE], w_uq[:, :, QK_NOPE:]
    wq = jnp.concatenate([w_nope.reshape(r, -1), w_rope.reshape(r, -1)], axis=1).astype(BF16)
    wk = w_dkv[:, KV_LORA:]
    wdkv = [w_dkv[:, :KV_LORA], wk, wk]
    wq_rot = None
    if rope:
        wq_rot = _rope_rotate_cols(w_rope).reshape(r, -1).astype(BF16)
        wk_rot = _rope_rotate_cols(wk)
        wdkv += [wk_rot, wk_rot]
    return (w_dq.astype(BF16), wq, jnp.concatenate(wdkv, axis=1).astype(BF16),
            w_uk.reshape(KV_LORA, -1).astype(BF16), w_uv.reshape(KV_LORA, -1).astype(BF16),
            w_o.astype(BF16), wq_rot)


def _lru_gate_weights(w_a, b_a, w_i, b_i):
    wg = (0.5 * jnp.concatenate([w_a[0], w_a[1], w_i[0], w_i[1]], axis=-1)).astype(BF16)
    blk = lambda v: v.reshape(LRU_BLOCKS, 1, LRU_BW)
    bg = 0.5 * jnp.concatenate([blk(b_a[0]), blk(b_a[1]), blk(b_i[0]), blk(b_i[1])], axis=-1)
    return wg, bg


def kernel(x_prompt, x_sample, cache_mla_ckv, cache_mla_krope, state_lru, c, c_ctx, w_ada, b_ada, ln_g, ln_b, w_ffn_in, w_ffn_out, w_pool, pool_scale, w_dq, g_q, w_uq, w_dkv, g_kv, w_uk, w_uv, w_mla_o, w_lru_in, lru_conv_w, lru_conv_b, w_lru_a, b_lru_a, w_lru_i, b_lru_i, lru_lambda, w_lru_out):
    d = D_MODEL
    n_lat = c.shape[0]
    cond = jnp.concatenate([c_ctx[None], c, jnp.zeros((MOD_ROWS - 1 - n_lat, d), F32)], axis=0)
    mod = _modulation(cond, w_ada, b_ada)
    lng, lnb = ln_g.reshape(DEPTH * 2, 1, d), ln_b.reshape(DEPTH * 2, 1, d)

    xs = [x_prompt, x_sample]
    ckv_out, krope_out, lru_out = [], [], []
    cos, sin = _rope_tables(x_sample.shape[1])
    win, wout = w_ffn_in, w_ffn_out
    for i in range(DEPTH):
        kind, j = i % N_MIXERS, i // N_MIXERS
        row = lambda v: v.reshape(1, -1)
        if kind == 0:
            wp, ps = w_pool[j].astype(BF16), row(pool_scale[j])
        elif kind == 1:
            mla_w = [_mla_weights(w_dq[j], w_uq[j], w_dkv[j], w_uk[j], w_uv[j], w_mla_o[j], rope)
                     for rope in (False, True)]
            gq, gkv = row(g_q[j]), row(g_kv[j])
        else:
            w_in = w_lru_in[j].astype(BF16)
            wg, bg = _lru_gate_weights(w_lru_a[j], b_lru_a[j], w_lru_i[j], b_lru_i[j])
            w_out = w_lru_out[j].astype(BF16)
        for p in range(2):
            x, m = xs[p], _Cond(mod, lng, lnb, i, p == 1)
            if kind == 0:
                x = _pool_layer(x, m, wp, ps)
            elif kind == 1:
                wdq, wq, wdkv, wuk, wuv, wo, wq_rot = mla_w[p]
                if p == 0:
                    qcat, kcat, v, ckv, kr64 = _mla_project(
                        x, m, wdq, gq, wq, wdkv, gkv, wuk, wuv, None, True)
                    ckv_out.append(ckv)
                    krope_out.append(kr64)
                    segments = [(kcat, v)]
                else:
                    qcat, kcat, v = _mla_project(
                        x, m, wdq, gq, wq, wdkv, gkv, wuk, wuv, (wq_rot, cos, sin), False)
                    kr_c = cache_mla_krope[:, j]
                    kr_c = jnp.concatenate([kr_c, kr_c], axis=-1).astype(BF16)
                    kcat_c, v_c = _kv_up(cache_mla_ckv[:, j], kr_c, wuk, wuv)
                    segments = [(kcat_c, v_c), (kcat, v)]
                x = _attn_layer(x, m, qcat, wo, segments)
            else:
                if p == 0:
                    h0 = jnp.zeros((x.shape[0], 2, D_RNN), F32)
                else:
                    h0 = state_lru[:, j]
                x, st = _lru_layer(x, m, w_in, lru_conv_w[j], row(lru_conv_b[j]),
                                   wg, bg, lru_lambda[j], h0, w_out)
                if p == 0:
                    lru_out.append(st)
            xs[p] = _ffn_layer(x, m, win, wout)
    return (xs[0], xs[1], jnp.stack(ckv_out, axis=1), jnp.stack(krope_out, axis=1),
            jnp.stack(lru_out, axis=1))
```
